```python
import math
import jax, jax.numpy as jnp
from jax import lax
import numpy as np

D_MODEL = 2048
BATCH = 4
SEQ = 2048
DEPTH = 2

CHUNK = 64

CONV_WIDTH = D_MODEL
CONV_K = 3
SSM_WIDTH = D_MODEL // 2
SSM_GROUP = 16
SSM_GROUPS = SSM_WIDTH // SSM_GROUP
SSM_STATE = 64
DT_MIN = 0.001
DT_MAX = 0.1
RMS_EPS = 1e-6

_SIZES = (CONV_WIDTH, CONV_WIDTH, CONV_WIDTH, CONV_WIDTH,
          SSM_WIDTH, SSM_WIDTH,
          D_MODEL, D_MODEL)
N_IN = int(sum(_SIZES))
SPLITS = tuple(int(s) for s in np.cumsum(_SIZES)[:-1])

kernel_name = "hybrid_conv_s5_gated_encoder"


def rmsnorm(x, g):
    x32 = x.astype(jnp.float32)
    y = x32 * lax.rsqrt(jnp.mean(x32 * x32, axis=-1, keepdims=True) + RMS_EPS)
    return (y * g.astype(jnp.float32)).astype(x.dtype)


def causal_depthwise_conv(v, w):
    L = v.shape[1]
    vp = jnp.pad(v, ((0, 0), (CONV_K - 1, 0), (0, 0)))
    out = w[0] * vp[:, 0:L]
    for k in range(1, CONV_K):
        out = out + w[k] * vp[:, k:k + L]
    return out


def s5_scan(u, a_re, a_im, log_dt, b_re, b_im, c_re, c_im, d_skip):
    bsz, L, _ = u.shape
    f32 = jnp.float32
    ug = u.astype(f32).reshape(bsz, L, SSM_GROUPS, SSM_GROUP)
    lam = lax.complex(a_re.astype(f32), a_im.astype(f32))
    dt = jnp.exp(log_dt.astype(f32))[:, None]
    lam_bar = jnp.exp(lam * dt)
    b = lax.complex(b_re.astype(f32), b_im.astype(f32))
    b_bar = ((lam_bar - 1.0) / lam)[..., None] * b
    bu = jnp.einsum('gpc,blgc->blgp', b_bar, ug.astype(jnp.complex64))
    a = jnp.broadcast_to(lam_bar, bu.shape)

    def combine(e1, e2):
        a1, h1 = e1
        a2, h2 = e2
        return a1 * a2, a2 * h1 + h2

    _, h = lax.associative_scan(combine, (a, bu), axis=1)
    c = lax.complex(c_re.astype(f32), c_im.astype(f32))
    y = jnp.einsum('gcp,blgp->blgc', c, h).real + d_skip.astype(f32) * ug
    return y.reshape(bsz, L, SSM_WIDTH)


def hybrid_layer(x, norm_g, w_in, conv_w, w_out_a, a_re, a_im, log_dt, b_re, b_im,
                 c_re, c_im, d_skip, w_glu, b_glu, w_out_b, w_o):
    h = rmsnorm(x, norm_g)
    proj = jnp.einsum('bld,dn->bln', h, w_in)
    v, bg, cg, za, u, zb, ga, gb = jnp.split(proj, SPLITS, axis=-1)
    ya = bg * causal_depthwise_conv(cg * v, conv_w)
    ya = jnp.einsum('blc,cd->bld', ya * jax.nn.silu(za), w_out_a)
    yb = jax.nn.gelu(s5_scan(u, a_re, a_im, log_dt, b_re, b_im, c_re, c_im, d_skip))
    yb = yb * jax.nn.sigmoid(jnp.einsum('blc,ce->ble', yb, w_glu.astype(jnp.float32))
                             + b_glu.astype(jnp.float32))
    yb = yb.astype(x.dtype)
    yb = jnp.einsum('blc,cd->bld', yb * jax.nn.silu(zb), w_out_b)
    m = jax.nn.sigmoid(ga) * ya + jax.nn.sigmoid(gb) * yb
    return x + jnp.einsum('bld,de->ble', m, w_o)


def setup_inputs(seed: int = 0) -> dict:
    key = jax.random.key(seed)
    ks = jax.random.split(key, 20)
    f32 = jnp.float32
    n = lambda k, shape, s: jax.random.normal(k, shape, f32) * s
    res_scale = 1.0 / math.sqrt(2.0 * DEPTH)
    G, P, c = SSM_GROUPS, SSM_STATE, SSM_GROUP
    a_im_base = math.pi * jnp.arange(P, dtype=f32)
    return {
        "x": n(ks[0], (BATCH, SEQ, D_MODEL), 1.0),
        "norm_g": 1.0 + n(ks[1], (DEPTH, D_MODEL), 0.02),
        "w_in": n(ks[2], (DEPTH, D_MODEL, N_IN), D_MODEL ** -0.5),
        "conv_w": n(ks[3], (DEPTH, CONV_K, CONV_WIDTH), CONV_K ** -0.5),
        "w_out_a": n(ks[4], (DEPTH, CONV_WIDTH, D_MODEL), CONV_WIDTH ** -0.5),
        "a_re": -0.5 + n(ks[5], (DEPTH, G, P), 0.01),
        "a_im": a_im_base + n(ks[6], (DEPTH, G, P), 0.01),
        "log_dt": jax.random.uniform(ks[7], (DEPTH, G), f32,
                                     math.log(DT_MIN), math.log(DT_MAX)),
        "b_re": n(ks[8], (DEPTH, G, P, c), (2.0 * c) ** -0.5),
        "b_im": n(ks[9], (DEPTH, G, P, c), (2.0 * c) ** -0.5),
        "c_re": n(ks[10], (DEPTH, G, c, P), (2.0 * P) ** -0.5),
        "c_im": n(ks[11], (DEPTH, G, c, P), (2.0 * P) ** -0.5),
        "d_skip": n(ks[12], (DEPTH, G, c), 1.0),
        "w_glu": n(ks[13], (DEPTH, SSM_WIDTH, SSM_WIDTH), SSM_WIDTH ** -0.5),
        "b_glu": n(ks[14], (DEPTH, SSM_WIDTH), 0.02),
        "w_out_b": n(ks[15], (DEPTH, SSM_WIDTH, D_MODEL), SSM_WIDTH ** -0.5),
        "w_o": n(ks[16], (DEPTH, D_MODEL, D_MODEL), D_MODEL ** -0.5 * res_scale),
        "final_g": 1.0 + n(ks[17], (D_MODEL,), 0.02),
    }


def reference(x, norm_g, w_in, conv_w, w_out_a, a_re, a_im, log_dt, b_re, b_im,
              c_re, c_im, d_skip, w_glu, b_glu, w_out_b, w_o, final_g):
    for i in range(DEPTH):
        x = hybrid_layer(x, norm_g[i], w_in[i], conv_w[i], w_out_a[i], a_re[i], a_im[i],
                         log_dt[i], b_re[i], b_im[i], c_re[i], c_im[i], d_skip[i],
                         w_glu[i], b_glu[i], w_out_b[i], w_o[i])
    return rmsnorm(x, final_g)
```

```python
import functools

import jax
import jax.numpy as jnp
from jax import lax
from jax.experimental import pallas as pl
from jax.experimental.pallas import tpu as pltpu

F32 = jnp.float32
BF16 = jnp.bfloat16

D_MODEL = 2048
CONV_WIDTH = D_MODEL
CONV_K = 3
SSM_WIDTH = D_MODEL // 2
SSM_GROUP = 16
SSM_GROUPS = SSM_WIDTH // SSM_GROUP
SSM_STATE = 64
RMS_EPS = 1e-6

OFF_V = 0
OFF_BG = CONV_WIDTH
OFF_CG = 2 * CONV_WIDTH
OFF_ZA = 3 * CONV_WIDTH
OFF_U = 4 * CONV_WIDTH
OFF_ZB = OFF_U + SSM_WIDTH
OFF_G = OFF_ZB + SSM_WIDTH

SSM_CHUNK = 16
CHUNK_LANES = SSM_CHUNK * SSM_GROUP
STATE_LANES = 128
GROUPS_PER_STEP = 8

V7X_VMEM_LIMIT = 56 * 1024 * 1024


def _params(*sem):
    return pltpu.CompilerParams(dimension_semantics=sem, vmem_limit_bytes=V7X_VMEM_LIMIT)


def _rmsnorm_kernel(x_ref, g_ref, o_ref):
    x = x_ref[...]
    ms = jnp.mean(x * x, axis=-1, keepdims=True)
    o_ref[...] = (x * lax.rsqrt(ms + RMS_EPS) * g_ref[...]).astype(o_ref.dtype)


def _rmsnorm(x, g, tm=512):
    n, d = x.shape
    return pl.pallas_call(
        _rmsnorm_kernel,
        grid=(n // tm,),
        in_specs=[pl.BlockSpec((tm, d), lambda i: (i, 0)),
                  pl.BlockSpec((1, d), lambda i: (0, 0))],
        out_specs=pl.BlockSpec((tm, d), lambda i: (i, 0)),
        out_shape=jax.ShapeDtypeStruct((n, d), BF16),
        compiler_params=_params("arbitrary"),
        name="rmsnorm",
    )(x, g.reshape(1, d))


def _branch_a_kernel(h_ref, wv_ref, wbg_ref, wcg_ref, wza_ref, cw_ref, o_ref, carry_ref,
                     *, tiles_per_seq):
    i = pl.program_id(1)

    @pl.when(i % tiles_per_seq == 0)
    def _():
        carry_ref[...] = jnp.zeros_like(carry_ref)

    h = h_ref[...]
    v = jnp.dot(h, wv_ref[...], preferred_element_type=F32)
    bg = jnp.dot(h, wbg_ref[...], preferred_element_type=F32)
    cg = jnp.dot(h, wcg_ref[...], preferred_element_type=F32)
    za = jnp.dot(h, wza_ref[...], preferred_element_type=F32)
    cv = cg * v
    tm = cv.shape[0]
    row = lax.broadcasted_iota(jnp.int32, cv.shape, 0)
    prev1 = carry_ref[7:8, :]
    prev2 = carry_ref[6:7, :]
    cv1 = jnp.where(row == 0, prev1, pltpu.roll(cv, 1, axis=0))
    cv2 = jnp.where(row == 0, prev2, jnp.where(row == 1, prev1, pltpu.roll(cv, 2, axis=0)))
    cw = cw_ref[...]
    conv = cw[0:1, :] * cv2 + cw[1:2, :] * cv1 + cw[2:3, :] * cv
    o_ref[...] = (bg * conv * (za * jax.nn.sigmoid(za))).astype(o_ref.dtype)
    carry_ref[...] = cv[tm - 8:tm, :]


def _branch_a(h, w_in, conv_w, seq_len, tm=512, tn=256):
    n, d = h.shape
    nj = CONV_WIDTH // tn

    def wspec(off):
        return pl.BlockSpec((d, tn), lambda j, i, o=off // tn: (0, o + j))

    return pl.pallas_call(
        functools.partial(_branch_a_kernel, tiles_per_seq=seq_len // tm),
        grid=(nj, n // tm),
        in_specs=[pl.BlockSpec((tm, d), lambda j, i: (i, 0)),
                  wspec(OFF_V), wspec(OFF_BG), wspec(OFF_CG), wspec(OFF_ZA),
                  pl.BlockSpec((CONV_K, tn), lambda j, i: (0, j))],
        out_specs=pl.BlockSpec((tm, tn), lambda j, i: (i, j)),
        out_shape=jax.ShapeDtypeStruct((n, CONV_WIDTH), BF16),
        scratch_shapes=[pltpu.VMEM((8, tn), F32)],
        compiler_params=_params("arbitrary", "arbitrary"),
        name="branch_a_proj_conv",
    )(h, w_in, w_in, w_in, w_in, conv_w)


def _proj_kernel(h_ref, w_ref, o_ref, *, act):
    y = jnp.dot(h_ref[...], w_ref[...], preferred_element_type=F32)
    if act == "silu":
        y = y * jax.nn.sigmoid(y)
    elif act == "sigmoid":
        y = jax.nn.sigmoid(y)
    o_ref[...] = y.astype(o_ref.dtype)


def _proj(h, w_in, col_off, width, act, out_dtype, tm=512, tn=512):
    n, d = h.shape
    return pl.pallas_call(
        functools.partial(_proj_kernel, act=act),
        grid=(width // tn, n // tm),
        in_specs=[pl.BlockSpec((tm, d), lambda j, i: (i, 0)),
                  pl.BlockSpec((d, tn), lambda j, i, o=col_off // tn: (0, o + j))],
        out_specs=pl.BlockSpec((tm, tn), lambda j, i: (i, j)),
        out_shape=jax.ShapeDtypeStruct((n, width), out_dtype),
        compiler_params=_params("arbitrary", "arbitrary"),
        name="proj_" + act,
    )(h, w_in)


def _ssm_kernel(u_ref, m0_ref, win_ref, wout_ref, lre_ref, lim_ref, y_ref,
                sre_ref, sim_ref, hre_ref, him_ref, *, n_chunks, batch):
    gb = u_ref.shape[0]
    for g in range(gb):
        s = jnp.dot(u_ref[g], win_ref[g], preferred_element_type=F32)
        sre_ref[:, g * STATE_LANES:(g + 1) * STATE_LANES] = s[:, :STATE_LANES]
        sim_ref[:, g * STATE_LANES:(g + 1) * STATE_LANES] = s[:, STATE_LANES:]
    lre = lre_ref[0]
    lim = lim_ref[0]
    rows = 2 * batch
    first = lax.broadcasted_iota(jnp.int32, (rows, gb * STATE_LANES), 0) < batch

    def advance(h_r, h_i, s_r, s_i):
        return lre * h_r - lim * h_i + s_r, lre * h_i + lim * h_r + s_i

    def step(j, carry):
        h_r, h_i = carry
        r0 = pl.multiple_of(j * rows, rows)
        s_r = sre_ref[pl.ds(r0, rows), :]
        s_i = sim_ref[pl.ds(r0, rows), :]
        n_r, n_i = advance(h_r, h_i, s_r, s_i)
        h_r = jnp.where(first, h_r, pltpu.roll(n_r, batch, axis=0))
        h_i = jnp.where(first, h_i, pltpu.roll(n_i, batch, axis=0))
        hre_ref[pl.ds(r0, rows), :] = h_r
        him_ref[pl.ds(r0, rows), :] = h_i
        n_r, n_i = advance(h_r, h_i, s_r, s_i)
        return pltpu.roll(n_r, batch, axis=0), pltpu.roll(n_i, batch, axis=0)

    zero = jnp.zeros((rows, gb * STATE_LANES), F32)
    lax.fori_loop(0, n_chunks // 2, step, (zero, zero))

    for g in range(gb):
        h = jnp.concatenate([hre_ref[:, g * STATE_LANES:(g + 1) * STATE_LANES],
                             him_ref[:, g * STATE_LANES:(g + 1) * STATE_LANES]], axis=1)
        y = jnp.dot(u_ref[g], m0_ref[g], preferred_element_type=F32)
        y = y + jnp.dot(h.astype(BF16), wout_ref[g], preferred_element_type=F32)
        y_ref[g] = y


def _ssm(u_r, m0, w_in_state, w_out_state, lam_re, lam_im, batch):
    g, n_inst, _ = u_r.shape
    gb = GROUPS_PER_STEP
    wspec = pl.BlockSpec((gb, CHUNK_LANES, 2 * STATE_LANES), lambda i: (i, 0, 0))
    return pl.pallas_call(
        functools.partial(_ssm_kernel, n_chunks=n_inst // batch, batch=batch),
        grid=(g // gb,),
        in_specs=[pl.BlockSpec((gb, n_inst, CHUNK_LANES), lambda i: (i, 0, 0)),
                  pl.BlockSpec((gb, CHUNK_LANES, CHUNK_LANES), lambda i: (i, 0, 0)),
                  wspec,
                  pl.BlockSpec((gb, 2 * STATE_LANES, CHUNK_LANES), lambda i: (i, 0, 0)),
                  pl.BlockSpec((1, 1, gb * STATE_LANES), lambda i: (i, 0, 0)),
                  pl.BlockSpec((1, 1, gb * STATE_LANES), lambda i: (i, 0, 0))],
        out_specs=pl.BlockSpec((gb, n_inst, CHUNK_LANES), lambda i: (i, 0, 0)),
        out_shape=jax.ShapeDtypeStruct((g, n_inst, CHUNK_LANES), F32),
        scratch_shapes=[pltpu.VMEM((n_inst, gb * STATE_LANES), F32) for _ in range(4)],
        compiler_params=_params("arbitrary"),
        name="s5_chunked_scan",
    )(u_r, m0, w_in_state, w_out_state, lam_re, lam_im)


def _ssm_weights(a_re, a_im, log_dt, b_re, b_im, c_re, c_im):
    t = SSM_CHUNK
    hi = lax.Precision.HIGHEST
    lam = lax.complex(a_re.astype(F32), a_im.astype(F32))
    dt = jnp.exp(log_dt.astype(F32))[:, None]
    lam_bar = jnp.exp(lam * dt)
    b_bar = ((lam_bar - 1.0) / lam)[..., None] * lax.complex(b_re.astype(F32), b_im.astype(F32))
    c = lax.complex(c_re.astype(F32), c_im.astype(F32))
    taus = jnp.arange(t + 1, dtype=F32)
    pw = jnp.exp(taus[:, None, None] * (lam * dt)[None])

    def cmm(x_re, x_im, y_re, y_im, spec):
        re = jnp.einsum(spec, x_re, y_re, precision=hi) - jnp.einsum(spec, x_im, y_im, precision=hi)
        im = jnp.einsum(spec, x_re, y_im, precision=hi) + jnp.einsum(spec, x_im, y_re, precision=hi)
        return re, im

    cp = c[None] * pw[:t, :, None, :]
    k_re, _ = cmm(cp.real, cp.imag, b_bar.real, b_bar.imag, "tgdp,gpc->tgdc")
    s_idx = jnp.arange(t)[:, None]
    t_idx = jnp.arange(t)[None, :]
    tau = t_idx - s_idx
    kk = k_re[jnp.clip(tau, 0, t - 1)]
    kk = jnp.where((tau >= 0)[:, :, None, None, None], kk, 0.0)
    m0 = kk.transpose(2, 0, 4, 1, 3).reshape(SSM_GROUPS, CHUNK_LANES, CHUNK_LANES)

    pin = pw[t - 1 - jnp.arange(t)]
    win = pin[:, :, :, None] * b_bar[None]
    win = win.transpose(1, 0, 3, 2).reshape(SSM_GROUPS, CHUNK_LANES, SSM_STATE)
    zpad = jnp.zeros((SSM_GROUPS, CHUNK_LANES, STATE_LANES - SSM_STATE), F32)
    w_in_state = jnp.concatenate([win.real, zpad, win.imag, zpad], axis=-1)

    z = c[None] * pw[1:t + 1, :, None, :]
    z = z.transpose(1, 3, 0, 2).reshape(SSM_GROUPS, SSM_STATE, CHUNK_LANES)
    zpad = jnp.zeros((SSM_GROUPS, STATE_LANES - SSM_STATE, CHUNK_LANES), F32)
    w_out_state = jnp.concatenate([z.real, zpad, -z.imag, zpad], axis=1)

    lam_t = pw[t]
    lpad = jnp.zeros((SSM_GROUPS, STATE_LANES - SSM_STATE), F32)
    shp = (SSM_GROUPS // GROUPS_PER_STEP, 1, GROUPS_PER_STEP * STATE_LANES)
    lam_re = jnp.concatenate([lam_t.real, lpad], axis=-1).reshape(shp)
    lam_im = jnp.concatenate([lam_t.imag, lpad], axis=-1).reshape(shp)
    return m0.astype(BF16), w_in_state.astype(BF16), w_out_state.astype(BF16), lam_re, lam_im


def _post_b_kernel(y_ref, u_ref, szb_ref, d_ref, wg_ref, bgl_ref, o_ref):
    yb = jax.nn.gelu(y_ref[...] + d_ref[...] * u_ref[...])
    z = jnp.dot(yb.astype(BF16), wg_ref[...], preferred_element_type=F32) + bgl_ref[...]
    o_ref[...] = (yb * jax.nn.sigmoid(z) * szb_ref[...].astype(F32)).astype(o_ref.dtype)


def _post_b(y, u, szb, d_skip, w_glu, b_glu, tm=512):
    n, w = y.shape
    row = pl.BlockSpec((tm, w), lambda i: (i, 0))
    vec = pl.BlockSpec((1, w), lambda i: (0, 0))
    return pl.pallas_call(
        _post_b_kernel,
        grid=(n // tm,),
        in_specs=[row, row, row, vec, pl.BlockSpec((w, w), lambda i: (0, 0)), vec],
        out_specs=row,
        out_shape=jax.ShapeDtypeStruct((n, w), BF16),
        compiler_params=_params("arbitrary"),
        name="branch_b_glu",
    )(y, u, szb, d_skip.reshape(1, w), w_glu, b_glu.reshape(1, w))


def _merge_kernel(a_ref, b_ref, wa_ref, wb_ref, sga_ref, sgb_ref, o_ref):
    ya = jnp.dot(a_ref[...], wa_ref[...], preferred_element_type=F32)
    yb = jnp.dot(b_ref[...], wb_ref[...], preferred_element_type=F32)
    o_ref[...] = (sga_ref[...].astype(F32) * ya + sgb_ref[...].astype(F32) * yb).astype(o_ref.dtype)


def _merge(a_in, b_in, w_a, w_b, sg, tm=512, tn=512):
    n = a_in.shape[0]
    nj = D_MODEL // tn
    return pl.pallas_call(
        _merge_kernel,
        grid=(nj, n // tm),
        in_specs=[pl.BlockSpec((tm, CONV_WIDTH), lambda j, i: (i, 0)),
                  pl.BlockSpec((tm, SSM_WIDTH), lambda j, i: (i, 0)),
                  pl.BlockSpec((CONV_WIDTH, tn), lambda j, i: (0, j)),
                  pl.BlockSpec((SSM_WIDTH, tn), lambda j, i: (0, j)),
                  pl.BlockSpec((tm, tn), lambda j, i: (i, j)),
                  pl.BlockSpec((tm, tn), lambda j, i, o=nj: (i, o + j))],
        out_specs=pl.BlockSpec((tm, tn), lambda j, i: (i, j)),
        out_shape=jax.ShapeDtypeStruct((n, D_MODEL), BF16),
        compiler_params=_params("arbitrary", "arbitrary"),
        name="gated_merge",
    )(a_in, b_in, w_a, w_b, sg, sg)


def _out_kernel(m_ref, wo_ref, x_ref, g_ref, *o_refs, emit_x):
    xn = x_ref[...] + jnp.dot(m_ref[...], wo_ref[...], preferred_element_type=F32)
    ms = jnp.mean(xn * xn, axis=-1, keepdims=True)
    normed = xn * lax.rsqrt(ms + RMS_EPS) * g_ref[...]
    if emit_x:
        o_refs[0][...] = xn
        o_refs[1][...] = normed.astype(o_refs[1].dtype)
    else:
        o_refs[0][...] = normed.astype(o_refs[0].dtype)


def _out_proj(m, w_o, x, g, emit_x, tm=256):
    n, d = x.shape
    row = pl.BlockSpec((tm, d), lambda i: (i, 0))
    if emit_x:
        out_specs = [row, row]
        out_shape = [jax.ShapeDtypeStruct((n, d), F32), jax.ShapeDtypeStruct((n, d), BF16)]
    else:
        out_specs = row
        out_shape = jax.ShapeDtypeStruct((n, d), F32)
    return pl.pallas_call(
        functools.partial(_out_kernel, emit_x=emit_x),
        grid=(n // tm,),
        in_specs=[row, pl.BlockSpec((d, d), lambda i: (0, 0)), row,
                  pl.BlockSpec((1, d), lambda i: (0, 0))],
        out_specs=out_specs,
        out_shape=out_shape,
        compiler_params=_params("arbitrary"),
        name="out_proj_residual_norm",
    )(m, w_o, x, g.reshape(1, d))


def kernel(x, norm_g, w_in, conv_w, w_out_a, a_re, a_im, log_dt, b_re, b_im, c_re, c_im,
           d_skip, w_glu, b_glu, w_out_b, w_o, final_g):
    bsz, seq, d = x.shape
    depth = norm_g.shape[0]
    n = bsz * seq
    n_chunks = seq // SSM_CHUNK
    assert 2 * bsz == 8 and n_chunks % 2 == 0
    xf = x.reshape(n, d)
    h = _rmsnorm(xf, norm_g[0])
    for l in range(depth):
        w_in_l = w_in[l].astype(BF16)
        a_in = _branch_a(h, w_in_l, conv_w[l], seq)
        u = _proj(h, w_in_l, OFF_U, SSM_WIDTH, "none", F32)
        szb = _proj(h, w_in_l, OFF_ZB, SSM_WIDTH, "silu", BF16)
        sg = _proj(h, w_in_l, OFF_G, 2 * D_MODEL, "sigmoid", BF16)

        m0, w_si, w_so, lam_re, lam_im = _ssm_weights(
            a_re[l], a_im[l], log_dt[l], b_re[l], b_im[l], c_re[l], c_im[l])
        u_r = u.astype(BF16).reshape(bsz, n_chunks, SSM_CHUNK, SSM_GROUPS, SSM_GROUP)
        u_r = u_r.transpose(3, 1, 0, 2, 4).reshape(SSM_GROUPS, n_chunks * bsz, CHUNK_LANES)
        y_r = _ssm(u_r, m0, w_si, w_so, lam_re, lam_im, bsz)
        y = y_r.reshape(SSM_GROUPS, n_chunks, bsz, SSM_CHUNK, SSM_GROUP)
        y = y.transpose(2, 1, 3, 0, 4).reshape(n, SSM_WIDTH)

        b_in = _post_b(y, u, szb, d_skip[l].reshape(-1), w_glu[l].astype(BF16), b_glu[l])
        m = _merge(a_in, b_in, w_out_a[l].astype(BF16), w_out_b[l].astype(BF16), sg)
        if l + 1 < depth:
            xf, h = _out_proj(m, w_o[l].astype(BF16), xf, norm_g[l + 1], True)
        else:
            out = _out_proj(m, w_o[l].astype(BF16), xf, final_g, False)
    return out.reshape(bsz, seq, d)
```

```python
import functools

import jax
import jax.numpy as jnp
from jax import lax
from jax.experimental import pallas as pl
from jax.experimental.pallas import tpu as pltpu

F32 = jnp.float32
BF16 = jnp.bfloat16

D_MODEL = 2048
CONV_WIDTH = D_MODEL
CONV_K = 3
SSM_WIDTH = D_MODEL // 2
SSM_GROUP = 16
SSM_GROUPS = SSM_WIDTH // SSM_GROUP
SSM_STATE = 64
RMS_EPS = 1e-6

OFF_V = 0
OFF_BG = CONV_WIDTH
OFF_CG = 2 * CONV_WIDTH
OFF_ZA = 3 * CONV_WIDTH
OFF_U = 4 * CONV_WIDTH
OFF_ZB = OFF_U + SSM_WIDTH
OFF_G = OFF_ZB + SSM_WIDTH

SSM_CHUNK = 16
CHUNK_LANES = SSM_CHUNK * SSM_GROUP
STATE_LANES = 128
GROUPS_PER_STEP = 8
PREP_ROWS = 24

V7X_VMEM_LIMIT = 56 * 1024 * 1024


def _params(*sem):
    return pltpu.CompilerParams(dimension_semantics=sem, vmem_limit_bytes=V7X_VMEM_LIMIT)


def _rmsnorm_kernel(x_ref, g_ref, o_ref):
    x = x_ref[...]
    ms = jnp.mean(x * x, axis=-1, keepdims=True)
    o_ref[...] = (x * lax.rsqrt(ms + RMS_EPS) * g_ref[...]).astype(o_ref.dtype)


def _rmsnorm(x, g, tm=512):
    n, d = x.shape
    return pl.pallas_call(
        _rmsnorm_kernel,
        grid=(n // tm,),
        in_specs=[pl.BlockSpec((tm, d), lambda i: (i, 0)),
                  pl.BlockSpec((1, d), lambda i: (0, 0))],
        out_specs=pl.BlockSpec((tm, d), lambda i: (i, 0)),
        out_shape=jax.ShapeDtypeStruct((n, d), BF16),
        compiler_params=_params("arbitrary"),
        name="rmsnorm",
    )(x, g.reshape(1, d))


def _branch_a_kernel(h_ref, wv_ref, wbg_ref, wcg_ref, wza_ref, cw_ref, o_ref, carry_ref, w_ref,
                     *, tiles_per_seq):
    i = pl.program_id(1)

    @pl.when(i == 0)
    def _():
        for k, src in enumerate((wv_ref, wbg_ref, wcg_ref, wza_ref)):
            w_ref[k] = src[...].astype(BF16)

    @pl.when(i % tiles_per_seq == 0)
    def _():
        carry_ref[...] = jnp.zeros_like(carry_ref)

    h = h_ref[...]
    v = jnp.dot(h, w_ref[0], preferred_element_type=F32)
    bg = jnp.dot(h, w_ref[1], preferred_element_type=F32)
    cg = jnp.dot(h, w_ref[2], preferred_element_type=F32)
    za = jnp.dot(h, w_ref[3], preferred_element_type=F32)
    cv = cg * v
    tm = cv.shape[0]
    row = lax.broadcasted_iota(jnp.int32, cv.shape, 0)
    prev1 = carry_ref[7:8, :]
    prev2 = carry_ref[6:7, :]
    cv1 = jnp.where(row == 0, prev1, pltpu.roll(cv, 1, axis=0))
    cv2 = jnp.where(row == 0, prev2, jnp.where(row == 1, prev1, pltpu.roll(cv, 2, axis=0)))
    cw = cw_ref[...]
    conv = cw[0:1, :] * cv2 + cw[1:2, :] * cv1 + cw[2:3, :] * cv
    o_ref[...] = (bg * conv * (za * jax.nn.sigmoid(za))).astype(o_ref.dtype)
    carry_ref[...] = cv[tm - 8:tm, :]


def _branch_a(h, w_in, conv_w, layer, seq_len, tm=512, tn=256):
    n, d = h.shape
    nj = CONV_WIDTH // tn

    def wspec(off):
        return pl.BlockSpec((None, d, tn), lambda j, i, o=off // tn: (layer, 0, o + j))

    return pl.pallas_call(
        functools.partial(_branch_a_kernel, tiles_per_seq=seq_len // tm),
        grid=(nj, n // tm),
        in_specs=[pl.BlockSpec((tm, d), lambda j, i: (i, 0)),
                  wspec(OFF_V), wspec(OFF_BG), wspec(OFF_CG), wspec(OFF_ZA),
                  pl.BlockSpec((None, CONV_K, tn), lambda j, i: (layer, 0, j))],
        out_specs=pl.BlockSpec((tm, tn), lambda j, i: (i, j)),
        out_shape=jax.ShapeDtypeStruct((n, CONV_WIDTH), BF16),
        scratch_shapes=[pltpu.VMEM((8, tn), F32), pltpu.VMEM((4, d, tn), BF16)],
        compiler_params=_params("arbitrary", "arbitrary"),
        name="branch_a_proj_conv",
    )(h, w_in, w_in, w_in, w_in, conv_w)


def _proj_kernel(h_ref, w32_ref, o_ref, w_ref, *, act):
    @pl.when(pl.program_id(1) == 0)
    def _():
        w_ref[...] = w32_ref[...].astype(BF16)

    y = jnp.dot(h_ref[...], w_ref[...], preferred_element_type=F32)
    if act == "silu":
        y = y * jax.nn.sigmoid(y)
    elif act == "sigmoid":
        y = jax.nn.sigmoid(y)
    o_ref[...] = y.astype(o_ref.dtype)


def _proj(h, w_in, layer, col_off, width, act, out_dtype, tm=512, tn=512):
    n, d = h.shape
    return pl.pallas_call(
        functools.partial(_proj_kernel, act=act),
        grid=(width // tn, n // tm),
        in_specs=[pl.BlockSpec((tm, d), lambda j, i: (i, 0)),
                  pl.BlockSpec((None, d, tn), lambda j, i, o=col_off // tn: (layer, 0, o + j))],
        out_specs=pl.BlockSpec((tm, tn), lambda j, i: (i, j)),
        out_shape=jax.ShapeDtypeStruct((n, width), out_dtype),
        scratch_shapes=[pltpu.VMEM((d, tn), BF16)],
        compiler_params=_params("arbitrary", "arbitrary"),
        name="proj_" + act,
    )(h, w_in)


def _ssm_kernel(u_ref, m0_ref, win_ref, wout_ref, lre_ref, lim_ref, y_ref,
                sre_ref, sim_ref, hre_ref, him_ref, *, n_chunks, batch):
    gb = u_ref.shape[0]
    for g in range(gb):
        s = jnp.dot(u_ref[g], win_ref[g], preferred_element_type=F32)
        sre_ref[:, g * STATE_LANES:(g + 1) * STATE_LANES] = s[:, :STATE_LANES]
        sim_ref[:, g * STATE_LANES:(g + 1) * STATE_LANES] = s[:, STATE_LANES:]
    lre = lre_ref[0]
    lim = lim_ref[0]
    rows = 2 * batch
    first = lax.broadcasted_iota(jnp.int32, (rows, gb * STATE_LANES), 0) < batch

    def advance(h_r, h_i, s_r, s_i):
        return lre * h_r - lim * h_i + s_r, lre * h_i + lim * h_r + s_i

    def step(j, carry):
        h_r, h_i = carry
        r0 = pl.multiple_of(j * rows, rows)
        s_r = sre_ref[pl.ds(r0, rows), :]
        s_i = sim_ref[pl.ds(r0, rows), :]
        n_r, n_i = advance(h_r, h_i, s_r, s_i)
        h_r = jnp.where(first, h_r, pltpu.roll(n_r, batch, axis=0))
        h_i = jnp.where(first, h_i, pltpu.roll(n_i, batch, axis=0))
        hre_ref[pl.ds(r0, rows), :] = h_r
        him_ref[pl.ds(r0, rows), :] = h_i
        n_r, n_i = advance(h_r, h_i, s_r, s_i)
        return pltpu.roll(n_r, batch, axis=0), pltpu.roll(n_i, batch, axis=0)

    zero = jnp.zeros((rows, gb * STATE_LANES), F32)
    lax.fori_loop(0, n_chunks // 2, step, (zero, zero))

    for g in range(gb):
        h = jnp.concatenate([hre_ref[:, g * STATE_LANES:(g + 1) * STATE_LANES],
                             him_ref[:, g * STATE_LANES:(g + 1) * STATE_LANES]], axis=1)
        y = jnp.dot(u_ref[g], m0_ref[g], preferred_element_type=F32)
        y = y + jnp.dot(h.astype(BF16), wout_ref[g], preferred_element_type=F32)
        y_ref[g] = y


def _ssm(u_r, m0, w_in_state, w_out_state, lam_re, lam_im, batch):
    g, n_inst, _ = u_r.shape
    gb = GROUPS_PER_STEP
    wspec = pl.BlockSpec((gb, CHUNK_LANES, 2 * STATE_LANES), lambda i: (i, 0, 0))
    return pl.pallas_call(
        functools.partial(_ssm_kernel, n_chunks=n_inst // batch, batch=batch),
        grid=(g // gb,),
        in_specs=[pl.BlockSpec((gb, n_inst, CHUNK_LANES), lambda i: (i, 0, 0)),
                  pl.BlockSpec((gb, CHUNK_LANES, CHUNK_LANES), lambda i: (i, 0, 0)),
                  wspec,
                  pl.BlockSpec((gb, 2 * STATE_LANES, CHUNK_LANES), lambda i: (i, 0, 0)),
                  pl.BlockSpec((1, 1, gb * STATE_LANES), lambda i: (i, 0, 0)),
                  pl.BlockSpec((1, 1, gb * STATE_LANES), lambda i: (i, 0, 0))],
        out_specs=pl.BlockSpec((gb, n_inst, CHUNK_LANES), lambda i: (i, 0, 0)),
        out_shape=jax.ShapeDtypeStruct((g, n_inst, CHUNK_LANES), F32),
        scratch_shapes=[pltpu.VMEM((n_inst, gb * STATE_LANES), F32) for _ in range(4)],
        compiler_params=_params("arbitrary"),
        name="s5_chunked_scan",
    )(u_r, m0, w_in_state, w_out_state, lam_re, lam_im)


def _split_bf16(x):
    hi = x.astype(BF16)
    return hi, (x - hi.astype(F32)).astype(BF16)


def _dot_nt_f32(a, b):
    dn = (((1,), (1,)), ((), ()))
    a_hi, a_lo = _split_bf16(a)
    b_hi, b_lo = _split_bf16(b)

    def d(x, y):
        return lax.dot_general(x, y, dn, preferred_element_type=F32)

    return d(a_hi, b_hi) + d(a_hi, b_lo) + d(a_lo, b_hi)


def _ssm_prep_kernel(are_ref, aim_ref, ldt_ref, btr_ref, bti_ref, cre_ref, cim_ref,
                     m0_ref, win_ref, wout_ref, lre_ref, lim_ref):
    gb = are_ref.shape[0]
    t = SSM_CHUNK
    tau = lax.broadcasted_iota(jnp.int32, (PREP_ROWS, STATE_LANES), 0).astype(F32)
    lane = lax.broadcasted_iota(jnp.int32, (SSM_GROUP, CHUNK_LANES), 1)
    for g in range(gb):
        are = are_ref[g]
        aim = aim_ref[g]
        dt = jnp.exp(ldt_ref[g])
        mag = jnp.exp(tau * (dt * are))
        ang = tau * (dt * aim)
        pw_re = mag * jnp.cos(ang)
        pw_im = mag * jnp.sin(ang)
        nr = pw_re[1:2] - 1.0
        ni = pw_im[1:2]
        den = are * are + aim * aim
        cr = (nr * are + ni * aim) / den
        ci = (ni * are - nr * aim) / den
        btr = btr_ref[g]
        bti = bti_ref[g]
        bb_re = cr * btr - ci * bti
        bb_im = cr * bti + ci * btr
        cre = cre_ref[g]
        cim = cim_ref[g]

        win_re, win_im = [], []
        for s in range(t):
            pr = pw_re[t - 1 - s:t - s]
            pi = pw_im[t - 1 - s:t - s]
            win_re.append(pr * bb_re - pi * bb_im)
            win_im.append(pr * bb_im + pi * bb_re)
        win_ref[g] = jnp.concatenate(
            [jnp.concatenate(win_re, axis=0), jnp.concatenate(win_im, axis=0)], axis=1).astype(BF16)

        z_re, z_im = [], []
        for k in range(t + 1):
            pr = pw_re[k:k + 1]
            pi = pw_im[k:k + 1]
            z_re.append(cre * pr - cim * pi)
            z_im.append(cre * pi + cim * pr)

        zo_re = jnp.concatenate(z_re[1:], axis=0)
        zo_im = jnp.concatenate(z_im[1:], axis=0)
        wout_ref[g] = jnp.concatenate([zo_re.T, -(zo_im.T)], axis=0).astype(BF16)

        zk_re = jnp.concatenate(z_re[:t], axis=0)
        zk_im = jnp.concatenate(z_im[:t], axis=0)
        kt = _dot_nt_f32(bb_re, zk_re) - _dot_nt_f32(bb_im, zk_im)
        blocks = [kt]
        for s in range(1, t):
            shifted = pltpu.roll(kt, s * SSM_GROUP, axis=1)
            blocks.append(jnp.where(lane >= s * SSM_GROUP, shifted, 0.0))
        m0_ref[g] = jnp.concatenate(blocks, axis=0).astype(BF16)

        lre_ref[0, :, g * STATE_LANES:(g + 1) * STATE_LANES] = pw_re[t:t + 1]
        lim_ref[0, :, g * STATE_LANES:(g + 1) * STATE_LANES] = pw_im[t:t + 1]


def _ssm_weights(a_re, a_im, log_dt, b_re, b_im, c_re, c_im):
    g, p = a_re.shape
    gb = GROUPS_PER_STEP
    pad = STATE_LANES - p

    def lanes(x, value=0.0):
        return jnp.pad(x.astype(F32), [(0, 0)] * (x.ndim - 1) + [(0, pad)], constant_values=value)

    are = lanes(a_re, -1.0).reshape(g, 1, STATE_LANES)
    aim = lanes(a_im).reshape(g, 1, STATE_LANES)
    ldt = jnp.broadcast_to(log_dt.astype(F32)[:, None, None], (g, 1, STATE_LANES))
    btr = lanes(jnp.swapaxes(b_re, 1, 2))
    bti = lanes(jnp.swapaxes(b_im, 1, 2))
    cre = lanes(c_re)
    cim = lanes(c_im)
    vec = pl.BlockSpec((gb, 1, STATE_LANES), lambda i: (i, 0, 0))
    mat = pl.BlockSpec((gb, SSM_GROUP, STATE_LANES), lambda i: (i, 0, 0))
    sq = pl.BlockSpec((gb, CHUNK_LANES, CHUNK_LANES), lambda i: (i, 0, 0))
    row = pl.BlockSpec((1, 1, gb * STATE_LANES), lambda i: (i, 0, 0))
    sq_shape = jax.ShapeDtypeStruct((g, CHUNK_LANES, CHUNK_LANES), BF16)
    row_shape = jax.ShapeDtypeStruct((g // gb, 1, gb * STATE_LANES), F32)
    return pl.pallas_call(
        _ssm_prep_kernel,
        grid=(g // gb,),
        in_specs=[vec, vec, vec, mat, mat, mat, mat],
        out_specs=[sq, sq, sq, row, row],
        out_shape=[sq_shape, sq_shape, sq_shape, row_shape, row_shape],
        compiler_params=_params("arbitrary"),
        name="s5_chunk_operators",
    )(are, aim, ldt, btr, bti, cre, cim)


def _post_b_kernel(y_ref, u_ref, szb_ref, d_ref, wg32_ref, bgl_ref, o_ref, wg_ref):
    @pl.when(pl.program_id(0) == 0)
    def _():
        wg_ref[...] = wg32_ref[...].astype(BF16)

    yb = jax.nn.gelu(y_ref[...] + d_ref[...] * u_ref[...])
    z = jnp.dot(yb.astype(BF16), wg_ref[...], preferred_element_type=F32) + bgl_ref[...]
    o_ref[...] = (yb * jax.nn.sigmoid(z) * szb_ref[...].astype(F32)).astype(o_ref.dtype)


def _post_b(y, u, szb, d_skip, w_glu, b_glu, layer, tm=512):
    n, w = y.shape
    row = pl.BlockSpec((tm, w), lambda i: (i, 0))
    vec = pl.BlockSpec((None, 1, w), lambda i: (layer, 0, 0))
    return pl.pallas_call(
        _post_b_kernel,
        grid=(n // tm,),
        in_specs=[row, row, row, vec, pl.BlockSpec((None, w, w), lambda i: (layer, 0, 0)), vec],
        out_specs=row,
        out_shape=jax.ShapeDtypeStruct((n, w), BF16),
        scratch_shapes=[pltpu.VMEM((w, w), BF16)],
        compiler_params=_params("arbitrary"),
        name="branch_b_glu",
    )(y, u, szb, d_skip.reshape(-1, 1, w), w_glu, b_glu.reshape(-1, 1, w))


def _merge_kernel(a_ref, b_ref, wa32_ref, wb32_ref, sga_ref, sgb_ref, o_ref, wa_ref, wb_ref):
    @pl.when(pl.program_id(1) == 0)
    def _():
        wa_ref[...] = wa32_ref[...].astype(BF16)
        wb_ref[...] = wb32_ref[...].astype(BF16)

    ya = jnp.dot(a_ref[...], wa_ref[...], preferred_element_type=F32)
    yb = jnp.dot(b_ref[...], wb_ref[...], preferred_element_type=F32)
    o_ref[...] = (sga_ref[...].astype(F32) * ya + sgb_ref[...].astype(F32) * yb).astype(o_ref.dtype)


def _merge(a_in, b_in, w_a, w_b, sg, layer, tm=512, tn=512):
    n = a_in.shape[0]
    nj = D_MODEL // tn
    return pl.pallas_call(
        _merge_kernel,
        grid=(nj, n // tm),
        in_specs=[pl.BlockSpec((tm, CONV_WIDTH), lambda j, i: (i, 0)),
                  pl.BlockSpec((tm, SSM_WIDTH), lambda j, i: (i, 0)),
                  pl.BlockSpec((None, CONV_WIDTH, tn), lambda j, i: (layer, 0, j)),
                  pl.BlockSpec((None, SSM_WIDTH, tn), lambda j, i: (layer, 0, j)),
                  pl.BlockSpec((tm, tn), lambda j, i: (i, j)),
                  pl.BlockSpec((tm, tn), lambda j, i, o=nj: (i, o + j))],
        out_specs=pl.BlockSpec((tm, tn), lambda j, i: (i, j)),
        out_shape=jax.ShapeDtypeStruct((n, D_MODEL), BF16),
        scratch_shapes=[pltpu.VMEM((CONV_WIDTH, tn), BF16), pltpu.VMEM((SSM_WIDTH, tn), BF16)],
        compiler_params=_params("arbitrary", "arbitrary"),
        name="gated_merge",
    )(a_in, b_in, w_a, w_b, sg, sg)


def _out_kernel(m_ref, wo32_ref, x_ref, g_ref, *refs, emit_x):
    o_refs, wo_ref = refs[:-1], refs[-1]

    @pl.when(pl.program_id(0) == 0)
    def _():
        wo_ref[...] = wo32_ref[...].astype(BF16)

    xn = x_ref[...] + jnp.dot(m_ref[...], wo_ref[...], preferred_element_type=F32)
    ms = jnp.mean(xn * xn, axis=-1, keepdims=True)
    normed = xn * lax.rsqrt(ms + RMS_EPS) * g_ref[...]
    if emit_x:
        o_refs[0][...] = xn
        o_refs[1][...] = normed.astype(o_refs[1].dtype)
    else:
        o_refs[0][...] = normed.astype(o_refs[0].dtype)


def _out_proj(m, w_o, layer, x, g, emit_x, tm=256):
    n, d = x.shape
    row = pl.BlockSpec((tm, d), lambda i: (i, 0))
    if emit_x:
        out_specs = [row, row]
        out_shape = [jax.ShapeDtypeStruct((n, d), F32), jax.ShapeDtypeStruct((n, d), BF16)]
    else:
        out_specs = row
        out_shape = jax.ShapeDtypeStruct((n, d), F32)
    w_spec = pl.BlockSpec((None, d, d), lambda i: (layer, 0, 0), pipeline_mode=pl.Buffered(1))
    return pl.pallas_call(
        functools.partial(_out_kernel, emit_x=emit_x),
        grid=(n // tm,),
        in_specs=[row, w_spec, row, pl.BlockSpec((1, d), lambda i: (0, 0))],
        out_specs=out_specs,
        out_shape=out_shape,
        scratch_shapes=[pltpu.VMEM((d, d), BF16)],
        compiler_params=_params("arbitrary"),
        name="out_proj_residual_norm",
    )(m, w_o, x, g.reshape(1, d))


def kernel(x, norm_g, w_in, conv_w, w_out_a, a_re, a_im, log_dt, b_re, b_im, c_re, c_im,
           d_skip, w_glu, b_glu, w_out_b, w_o, final_g):
    bsz, seq, d = x.shape
    depth = norm_g.shape[0]
    n = bsz * seq
    n_chunks = seq // SSM_CHUNK
    assert 2 * bsz == 8 and n_chunks % 2 == 0
    xf = x.reshape(n, d)
    h = _rmsnorm(xf, norm_g[0])
    for l in range(depth):
        a_in = _branch_a(h, w_in, conv_w, l, seq)
        u = _proj(h, w_in, l, OFF_U, SSM_WIDTH, "none", F32)
        szb = _proj(h, w_in, l, OFF_ZB, SSM_WIDTH, "silu", BF16)
        sg = _proj(h, w_in, l, OFF_G, 2 * D_MODEL, "sigmoid", BF16)

        m0, w_si, w_so, lam_re, lam_im = _ssm_weights(
            a_re[l], a_im[l], log_dt[l], b_re[l], b_im[l], c_re[l], c_im[l])
        u_r = u.astype(BF16).reshape(bsz, n_chunks, SSM_CHUNK, SSM_GROUPS, SSM_GROUP)
        u_r = u_r.transpose(3, 1, 0, 2, 4).reshape(SSM_GROUPS, n_chunks * bsz, CHUNK_LANES)
        y_r = _ssm(u_r, m0, w_si, w_so, lam_re, lam_im, bsz)
        y = y_r.reshape(SSM_GROUPS, n_chunks, bsz, SSM_CHUNK, SSM_GROUP)
        y = y.transpose(2, 1, 3, 0, 4).reshape(n, SSM_WIDTH)

        b_in = _post_b(y, u, szb, d_skip, w_glu, b_glu, l)
        m = _merge(a_in, b_in, w_out_a, w_out_b, sg, l)
        if l + 1 < depth:
            xf, h = _out_proj(m, w_o, l, xf, norm_g[l + 1], True)
        else:
            out = _out_proj(m, w_o, l, xf, final_g, False)
    return out.reshape(bsz, seq, d)
```

```python
import functools

import jax
import jax.numpy as jnp
from jax import lax
from jax.experimental import pallas as pl
from jax.experimental.pallas import tpu as pltpu

F32 = jnp.float32
BF16 = jnp.bfloat16

D_MODEL = 2048
CONV_WIDTH = D_MODEL
CONV_K = 3
SSM_WIDTH = D_MODEL // 2
SSM_GROUP = 16
SSM_GROUPS = SSM_WIDTH // SSM_GROUP
SSM_STATE = 64
RMS_EPS = 1e-6

OFF_V = 0
OFF_BG = CONV_WIDTH
OFF_CG = 2 * CONV_WIDTH
OFF_ZA = 3 * CONV_WIDTH
OFF_U = 4 * CONV_WIDTH
OFF_ZB = OFF_U + SSM_WIDTH
OFF_G = OFF_ZB + SSM_WIDTH

SSM_CHUNK = 16
CHUNK_LANES = SSM_CHUNK * SSM_GROUP
LANE_TILE = 128
STATE_LANES = LANE_TILE
GROUPS_PER_STEP = 8
PREP_ROWS = 24

V7X_VMEM_LIMIT = 56 * 1024 * 1024


def _params(*sem):
    return pltpu.CompilerParams(dimension_semantics=sem, vmem_limit_bytes=V7X_VMEM_LIMIT)


def _rmsnorm_kernel(x_ref, g_ref, o_ref):
    x = x_ref[...]
    ms = jnp.mean(x * x, axis=-1, keepdims=True)
    o_ref[...] = (x * lax.rsqrt(ms + RMS_EPS) * g_ref[...]).astype(o_ref.dtype)


def _rmsnorm(x, g, tm=512):
    n, d = x.shape
    return pl.pallas_call(
        _rmsnorm_kernel,
        grid=(n // tm,),
        in_specs=[pl.BlockSpec((tm, d), lambda i: (i, 0)),
                  pl.BlockSpec((1, d), lambda i: (0, 0))],
        out_specs=pl.BlockSpec((tm, d), lambda i: (i, 0)),
        out_shape=jax.ShapeDtypeStruct((n, d), BF16),
        compiler_params=_params("arbitrary"),
        name="rmsnorm",
    )(x, g.reshape(1, d))


def _branch_a_kernel(h_ref, wv_ref, wbg_ref, wcg_ref, wza_ref, cw_ref, o_ref, carry_ref, w_ref,
                     *, tiles_per_seq):
    i = pl.program_id(1)

    @pl.when(i == 0)
    def _():
        for k, src in enumerate((wv_ref, wbg_ref, wcg_ref, wza_ref)):
            w_ref[k] = src[...].astype(BF16)

    @pl.when(i % tiles_per_seq == 0)
    def _():
        carry_ref[...] = jnp.zeros_like(carry_ref)

    h = h_ref[...]
    v = jnp.dot(h, w_ref[0], preferred_element_type=F32)
    bg = jnp.dot(h, w_ref[1], preferred_element_type=F32)
    cg = jnp.dot(h, w_ref[2], preferred_element_type=F32)
    za = jnp.dot(h, w_ref[3], preferred_element_type=F32)
    cv = cg * v
    tm = cv.shape[0]
    row = lax.broadcasted_iota(jnp.int32, cv.shape, 0)
    prev1 = carry_ref[7:8, :]
    prev2 = carry_ref[6:7, :]
    cv1 = jnp.where(row == 0, prev1, pltpu.roll(cv, 1, axis=0))
    cv2 = jnp.where(row == 0, prev2, jnp.where(row == 1, prev1, pltpu.roll(cv, 2, axis=0)))
    cw = cw_ref[...]
    conv = cw[0:1, :] * cv2 + cw[1:2, :] * cv1 + cw[2:3, :] * cv
    o_ref[...] = (bg * conv * (za * jax.nn.sigmoid(za))).astype(o_ref.dtype)
    carry_ref[...] = cv[tm - 8:tm, :]


def _branch_a(h, w_in, conv_w, layer, seq_len, tm=512, tn=256):
    n, d = h.shape
    nj = CONV_WIDTH // tn

    def wspec(off):
        return pl.BlockSpec((None, d, tn), lambda j, i, o=off // tn: (layer, 0, o + j))

    return pl.pallas_call(
        functools.partial(_branch_a_kernel, tiles_per_seq=seq_len // tm),
        grid=(nj, n // tm),
        in_specs=[pl.BlockSpec((tm, d), lambda j, i: (i, 0)),
                  wspec(OFF_V), wspec(OFF_BG), wspec(OFF_CG), wspec(OFF_ZA),
                  pl.BlockSpec((None, CONV_K, tn), lambda j, i: (layer, 0, j))],
        out_specs=pl.BlockSpec((tm, tn), lambda j, i: (i, j)),
        out_shape=jax.ShapeDtypeStruct((n, CONV_WIDTH), BF16),
        scratch_shapes=[pltpu.VMEM((8, tn), F32), pltpu.VMEM((4, d, tn), BF16)],
        compiler_params=_params("arbitrary", "arbitrary"),
        name="branch_a_proj_conv",
    )(h, w_in, w_in, w_in, w_in, conv_w)


def _proj_kernel(h_ref, w32_ref, o_ref, w_ref, *, act):
    @pl.when(pl.program_id(1) == 0)
    def _():
        w_ref[...] = w32_ref[...].astype(BF16)

    h = h_ref[...]
    y = jnp.dot(h.reshape(-1, h.shape[-1]), w_ref[...], preferred_element_type=F32)
    if act == "silu":
        y = y * jax.nn.sigmoid(y)
    elif act == "sigmoid":
        y = jax.nn.sigmoid(y)
    o_ref[...] = y.astype(o_ref.dtype).reshape(o_ref.shape)


def _proj(h, w_in, layer, col_off, width, act, out_dtype, tm=512, tn=512):
    n, d = h.shape
    return pl.pallas_call(
        functools.partial(_proj_kernel, act=act),
        grid=(width // tn, n // tm),
        in_specs=[pl.BlockSpec((tm, d), lambda j, i: (i, 0)),
                  pl.BlockSpec((None, d, tn), lambda j, i, o=col_off // tn: (layer, 0, o + j))],
        out_specs=pl.BlockSpec((tm, tn), lambda j, i: (i, j)),
        out_shape=jax.ShapeDtypeStruct((n, width), out_dtype),
        scratch_shapes=[pltpu.VMEM((d, tn), BF16)],
        compiler_params=_params("arbitrary", "arbitrary"),
        name="proj_" + act,
    )(h, w_in)


def _proj_chunk_major(h, w_in, layer, col_off, width, act, out_dtype, bsz, n_chunks, tn=512):
    n, d = h.shape
    h3 = h.reshape(bsz, n_chunks, SSM_CHUNK * d)
    out = pl.pallas_call(
        functools.partial(_proj_kernel, act=act),
        grid=(width // tn, SSM_CHUNK),
        in_specs=[pl.BlockSpec((bsz, n_chunks, d), lambda j, s: (0, 0, s)),
                  pl.BlockSpec((None, d, tn), lambda j, s, o=col_off // tn: (layer, 0, o + j))],
        out_specs=pl.BlockSpec((None, bsz, n_chunks, tn), lambda j, s: (s, 0, 0, j)),
        out_shape=jax.ShapeDtypeStruct((SSM_CHUNK, bsz, n_chunks, width), out_dtype),
        scratch_shapes=[pltpu.VMEM((d, tn), BF16)],
        compiler_params=_params("arbitrary", "arbitrary"),
        name="proj_chunk_major_" + act,
    )(h3, w_in)
    return out.reshape(SSM_CHUNK, bsz * n_chunks, width)


def _ssm_kernel(u_ref, m0_ref, win_ref, wout_ref, lre_ref, lim_ref, y_ref,
                t_ref, r_ref, sre_ref, sim_ref, hre_ref, him_ref, *, n_chunks, batch):
    gb = m0_ref.shape[0]
    t = SSM_CHUNK
    for b in range(batch):
        for s in range(t):
            xt = u_ref[s, b * n_chunks:(b + 1) * n_chunks, :].T
            for g in range(gb):
                row0 = g * CHUNK_LANES + s * SSM_GROUP
                t_ref[b, row0:row0 + SSM_GROUP, :] = xt[g * SSM_GROUP:(g + 1) * SSM_GROUP, :]
    for b in range(batch):
        for g in range(gb):
            a = t_ref[b, g * CHUNK_LANES:(g + 1) * CHUNK_LANES, :].T
            for hf in range(2):
                r_ref[g, hf, pl.ds(b, n_chunks, stride=batch), :] = a[:, hf * LANE_TILE:(hf + 1) * LANE_TILE]

    def u_of(g):
        return jnp.concatenate([r_ref[g, 0], r_ref[g, 1]], axis=1).astype(BF16)

    for g in range(gb):
        s = jnp.dot(u_of(g), win_ref[g], preferred_element_type=F32)
        sre_ref[:, g * STATE_LANES:(g + 1) * STATE_LANES] = s[:, :STATE_LANES]
        sim_ref[:, g * STATE_LANES:(g + 1) * STATE_LANES] = s[:, STATE_LANES:]
    lre = lre_ref[0]
    lim = lim_ref[0]
    rows = 2 * batch
    first = lax.broadcasted_iota(jnp.int32, (rows, gb * STATE_LANES), 0) < batch

    def advance(h_r, h_i, s_r, s_i):
        return lre * h_r - lim * h_i + s_r, lre * h_i + lim * h_r + s_i

    def step(j, carry):
        h_r, h_i = carry
        r0 = pl.multiple_of(j * rows, rows)
        s_r = sre_ref[pl.ds(r0, rows), :]
        s_i = sim_ref[pl.ds(r0, rows), :]
        n_r, n_i = advance(h_r, h_i, s_r, s_i)
        h_r = jnp.where(first, h_r, pltpu.roll(n_r, batch, axis=0))
        h_i = jnp.where(first, h_i, pltpu.roll(n_i, batch, axis=0))
        hre_ref[pl.ds(r0, rows), :] = h_r
        him_ref[pl.ds(r0, rows), :] = h_i
        n_r, n_i = advance(h_r, h_i, s_r, s_i)
        return pltpu.roll(n_r, batch, axis=0), pltpu.roll(n_i, batch, axis=0)

    zero = jnp.zeros((rows, gb * STATE_LANES), F32)
    lax.fori_loop(0, n_chunks // 2, step, (zero, zero))

    for g in range(gb):
        h = jnp.concatenate([hre_ref[:, g * STATE_LANES:(g + 1) * STATE_LANES],
                             him_ref[:, g * STATE_LANES:(g + 1) * STATE_LANES]], axis=1)
        y = jnp.dot(u_of(g), m0_ref[g], preferred_element_type=F32)
        y = y + jnp.dot(h.astype(BF16), wout_ref[g], preferred_element_type=F32)
        for hf in range(2):
            r_ref[g, hf] = y[:, hf * LANE_TILE:(hf + 1) * LANE_TILE]

    for b in range(batch):
        for g in range(gb):
            a = jnp.concatenate([r_ref[g, hf, pl.ds(b, n_chunks, stride=batch), :] for hf in range(2)],
                                axis=1)
            at = a.T
            for tt in range(t):
                row0 = tt * LANE_TILE + g * SSM_GROUP
                t_ref[b, row0:row0 + SSM_GROUP, :] = at[tt * SSM_GROUP:(tt + 1) * SSM_GROUP, :]
    for b in range(batch):
        for tt in range(t):
            y_ref[tt, b * n_chunks:(b + 1) * n_chunks, :] = t_ref[b, tt * LANE_TILE:(tt + 1) * LANE_TILE, :].T


def _ssm(u_p, m0, w_in_state, w_out_state, lam_re, lam_im, batch):
    t, n_inst, width = u_p.shape
    gb = GROUPS_PER_STEP
    assert gb * SSM_GROUP == LANE_TILE and n_inst // batch == LANE_TILE
    blk = pl.BlockSpec((t, n_inst, LANE_TILE), lambda i: (0, 0, i))
    sq = pl.BlockSpec((gb, CHUNK_LANES, CHUNK_LANES), lambda i: (i, 0, 0))
    vec = pl.BlockSpec((1, 1, gb * STATE_LANES), lambda i: (i, 0, 0))
    return pl.pallas_call(
        functools.partial(_ssm_kernel, n_chunks=n_inst // batch, batch=batch),
        grid=(width // LANE_TILE,),
        in_specs=[blk, sq, sq, sq, vec, vec],
        out_specs=blk,
        out_shape=jax.ShapeDtypeStruct(u_p.shape, F32),
        scratch_shapes=[pltpu.VMEM((batch, gb * CHUNK_LANES, LANE_TILE), F32),
                        pltpu.VMEM((gb, 2, n_inst, LANE_TILE), F32)]
                       + [pltpu.VMEM((n_inst, gb * STATE_LANES), F32) for _ in range(4)],
        compiler_params=_params("arbitrary"),
        name="s5_chunked_scan",
    )(u_p, m0, w_in_state, w_out_state, lam_re, lam_im)


def _split_bf16(x):
    hi = x.astype(BF16)
    return hi, (x - hi.astype(F32)).astype(BF16)


def _dot_nt_f32(a, b):
    dn = (((1,), (1,)), ((), ()))
    a_hi, a_lo = _split_bf16(a)
    b_hi, b_lo = _split_bf16(b)

    def d(x, y):
        return lax.dot_general(x, y, dn, preferred_element_type=F32)

    return d(a_hi, b_hi) + d(a_hi, b_lo) + d(a_lo, b_hi)


def _ssm_prep_kernel(are_ref, aim_ref, ldt_ref, btr_ref, bti_ref, cre_ref, cim_ref,
                     m0_ref, win_ref, wout_ref, lre_ref, lim_ref):
    gb = are_ref.shape[0]
    t = SSM_CHUNK
    tau = lax.broadcasted_iota(jnp.int32, (PREP_ROWS, STATE_LANES), 0).astype(F32)
    lane = lax.broadcasted_iota(jnp.int32, (SSM_GROUP, CHUNK_LANES), 1)
    for g in range(gb):
        are = are_ref[g]
        aim = aim_ref[g]
        dt = jnp.exp(ldt_ref[g])
        mag = jnp.exp(tau * (dt * are))
        ang = tau * (dt * aim)
        pw_re = mag * jnp.cos(ang)
        pw_im = mag * jnp.sin(ang)
        nr = pw_re[1:2] - 1.0
        ni = pw_im[1:2]
        den = are * are + aim * aim
        cr = (nr * are + ni * aim) / den
        ci = (ni * are - nr * aim) / den
        btr = btr_ref[g]
        bti = bti_ref[g]
        bb_re = cr * btr - ci * bti
        bb_im = cr * bti + ci * btr
        cre = cre_ref[g]
        cim = cim_ref[g]

        win_re, win_im = [], []
        for s in range(t):
            pr = pw_re[t - 1 - s:t - s]
            pi = pw_im[t - 1 - s:t - s]
            win_re.append(pr * bb_re - pi * bb_im)
            win_im.append(pr * bb_im + pi * bb_re)
        win_ref[g] = jnp.concatenate(
            [jnp.concatenate(win_re, axis=0), jnp.concatenate(win_im, axis=0)], axis=1).astype(BF16)

        z_re, z_im = [], []
        for k in range(t + 1):
            pr = pw_re[k:k + 1]
            pi = pw_im[k:k + 1]
            z_re.append(cre * pr - cim * pi)
            z_im.append(cre * pi + cim * pr)

        zo_re = jnp.concatenate(z_re[1:], axis=0)
        zo_im = jnp.concatenate(z_im[1:], axis=0)
        wout_ref[g] = jnp.concatenate([zo_re.T, -(zo_im.T)], axis=0).astype(BF16)

        zk_re = jnp.concatenate(z_re[:t], axis=0)
        zk_im = jnp.concatenate(z_im[:t], axis=0)
        kt = _dot_nt_f32(bb_re, zk_re) - _dot_nt_f32(bb_im, zk_im)
        blocks = [kt]
        for s in range(1, t):
            shifted = pltpu.roll(kt, s * SSM_GROUP, axis=1)
            blocks.append(jnp.where(lane >= s * SSM_GROUP, shifted, 0.0))
        m0_ref[g] = jnp.concatenate(blocks, axis=0).astype(BF16)

        lre_ref[0, :, g * STATE_LANES:(g + 1) * STATE_LANES] = pw_re[t:t + 1]
        lim_ref[0, :, g * STATE_LANES:(g + 1) * STATE_LANES] = pw_im[t:t + 1]


def _ssm_weights(a_re, a_im, log_dt, b_re, b_im, c_re, c_im):
    g, p = a_re.shape
    gb = GROUPS_PER_STEP
    pad = STATE_LANES - p

    def lanes(x, value=0.0):
        return jnp.pad(x.astype(F32), [(0, 0)] * (x.ndim - 1) + [(0, pad)], constant_values=value)

    are = lanes(a_re, -1.0).reshape(g, 1, STATE_LANES)
    aim = lanes(a_im).reshape(g, 1, STATE_LANES)
    ldt = jnp.broadcast_to(log_dt.astype(F32)[:, None, None], (g, 1, STATE_LANES))
    btr = lanes(jnp.swapaxes(b_re, 1, 2))
    bti = lanes(jnp.swapaxes(b_im, 1, 2))
    cre = lanes(c_re)
    cim = lanes(c_im)
    vec = pl.BlockSpec((gb, 1, STATE_LANES), lambda i: (i, 0, 0))
    mat = pl.BlockSpec((gb, SSM_GROUP, STATE_LANES), lambda i: (i, 0, 0))
    sq = pl.BlockSpec((gb, CHUNK_LANES, CHUNK_LANES), lambda i: (i, 0, 0))
    row = pl.BlockSpec((1, 1, gb * STATE_LANES), lambda i: (i, 0, 0))
    sq_shape = jax.ShapeDtypeStruct((g, CHUNK_LANES, CHUNK_LANES), BF16)
    row_shape = jax.ShapeDtypeStruct((g // gb, 1, gb * STATE_LANES), F32)
    return pl.pallas_call(
        _ssm_prep_kernel,
        grid=(g // gb,),
        in_specs=[vec, vec, vec, mat, mat, mat, mat],
        out_specs=[sq, sq, sq, row, row],
        out_shape=[sq_shape, sq_shape, sq_shape, row_shape, row_shape],
        compiler_params=_params("arbitrary"),
        name="s5_chunk_operators",
    )(are, aim, ldt, btr, bti, cre, cim)


def _post_b_kernel(y_ref, u_ref, szb_ref, d_ref, wg32_ref, bgl_ref, o_ref, wg_ref):
    @pl.when(pl.program_id(0) == 0)
    def _():
        wg_ref[...] = wg32_ref[...].astype(BF16)

    yb = jax.nn.gelu(y_ref[...] + d_ref[...] * u_ref[...])
    z = jnp.dot(yb.astype(BF16), wg_ref[...], preferred_element_type=F32) + bgl_ref[...]
    out = (yb * jax.nn.sigmoid(z) * szb_ref[...].astype(F32)).astype(o_ref.dtype)
    o_ref[...] = out.reshape(o_ref.shape)


def _post_b(y_p, u_p, szb_p, d_skip, w_glu, b_glu, layer, bsz, n_chunks):
    t, n_inst, w = y_p.shape
    row = pl.BlockSpec((None, n_inst, w), lambda s: (s, 0, 0))
    vec = pl.BlockSpec((None, 1, w), lambda s: (layer, 0, 0))
    out = pl.pallas_call(
        _post_b_kernel,
        grid=(t,),
        in_specs=[row, row, row, vec, pl.BlockSpec((None, w, w), lambda s: (layer, 0, 0)), vec],
        out_specs=pl.BlockSpec((bsz, n_chunks, w), lambda s: (0, 0, s)),
        out_shape=jax.ShapeDtypeStruct((bsz, n_chunks, t * w), BF16),
        scratch_shapes=[pltpu.VMEM((w, w), BF16)],
        compiler_params=_params("arbitrary"),
        name="branch_b_glu",
    )(y_p, u_p, szb_p, d_skip.reshape(-1, 1, w), w_glu, b_glu.reshape(-1, 1, w))
    return out.reshape(bsz * n_chunks * t, w)


def _merge_kernel(a_ref, b_ref, wa32_ref, wb32_ref, sga_ref, sgb_ref, o_ref, wa_ref, wb_ref):
    @pl.when(pl.program_id(1) == 0)
    def _():
        wa_ref[...] = wa32_ref[...].astype(BF16)
        wb_ref[...] = wb32_ref[...].astype(BF16)

    ya = jnp.dot(a_ref[...], wa_ref[...], preferred_element_type=F32)
    yb = jnp.dot(b_ref[...], wb_ref[...], preferred_element_type=F32)
    o_ref[...] = (sga_ref[...].astype(F32) * ya + sgb_ref[...].astype(F32) * yb).astype(o_ref.dtype)


def _merge(a_in, b_in, w_a, w_b, sg, layer, tm=512, tn=512):
    n = a_in.shape[0]
    nj = D_MODEL // tn
    return pl.pallas_call(
        _merge_kernel,
        grid=(nj, n // tm),
        in_specs=[pl.BlockSpec((tm, CONV_WIDTH), lambda j, i: (i, 0)),
                  pl.BlockSpec((tm, SSM_WIDTH), lambda j, i: (i, 0)),
                  pl.BlockSpec((None, CONV_WIDTH, tn), lambda j, i: (layer, 0, j)),
                  pl.BlockSpec((None, SSM_WIDTH, tn), lambda j, i: (layer, 0, j)),
                  pl.BlockSpec((tm, tn), lambda j, i: (i, j)),
                  pl.BlockSpec((tm, tn), lambda j, i, o=nj: (i, o + j))],
        out_specs=pl.BlockSpec((tm, tn), lambda j, i: (i, j)),
        out_shape=jax.ShapeDtypeStruct((n, D_MODEL), BF16),
        scratch_shapes=[pltpu.VMEM((CONV_WIDTH, tn), BF16), pltpu.VMEM((SSM_WIDTH, tn), BF16)],
        compiler_params=_params("arbitrary", "arbitrary"),
        name="gated_merge",
    )(a_in, b_in, w_a, w_b, sg, sg)


def _out_kernel(m_ref, wo32_ref, x_ref, g_ref, *refs, emit_x):
    o_refs, wo_ref = refs[:-1], refs[-1]

    @pl.when(pl.program_id(0) == 0)
    def _():
        wo_ref[...] = wo32_ref[...].astype(BF16)

    xn = x_ref[...] + jnp.dot(m_ref[...], wo_ref[...], preferred_element_type=F32)
    ms = jnp.mean(xn * xn, axis=-1, keepdims=True)
    normed = xn * lax.rsqrt(ms + RMS_EPS) * g_ref[...]
    if emit_x:
        o_refs[0][...] = xn
        o_refs[1][...] = normed.astype(o_refs[1].dtype)
    else:
        o_refs[0][...] = normed.astype(o_refs[0].dtype)


def _out_proj(m, w_o, layer, x, g, emit_x, tm=256):
    n, d = x.shape
    row = pl.BlockSpec((tm, d), lambda i: (i, 0))
    if emit_x:
        out_specs = [row, row]
        out_shape = [jax.ShapeDtypeStruct((n, d), F32), jax.ShapeDtypeStruct((n, d), BF16)]
    else:
        out_specs = row
        out_shape = jax.ShapeDtypeStruct((n, d), F32)
    w_spec = pl.BlockSpec((None, d, d), lambda i: (layer, 0, 0), pipeline_mode=pl.Buffered(1))
    return pl.pallas_call(
        functools.partial(_out_kernel, emit_x=emit_x),
        grid=(n // tm,),
        in_specs=[row, w_spec, row, pl.BlockSpec((1, d), lambda i: (0, 0))],
        out_specs=out_specs,
        out_shape=out_shape,
        scratch_shapes=[pltpu.VMEM((d, d), BF16)],
        compiler_params=_params("arbitrary"),
        name="out_proj_residual_norm",
    )(m, w_o, x, g.reshape(1, d))


def kernel(x, norm_g, w_in, conv_w, w_out_a, a_re, a_im, log_dt, b_re, b_im, c_re, c_im,
           d_skip, w_glu, b_glu, w_out_b, w_o, final_g):
    bsz, seq, d = x.shape
    depth = norm_g.shape[0]
    n = bsz * seq
    n_chunks = seq // SSM_CHUNK
    assert 2 * bsz == 8 and n_chunks % 2 == 0
    xf = x.reshape(n, d)
    h = _rmsnorm(xf, norm_g[0])
    for l in range(depth):
        a_in = _branch_a(h, w_in, conv_w, l, seq)
        u_p = _proj_chunk_major(h, w_in, l, OFF_U, SSM_WIDTH, "none", F32, bsz, n_chunks)
        szb_p = _proj_chunk_major(h, w_in, l, OFF_ZB, SSM_WIDTH, "silu", BF16, bsz, n_chunks)
        sg = _proj(h, w_in, l, OFF_G, 2 * D_MODEL, "sigmoid", BF16)

        m0, w_si, w_so, lam_re, lam_im = _ssm_weights(
            a_re[l], a_im[l], log_dt[l], b_re[l], b_im[l], c_re[l], c_im[l])
        y_p = _ssm(u_p, m0, w_si, w_so, lam_re, lam_im, bsz)
        b_in = _post_b(y_p, u_p, szb_p, d_skip, w_glu, b_glu, l, bsz, n_chunks)
        m = _merge(a_in, b_in, w_out_a, w_out_b, sg, l)
        if l + 1 < depth:
            xf, h = _out_proj(m, w_o, l, xf, norm_g[l + 1], True)
        else:
            out = _out_proj(m, w_o, l, xf, final_g, False)
    return out.reshape(bsz, seq, d)
```

```python
import functools

import jax
import jax.numpy as jnp
from jax import lax
from jax.experimental import pallas as pl
from jax.experimental.pallas import tpu as pltpu

F32 = jnp.float32
BF16 = jnp.bfloat16

D_MODEL = 2048
CONV_WIDTH = D_MODEL
CONV_K = 3
SSM_WIDTH = D_MODEL // 2
SSM_GROUP = 16
SSM_GROUPS = SSM_WIDTH // SSM_GROUP
SSM_STATE = 64
RMS_EPS = 1e-6

OFF_V = 0
OFF_BG = CONV_WIDTH
OFF_CG = 2 * CONV_WIDTH
OFF_ZA = 3 * CONV_WIDTH
OFF_U = 4 * CONV_WIDTH
OFF_ZB = OFF_U + SSM_WIDTH
OFF_G = OFF_ZB + SSM_WIDTH

SSM_CHUNK = 16
CHUNK_LANES = SSM_CHUNK * SSM_GROUP
LANE_TILE = 128
STATE_LANES = LANE_TILE
GROUPS_PER_STEP = 8
PREP_ROWS = 24

V7X_VMEM_LIMIT = 56 * 1024 * 1024


def _params(*sem):
    return pltpu.CompilerParams(dimension_semantics=sem, vmem_limit_bytes=V7X_VMEM_LIMIT)


def _rmsnorm_kernel(x_ref, g_ref, o_ref):
    x = x_ref[...]
    ms = jnp.mean(x * x, axis=-1, keepdims=True)
    o_ref[...] = (x * lax.rsqrt(ms + RMS_EPS) * g_ref[...]).astype(o_ref.dtype)


def _rmsnorm(x, g, tm=512):
    n, d = x.shape
    return pl.pallas_call(
        _rmsnorm_kernel,
        grid=(n // tm,),
        in_specs=[pl.BlockSpec((tm, d), lambda i: (i, 0)),
                  pl.BlockSpec((1, d), lambda i: (0, 0))],
        out_specs=pl.BlockSpec((tm, d), lambda i: (i, 0)),
        out_shape=jax.ShapeDtypeStruct((n, d), BF16),
        compiler_params=_params("arbitrary"),
        name="rmsnorm",
    )(x, g.reshape(1, d))


def _branch_a_kernel(h_ref, wv_ref, wbg_ref, wcg_ref, wza_ref, cw_ref, o_ref, carry_ref, w_ref,
                     *, tiles_per_seq, sub):
    i = pl.program_id(1)

    @pl.when(i == 0)
    def _():
        for k, src in enumerate((wv_ref, wbg_ref, wcg_ref, wza_ref)):
            w_ref[k] = src[...].astype(BF16)

    @pl.when(i % tiles_per_seq == 0)
    def _():
        carry_ref[...] = jnp.zeros_like(carry_ref)

    cw = cw_ref[...]
    rs = h_ref.shape[0] // sub
    row = lax.broadcasted_iota(jnp.int32, (rs, o_ref.shape[1]), 0)
    tail = carry_ref[...]
    for r in range(sub):
        h = h_ref[r * rs:(r + 1) * rs, :]
        v = jnp.dot(h, w_ref[0], preferred_element_type=F32)
        bg = jnp.dot(h, w_ref[1], preferred_element_type=F32)
        cg = jnp.dot(h, w_ref[2], preferred_element_type=F32)
        za = jnp.dot(h, w_ref[3], preferred_element_type=F32)
        cv = cg * v
        prev1 = tail[7:8, :]
        prev2 = tail[6:7, :]
        cv1 = jnp.where(row == 0, prev1, pltpu.roll(cv, 1, axis=0))
        cv2 = jnp.where(row == 0, prev2, jnp.where(row == 1, prev1, pltpu.roll(cv, 2, axis=0)))
        conv = cw[0:1, :] * cv2 + cw[1:2, :] * cv1 + cw[2:3, :] * cv
        o_ref[r * rs:(r + 1) * rs, :] = (bg * conv * (za * jax.nn.sigmoid(za))).astype(o_ref.dtype)
        tail = cv[rs - 8:rs, :]
    carry_ref[...] = tail


def _branch_a(h, w_in, conv_w, layer, seq_len, tm=1024, tn=256, sub=4):
    n, d = h.shape
    nj = CONV_WIDTH // tn

    def wspec(off):
        return pl.BlockSpec((None, d, tn), lambda j, i, o=off // tn: (layer, 0, o + j))

    return pl.pallas_call(
        functools.partial(_branch_a_kernel, tiles_per_seq=seq_len // tm, sub=sub),
        grid=(nj, n // tm),
        in_specs=[pl.BlockSpec((tm, d), lambda j, i: (i, 0)),
                  wspec(OFF_V), wspec(OFF_BG), wspec(OFF_CG), wspec(OFF_ZA),
                  pl.BlockSpec((None, CONV_K, tn), lambda j, i: (layer, 0, j))],
        out_specs=pl.BlockSpec((tm, tn), lambda j, i: (i, j)),
        out_shape=jax.ShapeDtypeStruct((n, CONV_WIDTH), BF16),
        scratch_shapes=[pltpu.VMEM((8, tn), F32), pltpu.VMEM((4, d, tn), BF16)],
        compiler_params=_params("arbitrary", "arbitrary"),
        name="branch_a_proj_conv",
    )(h, w_in, w_in, w_in, w_in, conv_w)


def _proj_kernel(h_ref, w32_ref, o_ref, w_ref, *, act, sub):
    @pl.when(pl.program_id(1) == 0)
    def _():
        w_ref[...] = w32_ref[...].astype(BF16)

    rs = h_ref.shape[0] // sub
    for r in range(sub):
        rows = slice(r * rs, (r + 1) * rs)
        y = jnp.dot(h_ref[rows, :], w_ref[...], preferred_element_type=F32)
        if act == "silu":
            y = y * jax.nn.sigmoid(y)
        elif act == "sigmoid":
            y = jax.nn.sigmoid(y)
        o_ref[rows, :] = y.astype(o_ref.dtype)


def _proj(h, w_in, layer, col_off, width, act, out_dtype, tm=1024, tn=512, sub=4):
    n, d = h.shape
    return pl.pallas_call(
        functools.partial(_proj_kernel, act=act, sub=sub),
        grid=(width // tn, n // tm),
        in_specs=[pl.BlockSpec((tm, d), lambda j, i: (i, 0)),
                  pl.BlockSpec((None, d, tn), lambda j, i, o=col_off // tn: (layer, 0, o + j))],
        out_specs=pl.BlockSpec((tm, tn), lambda j, i: (i, j)),
        out_shape=jax.ShapeDtypeStruct((n, width), out_dtype),
        scratch_shapes=[pltpu.VMEM((d, tn), BF16)],
        compiler_params=_params("arbitrary", "arbitrary"),
        name="proj_" + act,
    )(h, w_in)


def _ssm_kernel(u_ref, m0_ref, win_ref, wout_ref, lre_ref, lim_ref, y_ref,
                t_ref, r_ref, sre_ref, sim_ref, hre_ref, him_ref, *, n_chunks, batch):
    gb = m0_ref.shape[0]
    t = SSM_CHUNK
    for b in range(batch):
        for s in range(t):
            xt = u_ref[b, pl.ds(s, n_chunks, stride=t), :].T
            for g in range(gb):
                row0 = g * CHUNK_LANES + s * SSM_GROUP
                t_ref[b, row0:row0 + SSM_GROUP, :] = xt[g * SSM_GROUP:(g + 1) * SSM_GROUP, :]
    for b in range(batch):
        for g in range(gb):
            a = t_ref[b, g * CHUNK_LANES:(g + 1) * CHUNK_LANES, :].T
            for hf in range(2):
                r_ref[g, hf, pl.ds(b, n_chunks, stride=batch), :] = a[:, hf * LANE_TILE:(hf + 1) * LANE_TILE]

    def u_of(g):
        return jnp.concatenate([r_ref[g, 0], r_ref[g, 1]], axis=1).astype(BF16)

    for g in range(gb):
        s = jnp.dot(u_of(g), win_ref[g], preferred_element_type=F32)
        sre_ref[:, g * STATE_LANES:(g + 1) * STATE_LANES] = s[:, :STATE_LANES]
        sim_ref[:, g * STATE_LANES:(g + 1) * STATE_LANES] = s[:, STATE_LANES:]
    lre = lre_ref[0]
    lim = lim_ref[0]
    rows = 2 * batch
    first = lax.broadcasted_iota(jnp.int32, (rows, gb * STATE_LANES), 0) < batch

    def advance(h_r, h_i, s_r, s_i):
        return lre * h_r - lim * h_i + s_r, lre * h_i + lim * h_r + s_i

    def step(j, carry):
        h_r, h_i = carry
        r0 = pl.multiple_of(j * rows, rows)
        s_r = sre_ref[pl.ds(r0, rows), :]
        s_i = sim_ref[pl.ds(r0, rows), :]
        n_r, n_i = advance(h_r, h_i, s_r, s_i)
        h_r = jnp.where(first, h_r, pltpu.roll(n_r, batch, axis=0))
        h_i = jnp.where(first, h_i, pltpu.roll(n_i, batch, axis=0))
        hre_ref[pl.ds(r0, rows), :] = h_r
        him_ref[pl.ds(r0, rows), :] = h_i
        n_r, n_i = advance(h_r, h_i, s_r, s_i)
        return pltpu.roll(n_r, batch, axis=0), pltpu.roll(n_i, batch, axis=0)

    zero = jnp.zeros((rows, gb * STATE_LANES), F32)
    lax.fori_loop(0, n_chunks // 2, step, (zero, zero))

    for g in range(gb):
        h = jnp.concatenate([hre_ref[:, g * STATE_LANES:(g + 1) * STATE_LANES],
                             him_ref[:, g * STATE_LANES:(g + 1) * STATE_LANES]], axis=1)
        y = jnp.dot(u_of(g), m0_ref[g], preferred_element_type=F32)
        y = y + jnp.dot(h.astype(BF16), wout_ref[g], preferred_element_type=F32)
        for hf in range(2):
            r_ref[g, hf] = y[:, hf * LANE_TILE:(hf + 1) * LANE_TILE]

    for b in range(batch):
        for g in range(gb):
            a = jnp.concatenate([r_ref[g, hf, pl.ds(b, n_chunks, stride=batch), :] for hf in range(2)],
                                axis=1)
            at = a.T
            for tt in range(t):
                row0 = tt * LANE_TILE + g * SSM_GROUP
                t_ref[b, row0:row0 + SSM_GROUP, :] = at[tt * SSM_GROUP:(tt + 1) * SSM_GROUP, :]
    for b in range(batch):
        for tt in range(t):
            y_ref[b, pl.ds(tt, n_chunks, stride=t), :] = t_ref[b, tt * LANE_TILE:(tt + 1) * LANE_TILE, :].T


def _ssm(u_p, m0, w_in_state, w_out_state, lam_re, lam_im):
    batch, seq, width = u_p.shape
    n_inst = batch * seq // SSM_CHUNK
    gb = GROUPS_PER_STEP
    assert gb * SSM_GROUP == LANE_TILE and n_inst // batch == LANE_TILE
    blk = pl.BlockSpec((batch, seq, LANE_TILE), lambda i: (0, 0, i))
    sq = pl.BlockSpec((gb, CHUNK_LANES, CHUNK_LANES), lambda i: (i, 0, 0))
    vec = pl.BlockSpec((1, 1, gb * STATE_LANES), lambda i: (i, 0, 0))
    return pl.pallas_call(
        functools.partial(_ssm_kernel, n_chunks=n_inst // batch, batch=batch),
        grid=(width // LANE_TILE,),
        in_specs=[blk, sq, sq, sq, vec, vec],
        out_specs=blk,
        out_shape=jax.ShapeDtypeStruct(u_p.shape, F32),
        scratch_shapes=[pltpu.VMEM((batch, gb * CHUNK_LANES, LANE_TILE), F32),
                        pltpu.VMEM((gb, 2, n_inst, LANE_TILE), F32)]
                       + [pltpu.VMEM((n_inst, gb * STATE_LANES), F32) for _ in range(4)],
        compiler_params=_params("arbitrary"),
        name="s5_chunked_scan",
    )(u_p, m0, w_in_state, w_out_state, lam_re, lam_im)


def _split_bf16(x):
    hi = x.astype(BF16)
    return hi, (x - hi.astype(F32)).astype(BF16)


def _dot_nt_f32(a, b):
    dn = (((1,), (1,)), ((), ()))
    a_hi, a_lo = _split_bf16(a)
    b_hi, b_lo = _split_bf16(b)

    def d(x, y):
        return lax.dot_general(x, y, dn, preferred_element_type=F32)

    return d(a_hi, b_hi) + d(a_hi, b_lo) + d(a_lo, b_hi)


def _ssm_prep_kernel(are_ref, aim_ref, ldt_ref, btr_ref, bti_ref, cre_ref, cim_ref,
                     m0_ref, win_ref, wout_ref, lre_ref, lim_ref):
    gb = are_ref.shape[0]
    t = SSM_CHUNK
    tau = lax.broadcasted_iota(jnp.int32, (PREP_ROWS, STATE_LANES), 0).astype(F32)
    lane = lax.broadcasted_iota(jnp.int32, (SSM_GROUP, CHUNK_LANES), 1)
    for g in range(gb):
        are = are_ref[g]
        aim = aim_ref[g]
        dt = jnp.exp(ldt_ref[g])
        mag = jnp.exp(tau * (dt * are))
        ang = tau * (dt * aim)
        pw_re = mag * jnp.cos(ang)
        pw_im = mag * jnp.sin(ang)
        nr = pw_re[1:2] - 1.0
        ni = pw_im[1:2]
        den = are * are + aim * aim
        cr = (nr * are + ni * aim) / den
        ci = (ni * are - nr * aim) / den
        btr = btr_ref[g]
        bti = bti_ref[g]
        bb_re = cr * btr - ci * bti
        bb_im = cr * bti + ci * btr
        cre = cre_ref[g]
        cim = cim_ref[g]

        win_re, win_im = [], []
        for s in range(t):
            pr = pw_re[t - 1 - s:t - s]
            pi = pw_im[t - 1 - s:t - s]
            win_re.append(pr * bb_re - pi * bb_im)
            win_im.append(pr * bb_im + pi * bb_re)
        win_ref[g] = jnp.concatenate(
            [jnp.concatenate(win_re, axis=0), jnp.concatenate(win_im, axis=0)], axis=1).astype(BF16)

        z_re, z_im = [], []
        for k in range(t + 1):
            pr = pw_re[k:k + 1]
            pi = pw_im[k:k + 1]
            z_re.append(cre * pr - cim * pi)
            z_im.append(cre * pi + cim * pr)

        zo_re = jnp.concatenate(z_re[1:], axis=0)
        zo_im = jnp.concatenate(z_im[1:], axis=0)
        wout_ref[g] = jnp.concatenate([zo_re.T, -(zo_im.T)], axis=0).astype(BF16)

        zk_re = jnp.concatenate(z_re[:t], axis=0)
        zk_im = jnp.concatenate(z_im[:t], axis=0)
        kt = _dot_nt_f32(bb_re, zk_re) - _dot_nt_f32(bb_im, zk_im)
        blocks = [kt]
        for s in range(1, t):
            shifted = pltpu.roll(kt, s * SSM_GROUP, axis=1)
            blocks.append(jnp.where(lane >= s * SSM_GROUP, shifted, 0.0))
        m0_ref[g] = jnp.concatenate(blocks, axis=0).astype(BF16)

        lre_ref[0, :, g * STATE_LANES:(g + 1) * STATE_LANES] = pw_re[t:t + 1]
        lim_ref[0, :, g * STATE_LANES:(g + 1) * STATE_LANES] = pw_im[t:t + 1]


def _ssm_weights(a_re, a_im, log_dt, b_re, b_im, c_re, c_im):
    g, p = a_re.shape
    gb = GROUPS_PER_STEP
    pad = STATE_LANES - p

    def lanes(x, value=0.0):
        return jnp.pad(x.astype(F32), [(0, 0)] * (x.ndim - 1) + [(0, pad)], constant_values=value)

    are = lanes(a_re, -1.0).reshape(g, 1, STATE_LANES)
    aim = lanes(a_im).reshape(g, 1, STATE_LANES)
    ldt = jnp.broadcast_to(log_dt.astype(F32)[:, None, None], (g, 1, STATE_LANES))
    btr = lanes(jnp.swapaxes(b_re, 1, 2))
    bti = lanes(jnp.swapaxes(b_im, 1, 2))
    cre = lanes(c_re)
    cim = lanes(c_im)
    vec = pl.BlockSpec((gb, 1, STATE_LANES), lambda i: (i, 0, 0))
    mat = pl.BlockSpec((gb, SSM_GROUP, STATE_LANES), lambda i: (i, 0, 0))
    sq = pl.BlockSpec((gb, CHUNK_LANES, CHUNK_LANES), lambda i: (i, 0, 0))
    row = pl.BlockSpec((1, 1, gb * STATE_LANES), lambda i: (i, 0, 0))
    sq_shape = jax.ShapeDtypeStruct((g, CHUNK_LANES, CHUNK_LANES), BF16)
    row_shape = jax.ShapeDtypeStruct((g // gb, 1, gb * STATE_LANES), F32)
    return pl.pallas_call(
        _ssm_prep_kernel,
        grid=(g // gb,),
        in_specs=[vec, vec, vec, mat, mat, mat, mat],
        out_specs=[sq, sq, sq, row, row],
        out_shape=[sq_shape, sq_shape, sq_shape, row_shape, row_shape],
        compiler_params=_params("arbitrary"),
        name="s5_chunk_operators",
    )(are, aim, ldt, btr, bti, cre, cim)


def _post_b_kernel(y_ref, u_ref, szb_ref, d_ref, wg32_ref, bgl_ref, o_ref, wg_ref):
    @pl.when(pl.program_id(0) == 0)
    def _():
        wg_ref[...] = wg32_ref[...].astype(BF16)

    yb = jax.nn.gelu(y_ref[...] + d_ref[...] * u_ref[...])
    z = jnp.dot(yb.astype(BF16), wg_ref[...], preferred_element_type=F32) + bgl_ref[...]
    o_ref[...] = (yb * jax.nn.sigmoid(z) * szb_ref[...].astype(F32)).astype(o_ref.dtype)


def _post_b(y, u, szb, d_skip, w_glu, b_glu, layer, tm=512):
    n, w = y.shape
    row = pl.BlockSpec((tm, w), lambda i: (i, 0))
    vec = pl.BlockSpec((None, 1, w), lambda i: (layer, 0, 0))
    return pl.pallas_call(
        _post_b_kernel,
        grid=(n // tm,),
        in_specs=[row, row, row, vec, pl.BlockSpec((None, w, w), lambda i: (layer, 0, 0)), vec],
        out_specs=row,
        out_shape=jax.ShapeDtypeStruct((n, w), BF16),
        scratch_shapes=[pltpu.VMEM((w, w), BF16)],
        compiler_params=_params("arbitrary"),
        name="branch_b_glu",
    )(y, u, szb, d_skip.reshape(-1, 1, w), w_glu, b_glu.reshape(-1, 1, w))


def _merge_kernel(a_ref, b_ref, wa32_ref, wb32_ref, sga_ref, sgb_ref, o_ref, wa_ref, wb_ref, *, sub):
    @pl.when(pl.program_id(1) == 0)
    def _():
        wa_ref[...] = wa32_ref[...].astype(BF16)
        wb_ref[...] = wb32_ref[...].astype(BF16)

    rs = a_ref.shape[0] // sub
    for r in range(sub):
        rows = slice(r * rs, (r + 1) * rs)
        ya = jnp.dot(a_ref[rows, :], wa_ref[...], preferred_element_type=F32)
        yb = jnp.dot(b_ref[rows, :], wb_ref[...], preferred_element_type=F32)
        o_ref[rows, :] = (sga_ref[rows, :].astype(F32) * ya
                          + sgb_ref[rows, :].astype(F32) * yb).astype(o_ref.dtype)


def _merge(a_in, b_in, w_a, w_b, sg, layer, tm=1024, tn=512, sub=4):
    n = a_in.shape[0]
    nj = D_MODEL // tn
    return pl.pallas_call(
        functools.partial(_merge_kernel, sub=sub),
        grid=(nj, n // tm),
        in_specs=[pl.BlockSpec((tm, CONV_WIDTH), lambda j, i: (i, 0)),
                  pl.BlockSpec((tm, SSM_WIDTH), lambda j, i: (i, 0)),
                  pl.BlockSpec((None, CONV_WIDTH, tn), lambda j, i: (layer, 0, j)),
                  pl.BlockSpec((None, SSM_WIDTH, tn), lambda j, i: (layer, 0, j)),
                  pl.BlockSpec((tm, tn), lambda j, i: (i, j)),
                  pl.BlockSpec((tm, tn), lambda j, i, o=nj: (i, o + j))],
        out_specs=pl.BlockSpec((tm, tn), lambda j, i: (i, j)),
        out_shape=jax.ShapeDtypeStruct((n, D_MODEL), BF16),
        scratch_shapes=[pltpu.VMEM((CONV_WIDTH, tn), BF16), pltpu.VMEM((SSM_WIDTH, tn), BF16)],
        compiler_params=_params("arbitrary", "arbitrary"),
        name="gated_merge",
    )(a_in, b_in, w_a, w_b, sg, sg)


def _out_kernel(m_ref, wo32_ref, x_ref, g_ref, *refs, emit_x, sub):
    o_refs, wo_ref = refs[:-1], refs[-1]

    @pl.when(pl.program_id(0) == 0)
    def _():
        wo_ref[...] = wo32_ref[...].astype(BF16)

    rs = m_ref.shape[0] // sub
    for r in range(sub):
        rows = slice(r * rs, (r + 1) * rs)
        xn = x_ref[rows, :] + jnp.dot(m_ref[rows, :], wo_ref[...], preferred_element_type=F32)
        ms = jnp.mean(xn * xn, axis=-1, keepdims=True)
        normed = xn * lax.rsqrt(ms + RMS_EPS) * g_ref[...]
        if emit_x:
            o_refs[0][rows, :] = xn
            o_refs[1][rows, :] = normed.astype(o_refs[1].dtype)
        else:
            o_refs[0][rows, :] = normed.astype(o_refs[0].dtype)


def _out_proj(m, w_o, layer, x, g, emit_x, tm=256, sub=2):
    n, d = x.shape
    row = pl.BlockSpec((tm, d), lambda i: (i, 0))
    if emit_x:
        out_specs = [row, row]
        out_shape = [jax.ShapeDtypeStruct((n, d), F32), jax.ShapeDtypeStruct((n, d), BF16)]
    else:
        out_specs = row
        out_shape = jax.ShapeDtypeStruct((n, d), F32)
    w_spec = pl.BlockSpec((None, d, d), lambda i: (layer, 0, 0), pipeline_mode=pl.Buffered(1))
    return pl.pallas_call(
        functools.partial(_out_kernel, emit_x=emit_x, sub=sub),
        grid=(n // tm,),
        in_specs=[row, w_spec, row, pl.BlockSpec((1, d), lambda i: (0, 0))],
        out_specs=out_specs,
        out_shape=out_shape,
        scratch_shapes=[pltpu.VMEM((d, d), BF16)],
        compiler_params=_params("arbitrary"),
        name="out_proj_residual_norm",
    )(m, w_o, x, g.reshape(1, d))


def kernel(x, norm_g, w_in, conv_w, w_out_a, a_re, a_im, log_dt, b_re, b_im, c_re, c_im,
           d_skip, w_glu, b_glu, w_out_b, w_o, final_g):
    bsz, seq, d = x.shape
    depth = norm_g.shape[0]
    n = bsz * seq
    n_chunks = seq // SSM_CHUNK
    assert 2 * bsz == 8 and n_chunks % 2 == 0
    xf = x.reshape(n, d)
    h = _rmsnorm(xf, norm_g[0])
    for l in range(depth):
        a_in = _branch_a(h, w_in, conv_w, l, seq)
        u = _proj(h, w_in, l, OFF_U, SSM_WIDTH, "none", F32)
        szb = _proj(h, w_in, l, OFF_ZB, SSM_WIDTH, "silu", BF16)
        sg = _proj(h, w_in, l, OFF_G, 2 * D_MODEL, "sigmoid", BF16)

        m0, w_si, w_so, lam_re, lam_im = _ssm_weights(
            a_re[l], a_im[l], log_dt[l], b_re[l], b_im[l], c_re[l], c_im[l])
        y = _ssm(u.reshape(bsz, seq, SSM_WIDTH), m0, w_si, w_so, lam_re, lam_im).reshape(n, SSM_WIDTH)
        b_in = _post_b(y, u, szb, d_skip, w_glu, b_glu, l)
        m = _merge(a_in, b_in, w_out_a, w_out_b, sg, l)
        if l + 1 < depth:
            xf, h = _out_proj(m, w_o, l, xf, norm_g[l + 1], True)
        else:
            out = _out_proj(m, w_o, l, xf, final_g, False)
    return out.reshape(bsz, seq, d)
```

```python
import functools

import jax
import jax.numpy as jnp
from jax import lax
from jax.experimental import pallas as pl
from jax.experimental.pallas import tpu as pltpu

F32 = jnp.float32
BF16 = jnp.bfloat16

D_MODEL = 2048
CONV_WIDTH = D_MODEL
CONV_K = 3
SSM_WIDTH = D_MODEL // 2
SSM_GROUP = 16
SSM_GROUPS = SSM_WIDTH // SSM_GROUP
SSM_STATE = 64
RMS_EPS = 1e-6

OFF_V = 0
OFF_BG = CONV_WIDTH
OFF_CG = 2 * CONV_WIDTH
OFF_ZA = 3 * CONV_WIDTH
OFF_U = 4 * CONV_WIDTH
OFF_ZB = OFF_U + SSM_WIDTH
OFF_G = OFF_ZB + SSM_WIDTH

SSM_CHUNK = 16
CHUNK_LANES = SSM_CHUNK * SSM_GROUP
LANE_TILE = 128
STATE_LANES = LANE_TILE
GROUPS_PER_STEP = 8
PREP_ROWS = 24

V7X_VMEM_LIMIT = 56 * 1024 * 1024


def _sigmoid(x):
    return 0.5 * (jnp.tanh(0.5 * x) + 1.0)


def _params(*sem):
    return pltpu.CompilerParams(dimension_semantics=sem, vmem_limit_bytes=V7X_VMEM_LIMIT)


def _rmsnorm_kernel(x_ref, g_ref, o_ref):
    x = x_ref[...]
    ms = jnp.mean(x * x, axis=-1, keepdims=True)
    o_ref[...] = (x * lax.rsqrt(ms + RMS_EPS) * g_ref[...]).astype(o_ref.dtype)


def _rmsnorm(x, g, tm=512):
    n, d = x.shape
    return pl.pallas_call(
        _rmsnorm_kernel,
        grid=(n // tm,),
        in_specs=[pl.BlockSpec((tm, d), lambda i: (i, 0)),
                  pl.BlockSpec((1, d), lambda i: (0, 0))],
        out_specs=pl.BlockSpec((tm, d), lambda i: (i, 0)),
        out_shape=jax.ShapeDtypeStruct((n, d), BF16),
        compiler_params=_params("arbitrary"),
        name="rmsnorm",
    )(x, g.reshape(1, d))


def _branch_a_kernel(h_ref, wv_ref, wbg_ref, wcg_ref, wza_ref, cw_ref, o_ref, carry_ref, w_ref,
                     *, tiles_per_seq, sub):
    i = pl.program_id(1)

    @pl.when(i == 0)
    def _():
        for k, src in enumerate((wv_ref, wbg_ref, wcg_ref, wza_ref)):
            w_ref[k] = src[...].astype(BF16)

    @pl.when(i % tiles_per_seq == 0)
    def _():
        carry_ref[...] = jnp.zeros_like(carry_ref)

    cw = cw_ref[...]
    rs = h_ref.shape[0] // sub
    row = lax.broadcasted_iota(jnp.int32, (rs, o_ref.shape[1]), 0)
    tail = carry_ref[...]
    for r in range(sub):
        h = h_ref[r * rs:(r + 1) * rs, :]
        v = jnp.dot(h, w_ref[0], preferred_element_type=F32)
        bg = jnp.dot(h, w_ref[1], preferred_element_type=F32)
        cg = jnp.dot(h, w_ref[2], preferred_element_type=F32)
        za = jnp.dot(h, w_ref[3], preferred_element_type=F32)
        cv = cg * v
        prev1 = tail[7:8, :]
        prev2 = tail[6:7, :]
        cv1 = jnp.where(row == 0, prev1, pltpu.roll(cv, 1, axis=0))
        cv2 = jnp.where(row == 0, prev2, jnp.where(row == 1, prev1, pltpu.roll(cv, 2, axis=0)))
        conv = cw[0:1, :] * cv2 + cw[1:2, :] * cv1 + cw[2:3, :] * cv
        o_ref[r * rs:(r + 1) * rs, :] = (bg * conv * (za * _sigmoid(za))).astype(o_ref.dtype)
        tail = cv[rs - 8:rs, :]
    carry_ref[...] = tail


def _branch_a(h, w_in, conv_w, layer, seq_len, tm=1024, tn=256, sub=4):
    n, d = h.shape
    nj = CONV_WIDTH // tn

    def wspec(off):
        return pl.BlockSpec((None, d, tn), lambda j, i, o=off // tn: (layer, 0, o + j))

    return pl.pallas_call(
        functools.partial(_branch_a_kernel, tiles_per_seq=seq_len // tm, sub=sub),
        grid=(nj, n // tm),
        in_specs=[pl.BlockSpec((tm, d), lambda j, i: (i, 0)),
                  wspec(OFF_V), wspec(OFF_BG), wspec(OFF_CG), wspec(OFF_ZA),
                  pl.BlockSpec((None, CONV_K, tn), lambda j, i: (layer, 0, j))],
        out_specs=pl.BlockSpec((tm, tn), lambda j, i: (i, j)),
        out_shape=jax.ShapeDtypeStruct((n, CONV_WIDTH), BF16),
        scratch_shapes=[pltpu.VMEM((8, tn), F32), pltpu.VMEM((4, d, tn), BF16)],
        compiler_params=_params("arbitrary", "arbitrary"),
        name="branch_a_proj_conv",
    )(h, w_in, w_in, w_in, w_in, conv_w)


def _proj_kernel(h_ref, w32_ref, o_ref, w_ref, *, act, sub):
    @pl.when(pl.program_id(1) == 0)
    def _():
        w_ref[...] = w32_ref[...].astype(BF16)

    rs = h_ref.shape[0] // sub
    for r in range(sub):
        rows = slice(r * rs, (r + 1) * rs)
        y = jnp.dot(h_ref[rows, :], w_ref[...], preferred_element_type=F32)
        if act == "silu":
            y = y * _sigmoid(y)
        elif act == "sigmoid":
            y = _sigmoid(y)
        o_ref[rows, :] = y.astype(o_ref.dtype)


def _proj(h, w_in, layer, col_off, width, act, out_dtype, tm=1024, tn=1024, sub=4):
    n, d = h.shape
    return pl.pallas_call(
        functools.partial(_proj_kernel, act=act, sub=sub),
        grid=(width // tn, n // tm),
        in_specs=[pl.BlockSpec((tm, d), lambda j, i: (i, 0)),
                  pl.BlockSpec((None, d, tn), lambda j, i, o=col_off // tn: (layer, 0, o + j))],
        out_specs=pl.BlockSpec((tm, tn), lambda j, i: (i, j)),
        out_shape=jax.ShapeDtypeStruct((n, width), out_dtype),
        scratch_shapes=[pltpu.VMEM((d, tn), BF16)],
        compiler_params=_params("arbitrary", "arbitrary"),
        name="proj_" + act,
    )(h, w_in)


def _ssm_kernel(u_ref, m0_ref, win_ref, wout_ref, lre_ref, lim_ref, y_ref,
                t_ref, r_ref, sre_ref, sim_ref, hre_ref, him_ref, *, n_chunks, batch):
    gb = m0_ref.shape[0]
    t = SSM_CHUNK
    for b in range(batch):
        for s in range(t):
            xt = u_ref[b, pl.ds(s, n_chunks, stride=t), :].T
            for g in range(gb):
                row0 = g * CHUNK_LANES + s * SSM_GROUP
                t_ref[b, row0:row0 + SSM_GROUP, :] = xt[g * SSM_GROUP:(g + 1) * SSM_GROUP, :]
    for b in range(batch):
        for g in range(gb):
            a = t_ref[b, g * CHUNK_LANES:(g + 1) * CHUNK_LANES, :].T
            for hf in range(2):
                r_ref[g, hf, pl.ds(b, n_chunks, stride=batch), :] = a[:, hf * LANE_TILE:(hf + 1) * LANE_TILE]

    def u_of(g):
        return jnp.concatenate([r_ref[g, 0], r_ref[g, 1]], axis=1).astype(BF16)

    for g in range(gb):
        s = jnp.dot(u_of(g), win_ref[g], preferred_element_type=F32)
        sre_ref[:, g * STATE_LANES:(g + 1) * STATE_LANES] = s[:, :STATE_LANES]
        sim_ref[:, g * STATE_LANES:(g + 1) * STATE_LANES] = s[:, STATE_LANES:]
    lre = lre_ref[0]
    lim = lim_ref[0]
    rows = 2 * batch
    first = lax.broadcasted_iota(jnp.int32, (rows, gb * STATE_LANES), 0) < batch

    def advance(h_r, h_i, s_r, s_i):
        return lre * h_r - lim * h_i + s_r, lre * h_i + lim * h_r + s_i

    def step(j, carry):
        h_r, h_i = carry
        r0 = pl.multiple_of(j * rows, rows)
        s_r = sre_ref[pl.ds(r0, rows), :]
        s_i = sim_ref[pl.ds(r0, rows), :]
        n_r, n_i = advance(h_r, h_i, s_r, s_i)
        h_r = jnp.where(first, h_r, pltpu.roll(n_r, batch, axis=0))
        h_i = jnp.where(first, h_i, pltpu.roll(n_i, batch, axis=0))
        hre_ref[pl.ds(r0, rows), :] = h_r
        him_ref[pl.ds(r0, rows), :] = h_i
        n_r, n_i = advance(h_r, h_i, s_r, s_i)
        return pltpu.roll(n_r, batch, axis=0), pltpu.roll(n_i, batch, axis=0)

    zero = jnp.zeros((rows, gb * STATE_LANES), F32)
    lax.fori_loop(0, n_chunks // 2, step, (zero, zero))

    for g in range(gb):
        h = jnp.concatenate([hre_ref[:, g * STATE_LANES:(g + 1) * STATE_LANES],
                             him_ref[:, g * STATE_LANES:(g + 1) * STATE_LANES]], axis=1)
        y = jnp.dot(u_of(g), m0_ref[g], preferred_element_type=F32)
        y = y + jnp.dot(h.astype(BF16), wout_ref[g], preferred_element_type=F32)
        for hf in range(2):
            r_ref[g, hf] = y[:, hf * LANE_TILE:(hf + 1) * LANE_TILE]

    for b in range(batch):
        for g in range(gb):
            a = jnp.concatenate([r_ref[g, hf, pl.ds(b, n_chunks, stride=batch), :] for hf in range(2)],
                                axis=1)
            at = a.T
            for tt in range(t):
                row0 = tt * LANE_TILE + g * SSM_GROUP
                t_ref[b, row0:row0 + SSM_GROUP, :] = at[tt * SSM_GROUP:(tt + 1) * SSM_GROUP, :]
    for b in range(batch):
        for tt in range(t):
            y_ref[b, pl.ds(tt, n_chunks, stride=t), :] = t_ref[b, tt * LANE_TILE:(tt + 1) * LANE_TILE, :].T


def _ssm(u_p, m0, w_in_state, w_out_state, lam_re, lam_im):
    batch, seq, width = u_p.shape
    n_inst = batch * seq // SSM_CHUNK
    gb = GROUPS_PER_STEP
    assert gb * SSM_GROUP == LANE_TILE and n_inst // batch == LANE_TILE
    blk = pl.BlockSpec((batch, seq, LANE_TILE), lambda i: (0, 0, i))
    sq = pl.BlockSpec((gb, CHUNK_LANES, CHUNK_LANES), lambda i: (i, 0, 0))
    vec = pl.BlockSpec((1, 1, gb * STATE_LANES), lambda i: (i, 0, 0))
    return pl.pallas_call(
        functools.partial(_ssm_kernel, n_chunks=n_inst // batch, batch=batch),
        grid=(width // LANE_TILE,),
        in_specs=[blk, sq, sq, sq, vec, vec],
        out_specs=blk,
        out_shape=jax.ShapeDtypeStruct(u_p.shape, F32),
        scratch_shapes=[pltpu.VMEM((batch, gb * CHUNK_LANES, LANE_TILE), F32),
                        pltpu.VMEM((gb, 2, n_inst, LANE_TILE), F32)]
                       + [pltpu.VMEM((n_inst, gb * STATE_LANES), F32) for _ in range(4)],
        compiler_params=_params("arbitrary"),
        name="s5_chunked_scan",
    )(u_p, m0, w_in_state, w_out_state, lam_re, lam_im)


def _split_bf16(x):
    hi = x.astype(BF16)
    return hi, (x - hi.astype(F32)).astype(BF16)


def _dot_nt_f32(a, b):
    dn = (((1,), (1,)), ((), ()))
    a_hi, a_lo = _split_bf16(a)
    b_hi, b_lo = _split_bf16(b)

    def d(x, y):
        return lax.dot_general(x, y, dn, preferred_element_type=F32)

    return d(a_hi, b_hi) + d(a_hi, b_lo) + d(a_lo, b_hi)


def _ssm_prep_kernel(are_ref, aim_ref, ldt_ref, btr_ref, bti_ref, cre_ref, cim_ref,
                     m0_ref, win_ref, wout_ref, lre_ref, lim_ref):
    gb = are_ref.shape[0]
    t = SSM_CHUNK
    tau = lax.broadcasted_iota(jnp.int32, (PREP_ROWS, STATE_LANES), 0).astype(F32)
    lane = lax.broadcasted_iota(jnp.int32, (SSM_GROUP, CHUNK_LANES), 1)
    for g in range(gb):
        are = are_ref[g]
        aim = aim_ref[g]
        dt = jnp.exp(ldt_ref[g])
        mag = jnp.exp(tau * (dt * are))
        ang = tau * (dt * aim)
        pw_re = mag * jnp.cos(ang)
        pw_im = mag * jnp.sin(ang)
        nr = pw_re[1:2] - 1.0
        ni = pw_im[1:2]
        den = are * are + aim * aim
        cr = (nr * are + ni * aim) / den
        ci = (ni * are - nr * aim) / den
        btr = btr_ref[g]
        bti = bti_ref[g]
        bb_re = cr * btr - ci * bti
        bb_im = cr * bti + ci * btr
        cre = cre_ref[g]
        cim = cim_ref[g]

        win_re, win_im = [], []
        for s in range(t):
            pr = pw_re[t - 1 - s:t - s]
            pi = pw_im[t - 1 - s:t - s]
            win_re.append(pr * bb_re - pi * bb_im)
            win_im.append(pr * bb_im + pi * bb_re)
        win_ref[g] = jnp.concatenate(
            [jnp.concatenate(win_re, axis=0), jnp.concatenate(win_im, axis=0)], axis=1).astype(BF16)

        z_re, z_im = [], []
        for k in range(t + 1):
            pr = pw_re[k:k + 1]
            pi = pw_im[k:k + 1]
            z_re.append(cre * pr - cim * pi)
            z_im.append(cre * pi + cim * pr)

        zo_re = jnp.concatenate(z_re[1:], axis=0)
        zo_im = jnp.concatenate(z_im[1:], axis=0)
        wout_ref[g] = jnp.concatenate([zo_re.T, -(zo_im.T)], axis=0).astype(BF16)

        zk_re = jnp.concatenate(z_re[:t], axis=0)
        zk_im = jnp.concatenate(z_im[:t], axis=0)
        kt = _dot_nt_f32(bb_re, zk_re) - _dot_nt_f32(bb_im, zk_im)
        blocks = [kt]
        for s in range(1, t):
            shifted = pltpu.roll(kt, s * SSM_GROUP, axis=1)
            blocks.append(jnp.where(lane >= s * SSM_GROUP, shifted, 0.0))
        m0_ref[g] = jnp.concatenate(blocks, axis=0).astype(BF16)

        lre_ref[0, :, g * STATE_LANES:(g + 1) * STATE_LANES] = pw_re[t:t + 1]
        lim_ref[0, :, g * STATE_LANES:(g + 1) * STATE_LANES] = pw_im[t:t + 1]


def _ssm_weights(a_re, a_im, log_dt, b_re, b_im, c_re, c_im):
    g, p = a_re.shape
    gb = GROUPS_PER_STEP
    pad = STATE_LANES - p

    def lanes(x, value=0.0):
        return jnp.pad(x.astype(F32), [(0, 0)] * (x.ndim - 1) + [(0, pad)], constant_values=value)

    are = lanes(a_re, -1.0).reshape(g, 1, STATE_LANES)
    aim = lanes(a_im).reshape(g, 1, STATE_LANES)
    ldt = jnp.broadcast_to(log_dt.astype(F32)[:, None, None], (g, 1, STATE_LANES))
    btr = lanes(jnp.swapaxes(b_re, 1, 2))
    bti = lanes(jnp.swapaxes(b_im, 1, 2))
    cre = lanes(c_re)
    cim = lanes(c_im)
    vec = pl.BlockSpec((gb, 1, STATE_LANES), lambda i: (i, 0, 0))
    mat = pl.BlockSpec((gb, SSM_GROUP, STATE_LANES), lambda i: (i, 0, 0))
    sq = pl.BlockSpec((gb, CHUNK_LANES, CHUNK_LANES), lambda i: (i, 0, 0))
    row = pl.BlockSpec((1, 1, gb * STATE_LANES), lambda i: (i, 0, 0))
    sq_shape = jax.ShapeDtypeStruct((g, CHUNK_LANES, CHUNK_LANES), BF16)
    row_shape = jax.ShapeDtypeStruct((g // gb, 1, gb * STATE_LANES), F32)
    return pl.pallas_call(
        _ssm_prep_kernel,
        grid=(g // gb,),
        in_specs=[vec, vec, vec, mat, mat, mat, mat],
        out_specs=[sq, sq, sq, row, row],
        out_shape=[sq_shape, sq_shape, sq_shape, row_shape, row_shape],
        compiler_params=_params("arbitrary"),
        name="s5_chunk_operators",
    )(are, aim, ldt, btr, bti, cre, cim)


def _post_b_kernel(y_ref, u_ref, szb_ref, d_ref, wg32_ref, bgl_ref, o_ref, wg_ref):
    @pl.when(pl.program_id(0) == 0)
    def _():
        wg_ref[...] = wg32_ref[...].astype(BF16)

    yb = jax.nn.gelu(y_ref[...] + d_ref[...] * u_ref[...])
    z = jnp.dot(yb.astype(BF16), wg_ref[...], preferred_element_type=F32) + bgl_ref[...]
    o_ref[...] = (yb * _sigmoid(z) * szb_ref[...].astype(F32)).astype(o_ref.dtype)


def _post_b(y, u, szb, d_skip, w_glu, b_glu, layer, tm=512):
    n, w = y.shape
    row = pl.BlockSpec((tm, w), lambda i: (i, 0))
    vec = pl.BlockSpec((None, 1, w), lambda i: (layer, 0, 0))
    return pl.pallas_call(
        _post_b_kernel,
        grid=(n // tm,),
        in_specs=[row, row, row, vec, pl.BlockSpec((None, w, w), lambda i: (layer, 0, 0)), vec],
        out_specs=row,
        out_shape=jax.ShapeDtypeStruct((n, w), BF16),
        scratch_shapes=[pltpu.VMEM((w, w), BF16)],
        compiler_params=_params("arbitrary"),
        name="branch_b_glu",
    )(y, u, szb, d_skip.reshape(-1, 1, w), w_glu, b_glu.reshape(-1, 1, w))


def _merge_kernel(a_ref, b_ref, wa32_ref, wb32_ref, sga_ref, sgb_ref, o_ref, wa_ref, wb_ref, *, sub):
    @pl.when(pl.program_id(1) == 0)
    def _():
        wa_ref[...] = wa32_ref[...].astype(BF16)
        wb_ref[...] = wb32_ref[...].astype(BF16)

    rs = a_ref.shape[0] // sub
    for r in range(sub):
        rows = slice(r * rs, (r + 1) * rs)
        ya = jnp.dot(a_ref[rows, :], wa_ref[...], preferred_element_type=F32)
        yb = jnp.dot(b_ref[rows, :], wb_ref[...], preferred_element_type=F32)
        o_ref[rows, :] = (sga_ref[rows, :].astype(F32) * ya
                          + sgb_ref[rows, :].astype(F32) * yb).astype(o_ref.dtype)


def _merge(a_in, b_in, w_a, w_b, sg, layer, tm=512, tn=1024, sub=2):
    n = a_in.shape[0]
    nj = D_MODEL // tn
    return pl.pallas_call(
        functools.partial(_merge_kernel, sub=sub),
        grid=(nj, n // tm),
        in_specs=[pl.BlockSpec((tm, CONV_WIDTH), lambda j, i: (i, 0)),
                  pl.BlockSpec((tm, SSM_WIDTH), lambda j, i: (i, 0)),
                  pl.BlockSpec((None, CONV_WIDTH, tn), lambda j, i: (layer, 0, j)),
                  pl.BlockSpec((None, SSM_WIDTH, tn), lambda j, i: (layer, 0, j)),
                  pl.BlockSpec((tm, tn), lambda j, i: (i, j)),
                  pl.BlockSpec((tm, tn), lambda j, i, o=nj: (i, o + j))],
        out_specs=pl.BlockSpec((tm, tn), lambda j, i: (i, j)),
        out_shape=jax.ShapeDtypeStruct((n, D_MODEL), BF16),
        scratch_shapes=[pltpu.VMEM((CONV_WIDTH, tn), BF16), pltpu.VMEM((SSM_WIDTH, tn), BF16)],
        compiler_params=_params("arbitrary", "arbitrary"),
        name="gated_merge",
    )(a_in, b_in, w_a, w_b, sg, sg)


def _out_kernel(m_ref, wo32_ref, x_ref, g_ref, *refs, emit_x, sub, w_steps):
    o_refs, wo_ref = refs[:-1], refs[-1]
    step = pl.program_id(0)
    wr = wo32_ref.shape[0]

    @pl.when(step < w_steps)
    def _():
        r0 = pl.multiple_of(step * wr, wr)
        wo_ref[pl.ds(r0, wr), :] = wo32_ref[...].astype(BF16)

    @pl.when(step >= w_steps)
    def _():
        rs = m_ref.shape[0] // sub
        for r in range(sub):
            rows = slice(r * rs, (r + 1) * rs)
            xn = x_ref[rows, :] + jnp.dot(m_ref[rows, :], wo_ref[...], preferred_element_type=F32)
            ms = jnp.mean(xn * xn, axis=-1, keepdims=True)
            normed = xn * lax.rsqrt(ms + RMS_EPS) * g_ref[...]
            if emit_x:
                o_refs[0][rows, :] = xn
                o_refs[1][rows, :] = normed.astype(o_refs[1].dtype)
            else:
                o_refs[0][rows, :] = normed.astype(o_refs[0].dtype)


def _out_proj(m, w_o, layer, x, g, emit_x, tm=512, sub=2, w_steps=4):
    n, d = x.shape
    row = pl.BlockSpec((tm, d), lambda s: (jnp.maximum(s - w_steps, 0), 0))
    if emit_x:
        out_specs = [row, row]
        out_shape = [jax.ShapeDtypeStruct((n, d), F32), jax.ShapeDtypeStruct((n, d), BF16)]
    else:
        out_specs = row
        out_shape = jax.ShapeDtypeStruct((n, d), F32)
    w_spec = pl.BlockSpec((None, d // w_steps, d), lambda s: (layer, jnp.minimum(s, w_steps - 1), 0))
    return pl.pallas_call(
        functools.partial(_out_kernel, emit_x=emit_x, sub=sub, w_steps=w_steps),
        grid=(w_steps + n // tm,),
        in_specs=[row, w_spec, row, pl.BlockSpec((1, d), lambda s: (0, 0))],
        out_specs=out_specs,
        out_shape=out_shape,
        scratch_shapes=[pltpu.VMEM((d, d), BF16)],
        compiler_params=_params("arbitrary"),
        name="out_proj_residual_norm",
    )(m, w_o, x, g.reshape(1, d))


def kernel(x, norm_g, w_in, conv_w, w_out_a, a_re, a_im, log_dt, b_re, b_im, c_re, c_im,
           d_skip, w_glu, b_glu, w_out_b, w_o, final_g):
    bsz, seq, d = x.shape
    depth = norm_g.shape[0]
    n = bsz * seq
    n_chunks = seq // SSM_CHUNK
    assert 2 * bsz == 8 and n_chunks % 2 == 0
    xf = x.reshape(n, d)
    h = _rmsnorm(xf, norm_g[0])
    for l in range(depth):
        a_in = _branch_a(h, w_in, conv_w, l, seq)
        u = _proj(h, w_in, l, OFF_U, SSM_WIDTH, "none", F32)
        szb = _proj(h, w_in, l, OFF_ZB, SSM_WIDTH, "silu", BF16)
        sg = _proj(h, w_in, l, OFF_G, 2 * D_MODEL, "sigmoid", BF16)

        m0, w_si, w_so, lam_re, lam_im = _ssm_weights(
            a_re[l], a_im[l], log_dt[l], b_re[l], b_im[l], c_re[l], c_im[l])
        y = _ssm(u.reshape(bsz, seq, SSM_WIDTH), m0, w_si, w_so, lam_re, lam_im).reshape(n, SSM_WIDTH)
        b_in = _post_b(y, u, szb, d_skip, w_glu, b_glu, l)
        m = _merge(a_in, b_in, w_out_a, w_out_b, sg, l)
        if l + 1 < depth:
            xf, h = _out_proj(m, w_o, l, xf, norm_g[l + 1], True)
        else:
            out = _out_proj(m, w_o, l, xf, final_g, False)
    return out.reshape(bsz, seq, d)
```

```python
import functools

import jax
import jax.numpy as jnp
from jax import lax
from jax.experimental import pallas as pl
from jax.experimental.pallas import tpu as pltpu

F32 = jnp.float32
BF16 = jnp.bfloat16

D_MODEL = 2048
CONV_WIDTH = D_MODEL
CONV_K = 3
SSM_WIDTH = D_MODEL // 2
SSM_GROUP = 16
SSM_GROUPS = SSM_WIDTH // SSM_GROUP
SSM_STATE = 64
RMS_EPS = 1e-6

OFF_V = 0
OFF_BG = CONV_WIDTH
OFF_CG = 2 * CONV_WIDTH
OFF_ZA = 3 * CONV_WIDTH
OFF_U = 4 * CONV_WIDTH
OFF_ZB = OFF_U + SSM_WIDTH
OFF_G = OFF_ZB + SSM_WIDTH

SSM_CHUNK = 16
CHUNK_LANES = SSM_CHUNK * SSM_GROUP
LANE_TILE = 128
STATE_LANES = LANE_TILE
GROUPS_PER_STEP = 8
PREP_ROWS = 24

V7X_VMEM_LIMIT = 56 * 1024 * 1024


def _sigmoid(x):
    return 0.5 * (jnp.tanh(0.5 * x) + 1.0)


def _params(*sem):
    return pltpu.CompilerParams(dimension_semantics=sem, vmem_limit_bytes=V7X_VMEM_LIMIT)


def _rmsnorm_kernel(x_ref, g_ref, o_ref):
    x = x_ref[...]
    ms = jnp.mean(x * x, axis=-1, keepdims=True)
    o_ref[...] = (x * lax.rsqrt(ms + RMS_EPS) * g_ref[...]).astype(o_ref.dtype)


def _rmsnorm(x, g, tm=512):
    n, d = x.shape
    return pl.pallas_call(
        _rmsnorm_kernel,
        grid=(n // tm,),
        in_specs=[pl.BlockSpec((tm, d), lambda i: (i, 0)),
                  pl.BlockSpec((1, d), lambda i: (0, 0))],
        out_specs=pl.BlockSpec((tm, d), lambda i: (i, 0)),
        out_shape=jax.ShapeDtypeStruct((n, d), BF16),
        compiler_params=_params("arbitrary"),
        name="rmsnorm",
    )(x, g.reshape(1, d))


def _in_proj_a_kernel(h_ref, wv_ref, wbg_ref, wcg_ref, wza_ref, wga_ref, wgb_ref, cw_ref,
                      a_ref, sga_ref, sgb_ref, carry_ref, w_ref, *, tiles_per_seq, sub):
    i = pl.program_id(1)

    @pl.when(i == 0)
    def _():
        for k, src in enumerate((wv_ref, wbg_ref, wcg_ref, wza_ref, wga_ref, wgb_ref)):
            w_ref[k] = src[...].astype(BF16)

    @pl.when(i % tiles_per_seq == 0)
    def _():
        carry_ref[...] = jnp.zeros_like(carry_ref)

    cw = cw_ref[...]
    rs = h_ref.shape[0] // sub
    row = lax.broadcasted_iota(jnp.int32, (rs, a_ref.shape[1]), 0)
    tail = carry_ref[...]
    for r in range(sub):
        rows = slice(r * rs, (r + 1) * rs)
        h = h_ref[rows, :]
        v = jnp.dot(h, w_ref[0], preferred_element_type=F32)
        bg = jnp.dot(h, w_ref[1], preferred_element_type=F32)
        cg = jnp.dot(h, w_ref[2], preferred_element_type=F32)
        za = jnp.dot(h, w_ref[3], preferred_element_type=F32)
        cv = cg * v
        prev1 = tail[7:8, :]
        prev2 = tail[6:7, :]
        cv1 = jnp.where(row == 0, prev1, pltpu.roll(cv, 1, axis=0))
        cv2 = jnp.where(row == 0, prev2, jnp.where(row == 1, prev1, pltpu.roll(cv, 2, axis=0)))
        conv = cw[0:1, :] * cv2 + cw[1:2, :] * cv1 + cw[2:3, :] * cv
        a_ref[rows, :] = (bg * conv * (za * _sigmoid(za))).astype(a_ref.dtype)
        tail = cv[rs - 8:rs, :]
        ga = jnp.dot(h, w_ref[4], preferred_element_type=F32)
        sga_ref[rows, :] = _sigmoid(ga).astype(sga_ref.dtype)
        gb = jnp.dot(h, w_ref[5], preferred_element_type=F32)
        sgb_ref[rows, :] = _sigmoid(gb).astype(sgb_ref.dtype)
    carry_ref[...] = tail


def _in_proj_a(h, w_in, conv_w, layer, seq_len, tm=1024, tn=256, sub=4):
    n, d = h.shape
    nj = CONV_WIDTH // tn
    offsets = (OFF_V, OFF_BG, OFF_CG, OFF_ZA, OFF_G, OFF_G + D_MODEL)

    def wspec(off):
        return pl.BlockSpec((None, d, tn), lambda j, i, o=off // tn: (layer, 0, o + j))

    out = pl.BlockSpec((tm, tn), lambda j, i: (i, j))
    shape = jax.ShapeDtypeStruct((n, CONV_WIDTH), BF16)
    return pl.pallas_call(
        functools.partial(_in_proj_a_kernel, tiles_per_seq=seq_len // tm, sub=sub),
        grid=(nj, n // tm),
        in_specs=[pl.BlockSpec((tm, d), lambda j, i: (i, 0))] + [wspec(o) for o in offsets]
                 + [pl.BlockSpec((None, CONV_K, tn), lambda j, i: (layer, 0, j))],
        out_specs=[out, out, out],
        out_shape=[shape, shape, shape],
        scratch_shapes=[pltpu.VMEM((8, tn), F32), pltpu.VMEM((len(offsets), d, tn), BF16)],
        compiler_params=_params("arbitrary", "arbitrary"),
        name="in_proj_conv_gates",
    )(h, *([w_in] * len(offsets)), conv_w)


def _in_proj_b_kernel(h_ref, wu_ref, wzb_ref, u_ref, szb_ref, w_ref, *, sub):
    @pl.when(pl.program_id(1) == 0)
    def _():
        w_ref[0] = wu_ref[...].astype(BF16)
        w_ref[1] = wzb_ref[...].astype(BF16)

    rs = h_ref.shape[0] // sub
    for r in range(sub):
        rows = slice(r * rs, (r + 1) * rs)
        h = h_ref[rows, :]
        u_ref[rows, :] = jnp.dot(h, w_ref[0], preferred_element_type=F32)
        zb = jnp.dot(h, w_ref[1], preferred_element_type=F32)
        szb_ref[rows, :] = (zb * _sigmoid(zb)).astype(szb_ref.dtype)


def _in_proj_b(h, w_in, layer, tm=1024, tn=512, sub=4):
    n, d = h.shape

    def wspec(off):
        return pl.BlockSpec((None, d, tn), lambda j, i, o=off // tn: (layer, 0, o + j))

    out = pl.BlockSpec((tm, tn), lambda j, i: (i, j))
    return pl.pallas_call(
        functools.partial(_in_proj_b_kernel, sub=sub),
        grid=(SSM_WIDTH // tn, n // tm),
        in_specs=[pl.BlockSpec((tm, d), lambda j, i: (i, 0)), wspec(OFF_U), wspec(OFF_ZB)],
        out_specs=[out, out],
        out_shape=[jax.ShapeDtypeStruct((n, SSM_WIDTH), F32), jax.ShapeDtypeStruct((n, SSM_WIDTH), BF16)],
        scratch_shapes=[pltpu.VMEM((2, d, tn), BF16)],
        compiler_params=_params("arbitrary", "arbitrary"),
        name="in_proj_ssm",
    )(h, w_in, w_in)


def _ssm_kernel(u_ref, m0_ref, win_ref, wout_ref, lre_ref, lim_ref, y_ref,
                t_ref, r_ref, sre_ref, sim_ref, hre_ref, him_ref, *, n_chunks, batch):
    gb = m0_ref.shape[0]
    t = SSM_CHUNK
    for b in range(batch):
        for s in range(t):
            xt = u_ref[b, pl.ds(s, n_chunks, stride=t), :].T
            for g in range(gb):
                row0 = g * CHUNK_LANES + s * SSM_GROUP
                t_ref[b, row0:row0 + SSM_GROUP, :] = xt[g * SSM_GROUP:(g + 1) * SSM_GROUP, :]
    for b in range(batch):
        for g in range(gb):
            a = t_ref[b, g * CHUNK_LANES:(g + 1) * CHUNK_LANES, :].T
            for hf in range(2):
                r_ref[g, hf, pl.ds(b, n_chunks, stride=batch), :] = a[:, hf * LANE_TILE:(hf + 1) * LANE_TILE]

    def u_of(g):
        return jnp.concatenate([r_ref[g, 0], r_ref[g, 1]], axis=1).astype(BF16)

    for g in range(gb):
        s = jnp.dot(u_of(g), win_ref[g], preferred_element_type=F32)
        sre_ref[:, g * STATE_LANES:(g + 1) * STATE_LANES] = s[:, :STATE_LANES]
        sim_ref[:, g * STATE_LANES:(g + 1) * STATE_LANES] = s[:, STATE_LANES:]
    lre = lre_ref[0]
    lim = lim_ref[0]
    rows = 2 * batch
    first = lax.broadcasted_iota(jnp.int32, (rows, gb * STATE_LANES), 0) < batch

    def advance(h_r, h_i, s_r, s_i):
        return lre * h_r - lim * h_i + s_r, lre * h_i + lim * h_r + s_i

    def step(j, carry):
        h_r, h_i = carry
        r0 = pl.multiple_of(j * rows, rows)
        s_r = sre_ref[pl.ds(r0, rows), :]
        s_i = sim_ref[pl.ds(r0, rows), :]
        n_r, n_i = advance(h_r, h_i, s_r, s_i)
        h_r = jnp.where(first, h_r, pltpu.roll(n_r, batch, axis=0))
        h_i = jnp.where(first, h_i, pltpu.roll(n_i, batch, axis=0))
        hre_ref[pl.ds(r0, rows), :] = h_r
        him_ref[pl.ds(r0, rows), :] = h_i
        n_r, n_i = advance(h_r, h_i, s_r, s_i)
        return pltpu.roll(n_r, batch, axis=0), pltpu.roll(n_i, batch, axis=0)

    zero = jnp.zeros((rows, gb * STATE_LANES), F32)
    lax.fori_loop(0, n_chunks // 2, step, (zero, zero))

    for g in range(gb):
        h = jnp.concatenate([hre_ref[:, g * STATE_LANES:(g + 1) * STATE_LANES],
                             him_ref[:, g * STATE_LANES:(g + 1) * STATE_LANES]], axis=1)
        y = jnp.dot(u_of(g), m0_ref[g], preferred_element_type=F32)
        y = y + jnp.dot(h.astype(BF16), wout_ref[g], preferred_element_type=F32)
        for hf in range(2):
            r_ref[g, hf] = y[:, hf * LANE_TILE:(hf + 1) * LANE_TILE]

    for b in range(batch):
        for g in range(gb):
            a = jnp.concatenate([r_ref[g, hf, pl.ds(b, n_chunks, stride=batch), :] for hf in range(2)],
                                axis=1)
            at = a.T
            for tt in range(t):
                row0 = tt * LANE_TILE + g * SSM_GROUP
                t_ref[b, row0:row0 + SSM_GROUP, :] = at[tt * SSM_GROUP:(tt + 1) * SSM_GROUP, :]
    for b in range(batch):
        for tt in range(t):
            y_ref[b, pl.ds(tt, n_chunks, stride=t), :] = t_ref[b, tt * LANE_TILE:(tt + 1) * LANE_TILE, :].T


def _ssm(u_p, m0, w_in_state, w_out_state, lam_re, lam_im, layer):
    batch, seq, width = u_p.shape
    n_inst = batch * seq // SSM_CHUNK
    gb = GROUPS_PER_STEP
    assert gb * SSM_GROUP == LANE_TILE and n_inst // batch == LANE_TILE
    first = layer * (width // LANE_TILE)
    blk = pl.BlockSpec((batch, seq, LANE_TILE), lambda i: (0, 0, i))
    sq = pl.BlockSpec((gb, CHUNK_LANES, CHUNK_LANES), lambda i: (first + i, 0, 0))
    vec = pl.BlockSpec((1, 1, gb * STATE_LANES), lambda i: (first + i, 0, 0))
    return pl.pallas_call(
        functools.partial(_ssm_kernel, n_chunks=n_inst // batch, batch=batch),
        grid=(width // LANE_TILE,),
        in_specs=[blk, sq, sq, sq, vec, vec],
        out_specs=blk,
        out_shape=jax.ShapeDtypeStruct(u_p.shape, F32),
        scratch_shapes=[pltpu.VMEM((batch, gb * CHUNK_LANES, LANE_TILE), F32),
                        pltpu.VMEM((gb, 2, n_inst, LANE_TILE), F32)]
                       + [pltpu.VMEM((n_inst, gb * STATE_LANES), F32) for _ in range(4)],
        compiler_params=_params("arbitrary"),
        name="s5_chunked_scan",
    )(u_p, m0, w_in_state, w_out_state, lam_re, lam_im)


def _split_bf16(x):
    hi = x.astype(BF16)
    return hi, (x - hi.astype(F32)).astype(BF16)


def _dot_nt_f32(a, b):
    dn = (((1,), (1,)), ((), ()))
    a_hi, a_lo = _split_bf16(a)
    b_hi, b_lo = _split_bf16(b)

    def d(x, y):
        return lax.dot_general(x, y, dn, preferred_element_type=F32)

    return d(a_hi, b_hi) + d(a_hi, b_lo) + d(a_lo, b_hi)


def _ssm_prep_kernel(are_ref, aim_ref, ldt_ref, btr_ref, bti_ref, cre_ref, cim_ref,
                     m0_ref, win_ref, wout_ref, lre_ref, lim_ref):
    gb = are_ref.shape[0]
    t = SSM_CHUNK
    tau = lax.broadcasted_iota(jnp.int32, (PREP_ROWS, STATE_LANES), 0).astype(F32)
    lane = lax.broadcasted_iota(jnp.int32, (SSM_GROUP, CHUNK_LANES), 1)
    for g in range(gb):
        are = are_ref[g]
        aim = aim_ref[g]
        dt = jnp.exp(ldt_ref[g])
        mag = jnp.exp(tau * (dt * are))
        ang = tau * (dt * aim)
        pw_re = mag * jnp.cos(ang)
        pw_im = mag * jnp.sin(ang)
        nr = pw_re[1:2] - 1.0
        ni = pw_im[1:2]
        den = are * are + aim * aim
        cr = (nr * are + ni * aim) / den
        ci = (ni * are - nr * aim) / den
        btr = btr_ref[g]
        bti = bti_ref[g]
        bb_re = cr * btr - ci * bti
        bb_im = cr * bti + ci * btr
        cre = cre_ref[g]
        cim = cim_ref[g]

        win_re, win_im = [], []
        for s in range(t):
            pr = pw_re[t - 1 - s:t - s]
            pi = pw_im[t - 1 - s:t - s]
            win_re.append(pr * bb_re - pi * bb_im)
            win_im.append(pr * bb_im + pi * bb_re)
        win_ref[g] = jnp.concatenate(
            [jnp.concatenate(win_re, axis=0), jnp.concatenate(win_im, axis=0)], axis=1).astype(BF16)

        z_re, z_im = [], []
        for k in range(t + 1):
            pr = pw_re[k:k + 1]
            pi = pw_im[k:k + 1]
            z_re.append(cre * pr - cim * pi)
            z_im.append(cre * pi + cim * pr)

        zo_re = jnp.concatenate(z_re[1:], axis=0)
        zo_im = jnp.concatenate(z_im[1:], axis=0)
        wout_ref[g] = jnp.concatenate([zo_re.T, -(zo_im.T)], axis=0).astype(BF16)

        zk_re = jnp.concatenate(z_re[:t], axis=0)
        zk_im = jnp.concatenate(z_im[:t], axis=0)
        kt = _dot_nt_f32(bb_re, zk_re) - _dot_nt_f32(bb_im, zk_im)
        blocks = [kt]
        for s in range(1, t):
            shifted = pltpu.roll(kt, s * SSM_GROUP, axis=1)
            blocks.append(jnp.where(lane >= s * SSM_GROUP, shifted, 0.0))
        m0_ref[g] = jnp.concatenate(blocks, axis=0).astype(BF16)

        lre_ref[0, :, g * STATE_LANES:(g + 1) * STATE_LANES] = pw_re[t:t + 1]
        lim_ref[0, :, g * STATE_LANES:(g + 1) * STATE_LANES] = pw_im[t:t + 1]


def _ssm_weights(a_re, a_im, log_dt, b_re, b_im, c_re, c_im):
    a_re, a_im, b_re, b_im, c_re, c_im = (
        x.reshape((-1,) + x.shape[2:]) for x in (a_re, a_im, b_re, b_im, c_re, c_im))
    log_dt = log_dt.reshape(-1)
    g, p = a_re.shape
    gb = GROUPS_PER_STEP
    pad = STATE_LANES - p

    def lanes(x, value=0.0):
        return jnp.pad(x.astype(F32), [(0, 0)] * (x.ndim - 1) + [(0, pad)], constant_values=value)

    are = lanes(a_re, -1.0).reshape(g, 1, STATE_LANES)
    aim = lanes(a_im).reshape(g, 1, STATE_LANES)
    ldt = jnp.broadcast_to(log_dt.astype(F32)[:, None, None], (g, 1, STATE_LANES))
    btr = lanes(jnp.swapaxes(b_re, 1, 2))
    bti = lanes(jnp.swapaxes(b_im, 1, 2))
    cre = lanes(c_re)
    cim = lanes(c_im)
    vec = pl.BlockSpec((gb, 1, STATE_LANES), lambda i: (i, 0, 0))
    mat = pl.BlockSpec((gb, SSM_GROUP, STATE_LANES), lambda i: (i, 0, 0))
    sq = pl.BlockSpec((gb, CHUNK_LANES, CHUNK_LANES), lambda i: (i, 0, 0))
    row = pl.BlockSpec((1, 1, gb * STATE_LANES), lambda i: (i, 0, 0))
    sq_shape = jax.ShapeDtypeStruct((g, CHUNK_LANES, CHUNK_LANES), BF16)
    row_shape = jax.ShapeDtypeStruct((g // gb, 1, gb * STATE_LANES), F32)
    return pl.pallas_call(
        _ssm_prep_kernel,
        grid=(g // gb,),
        in_specs=[vec, vec, vec, mat, mat, mat, mat],
        out_specs=[sq, sq, sq, row, row],
        out_shape=[sq_shape, sq_shape, sq_shape, row_shape, row_shape],
        compiler_params=_params("arbitrary"),
        name="s5_chunk_operators",
    )(are, aim, ldt, btr, bti, cre, cim)


def _post_b_kernel(y_ref, u_ref, szb_ref, d_ref, wg32_ref, bgl_ref, o_ref, wg_ref):
    @pl.when(pl.program_id(0) == 0)
    def _():
        wg_ref[...] = wg32_ref[...].astype(BF16)

    yb = jax.nn.gelu(y_ref[...] + d_ref[...] * u_ref[...])
    z = jnp.dot(yb.astype(BF16), wg_ref[...], preferred_element_type=F32) + bgl_ref[...]
    o_ref[...] = (yb * _sigmoid(z) * szb_ref[...].astype(F32)).astype(o_ref.dtype)


def _post_b(y, u, szb, d_skip, w_glu, b_glu, layer, tm=512):
    n, w = y.shape
    row = pl.BlockSpec((tm, w), lambda i: (i, 0))
    vec = pl.BlockSpec((None, 1, w), lambda i: (layer, 0, 0))
    return pl.pallas_call(
        _post_b_kernel,
        grid=(n // tm,),
        in_specs=[row, row, row, vec, pl.BlockSpec((None, w, w), lambda i: (layer, 0, 0)), vec],
        out_specs=row,
        out_shape=jax.ShapeDtypeStruct((n, w), BF16),
        scratch_shapes=[pltpu.VMEM((w, w), BF16)],
        compiler_params=_params("arbitrary"),
        name="branch_b_glu",
    )(y, u, szb, d_skip.reshape(-1, 1, w), w_glu, b_glu.reshape(-1, 1, w))


def _merge_kernel(a_ref, b_ref, wa32_ref, wb32_ref, sga_ref, sgb_ref, o_ref, wa_ref, wb_ref, *, sub):
    @pl.when(pl.program_id(1) == 0)
    def _():
        wa_ref[...] = wa32_ref[...].astype(BF16)
        wb_ref[...] = wb32_ref[...].astype(BF16)

    rs = a_ref.shape[0] // sub
    for r in range(sub):
        rows = slice(r * rs, (r + 1) * rs)
        ya = jnp.dot(a_ref[rows, :], wa_ref[...], preferred_element_type=F32)
        yb = jnp.dot(b_ref[rows, :], wb_ref[...], preferred_element_type=F32)
        o_ref[rows, :] = (sga_ref[rows, :].astype(F32) * ya
                          + sgb_ref[rows, :].astype(F32) * yb).astype(o_ref.dtype)


def _merge(a_in, b_in, w_a, w_b, sga, sgb, layer, tm=512, tn=1024, sub=2):
    n = a_in.shape[0]
    nj = D_MODEL // tn
    return pl.pallas_call(
        functools.partial(_merge_kernel, sub=sub),
        grid=(nj, n // tm),
        in_specs=[pl.BlockSpec((tm, CONV_WIDTH), lambda j, i: (i, 0)),
                  pl.BlockSpec((tm, SSM_WIDTH), lambda j, i: (i, 0)),
                  pl.BlockSpec((None, CONV_WIDTH, tn), lambda j, i: (layer, 0, j)),
                  pl.BlockSpec((None, SSM_WIDTH, tn), lambda j, i: (layer, 0, j)),
                  pl.BlockSpec((tm, tn), lambda j, i: (i, j)),
                  pl.BlockSpec((tm, tn), lambda j, i: (i, j))],
        out_specs=pl.BlockSpec((tm, tn), lambda j, i: (i, j)),
        out_shape=jax.ShapeDtypeStruct((n, D_MODEL), BF16),
        scratch_shapes=[pltpu.VMEM((CONV_WIDTH, tn), BF16), pltpu.VMEM((SSM_WIDTH, tn), BF16)],
        compiler_params=_params("arbitrary", "arbitrary"),
        name="gated_merge",
    )(a_in, b_in, w_a, w_b, sga, sgb)


def _out_kernel(m_ref, wo32_ref, x_ref, g_ref, *refs, emit_x, sub, w_steps):
    o_refs, wo_ref = refs[:-1], refs[-1]
    step = pl.program_id(0)
    wr = wo32_ref.shape[0]

    @pl.when(step < w_steps)
    def _():
        r0 = pl.multiple_of(step * wr, wr)
        wo_ref[pl.ds(r0, wr), :] = wo32_ref[...].astype(BF16)

    @pl.when(step >= w_steps)
    def _():
        rs = m_ref.shape[0] // sub
        for r in range(sub):
            rows = slice(r * rs, (r + 1) * rs)
            xn = x_ref[rows, :] + jnp.dot(m_ref[rows, :], wo_ref[...], preferred_element_type=F32)
            ms = jnp.mean(xn * xn, axis=-1, keepdims=True)
            normed = xn * lax.rsqrt(ms + RMS_EPS) * g_ref[...]
            if emit_x:
                o_refs[0][rows, :] = xn
                o_refs[1][rows, :] = normed.astype(o_refs[1].dtype)
            else:
                o_refs[0][rows, :] = normed.astype(o_refs[0].dtype)


def _out_proj(m, w_o, layer, x, g, emit_x, tm=512, sub=2, w_steps=4):
    n, d = x.shape
    row = pl.BlockSpec((tm, d), lambda s: (jnp.maximum(s - w_steps, 0), 0))
    if emit_x:
        out_specs = [row, row]
        out_shape = [jax.ShapeDtypeStruct((n, d), F32), jax.ShapeDtypeStruct((n, d), BF16)]
    else:
        out_specs = row
        out_shape = jax.ShapeDtypeStruct((n, d), F32)
    w_spec = pl.BlockSpec((None, d // w_steps, d), lambda s: (layer, jnp.minimum(s, w_steps - 1), 0))
    return pl.pallas_call(
        functools.partial(_out_kernel, emit_x=emit_x, sub=sub, w_steps=w_steps),
        grid=(w_steps + n // tm,),
        in_specs=[row, w_spec, row, pl.BlockSpec((1, d), lambda s: (0, 0))],
        out_specs=out_specs,
        out_shape=out_shape,
        scratch_shapes=[pltpu.VMEM((d, d), BF16)],
        compiler_params=_params("arbitrary"),
        name="out_proj_residual_norm",
    )(m, w_o, x, g.reshape(1, d))


def kernel(x, norm_g, w_in, conv_w, w_out_a, a_re, a_im, log_dt, b_re, b_im, c_re, c_im,
           d_skip, w_glu, b_glu, w_out_b, w_o, final_g):
    bsz, seq, d = x.shape
    depth = norm_g.shape[0]
    n = bsz * seq
    n_chunks = seq // SSM_CHUNK
    assert 2 * bsz == 8 and n_chunks % 2 == 0
    xf = x.reshape(n, d)
    h = _rmsnorm(xf, norm_g[0])
    ssm_ops = _ssm_weights(a_re, a_im, log_dt, b_re, b_im, c_re, c_im)
    for l in range(depth):
        a_in, sga, sgb = _in_proj_a(h, w_in, conv_w, l, seq)
        u, szb = _in_proj_b(h, w_in, l)
        y = _ssm(u.reshape(bsz, seq, SSM_WIDTH), *ssm_ops, l).reshape(n, SSM_WIDTH)
        b_in = _post_b(y, u, szb, d_skip, w_glu, b_glu, l)
        m = _merge(a_in, b_in, w_out_a, w_out_b, sga, sgb, l)
        if l + 1 < depth:
            xf, h = _out_proj(m, w_o, l, xf, norm_g[l + 1], True)
        else:
            out = _out_proj(m, w_o, l, xf, final_g, False)
    return out.reshape(bsz, seq, d)
```

```python
import functools

import jax
import jax.numpy as jnp
from jax import lax
from jax.experimental import pallas as pl
from jax.experimental.pallas import tpu as pltpu

F32 = jnp.float32
BF16 = jnp.bfloat16

D_MODEL = 2048
CONV_WIDTH = D_MODEL
CONV_K = 3
SSM_WIDTH = D_MODEL // 2
SSM_GROUP = 16
SSM_GROUPS = SSM_WIDTH // SSM_GROUP
SSM_STATE = 64
RMS_EPS = 1e-6

OFF_V = 0
OFF_BG = CONV_WIDTH
OFF_CG = 2 * CONV_WIDTH
OFF_ZA = 3 * CONV_WIDTH
OFF_U = 4 * CONV_WIDTH
OFF_ZB = OFF_U + SSM_WIDTH
OFF_G = OFF_ZB + SSM_WIDTH

SSM_CHUNK = 16
CHUNK_LANES = SSM_CHUNK * SSM_GROUP
LANE_TILE = 128
STATE_LANES = LANE_TILE
GROUPS_PER_STEP = 8
PREP_ROWS = 24

V7X_VMEM_LIMIT = 56 * 1024 * 1024


def _sigmoid(x):
    return 0.5 * (jnp.tanh(0.5 * x) + 1.0)


def _params(*sem):
    return pltpu.CompilerParams(dimension_semantics=sem, vmem_limit_bytes=V7X_VMEM_LIMIT)


def _rmsnorm_kernel(x_ref, g_ref, o_ref):
    x = x_ref[...]
    ms = jnp.mean(x * x, axis=-1, keepdims=True)
    o_ref[...] = (x * lax.rsqrt(ms + RMS_EPS) * g_ref[...]).astype(o_ref.dtype)


def _rmsnorm(x, g, tm=512):
    n, d = x.shape
    return pl.pallas_call(
        _rmsnorm_kernel,
        grid=(n // tm,),
        in_specs=[pl.BlockSpec((tm, d), lambda i: (i, 0)),
                  pl.BlockSpec((1, d), lambda i: (0, 0))],
        out_specs=pl.BlockSpec((tm, d), lambda i: (i, 0)),
        out_shape=jax.ShapeDtypeStruct((n, d), BF16),
        compiler_params=_params("arbitrary"),
        name="rmsnorm",
    )(x, g.reshape(1, d))


def _in_proj_a_kernel(h_ref, wv_ref, wbg_ref, wcg_ref, wza_ref, wga_ref, wgb_ref, cw_ref,
                      a_ref, sga_ref, sgb_ref, carry_ref, w_ref, *, tiles_per_seq, sub):
    i = pl.program_id(1)

    @pl.when(i == 0)
    def _():
        for k, src in enumerate((wv_ref, wbg_ref, wcg_ref, wza_ref, wga_ref, wgb_ref)):
            w_ref[k] = src[...].astype(BF16)

    @pl.when(i % tiles_per_seq == 0)
    def _():
        carry_ref[...] = jnp.zeros_like(carry_ref)

    cw = cw_ref[...]
    rs = h_ref.shape[0] // sub
    row = lax.broadcasted_iota(jnp.int32, (rs, a_ref.shape[1]), 0)
    tail = carry_ref[...]
    for r in range(sub):
        rows = slice(r * rs, (r + 1) * rs)
        h = h_ref[rows, :]
        v = jnp.dot(h, w_ref[0], preferred_element_type=F32)
        bg = jnp.dot(h, w_ref[1], preferred_element_type=F32)
        cg = jnp.dot(h, w_ref[2], preferred_element_type=F32)
        za = jnp.dot(h, w_ref[3], preferred_element_type=F32)
        cv = cg * v
        prev1 = tail[7:8, :]
        prev2 = tail[6:7, :]
        cv1 = jnp.where(row == 0, prev1, pltpu.roll(cv, 1, axis=0))
        cv2 = jnp.where(row == 0, prev2, jnp.where(row == 1, prev1, pltpu.roll(cv, 2, axis=0)))
        conv = cw[0:1, :] * cv2 + cw[1:2, :] * cv1 + cw[2:3, :] * cv
        a_ref[rows, :] = (bg * conv * (za * _sigmoid(za))).astype(a_ref.dtype)
        tail = cv[rs - 8:rs, :]
        ga = jnp.dot(h, w_ref[4], preferred_element_type=F32)
        sga_ref[rows, :] = _sigmoid(ga).astype(sga_ref.dtype)
        gb = jnp.dot(h, w_ref[5], preferred_element_type=F32)
        sgb_ref[rows, :] = _sigmoid(gb).astype(sgb_ref.dtype)
    carry_ref[...] = tail


def _in_proj_a(h, w_in, conv_w, layer, seq_len, tm=1024, tn=256, sub=4):
    n, d = h.shape
    nj = CONV_WIDTH // tn
    offsets = (OFF_V, OFF_BG, OFF_CG, OFF_ZA, OFF_G, OFF_G + D_MODEL)

    def wspec(off):
        return pl.BlockSpec((None, d, tn), lambda j, i, o=off // tn: (layer, 0, o + j))

    out = pl.BlockSpec((tm, tn), lambda j, i: (i, j))
    shape = jax.ShapeDtypeStruct((n, CONV_WIDTH), BF16)
    return pl.pallas_call(
        functools.partial(_in_proj_a_kernel, tiles_per_seq=seq_len // tm, sub=sub),
        grid=(nj, n // tm),
        in_specs=[pl.BlockSpec((tm, d), lambda j, i: (i, 0))] + [wspec(o) for o in offsets]
                 + [pl.BlockSpec((None, CONV_K, tn), lambda j, i: (layer, 0, j))],
        out_specs=[out, out, out],
        out_shape=[shape, shape, shape],
        scratch_shapes=[pltpu.VMEM((8, tn), F32), pltpu.VMEM((len(offsets), d, tn), BF16)],
        compiler_params=_params("arbitrary", "arbitrary"),
        name="in_proj_conv_gates",
    )(h, *([w_in] * len(offsets)), conv_w)


def _in_proj_b_kernel(h_ref, wu_ref, wzb_ref, u_ref, szb_ref, w_ref, *, sub):
    @pl.when(pl.program_id(1) == 0)
    def _():
        w_ref[0] = wu_ref[...].astype(BF16)
        w_ref[1] = wzb_ref[...].astype(BF16)

    rs = h_ref.shape[0] // sub
    for r in range(sub):
        rows = slice(r * rs, (r + 1) * rs)
        h = h_ref[rows, :]
        u_ref[rows, :] = jnp.dot(h, w_ref[0], preferred_element_type=F32)
        zb = jnp.dot(h, w_ref[1], preferred_element_type=F32)
        szb_ref[rows, :] = (zb * _sigmoid(zb)).astype(szb_ref.dtype)


def _in_proj_b(h, w_in, layer, tm=1024, tn=512, sub=4):
    n, d = h.shape

    def wspec(off):
        return pl.BlockSpec((None, d, tn), lambda j, i, o=off // tn: (layer, 0, o + j))

    out = pl.BlockSpec((tm, tn), lambda j, i: (i, j))
    return pl.pallas_call(
        functools.partial(_in_proj_b_kernel, sub=sub),
        grid=(SSM_WIDTH // tn, n // tm),
        in_specs=[pl.BlockSpec((tm, d), lambda j, i: (i, 0)), wspec(OFF_U), wspec(OFF_ZB)],
        out_specs=[out, out],
        out_shape=[jax.ShapeDtypeStruct((n, SSM_WIDTH), F32), jax.ShapeDtypeStruct((n, SSM_WIDTH), BF16)],
        scratch_shapes=[pltpu.VMEM((2, d, tn), BF16)],
        compiler_params=_params("arbitrary", "arbitrary"),
        name="in_proj_ssm",
    )(h, w_in, w_in)


def _ssm_kernel(u_ref, m0_ref, win_ref, wout_ref, lre_ref, lim_ref, y_ref,
                t_ref, r_ref, sre_ref, sim_ref, hre_ref, him_ref, *, n_chunks, batch):
    gb = m0_ref.shape[0]
    t = SSM_CHUNK
    for b in range(batch):
        for s in range(t):
            xt = u_ref[b, pl.ds(s, n_chunks, stride=t), :].T
            for g in range(gb):
                row0 = g * CHUNK_LANES + s * SSM_GROUP
                t_ref[b, row0:row0 + SSM_GROUP, :] = xt[g * SSM_GROUP:(g + 1) * SSM_GROUP, :]
    for b in range(batch):
        for g in range(gb):
            a = t_ref[b, g * CHUNK_LANES:(g + 1) * CHUNK_LANES, :].T
            for hf in range(2):
                r_ref[g, hf, pl.ds(b, n_chunks, stride=batch), :] = a[:, hf * LANE_TILE:(hf + 1) * LANE_TILE]

    def u_of(g):
        return jnp.concatenate([r_ref[g, 0], r_ref[g, 1]], axis=1).astype(BF16)

    for g in range(gb):
        s = jnp.dot(u_of(g), win_ref[g], preferred_element_type=F32)
        sre_ref[:, g * STATE_LANES:(g + 1) * STATE_LANES] = s[:, :STATE_LANES]
        sim_ref[:, g * STATE_LANES:(g + 1) * STATE_LANES] = s[:, STATE_LANES:]
    lre = lre_ref[0]
    lim = lim_ref[0]
    rows = 2 * batch
    first = lax.broadcasted_iota(jnp.int32, (rows, gb * STATE_LANES), 0) < batch

    def advance(h_r, h_i, s_r, s_i):
        return lre * h_r - lim * h_i + s_r, lre * h_i + lim * h_r + s_i

    def step(j, carry):
        h_r, h_i = carry
        r0 = pl.multiple_of(j * rows, rows)
        s_r = sre_ref[pl.ds(r0, rows), :]
        s_i = sim_ref[pl.ds(r0, rows), :]
        n_r, n_i = advance(h_r, h_i, s_r, s_i)
        h_r = jnp.where(first, h_r, pltpu.roll(n_r, batch, axis=0))
        h_i = jnp.where(first, h_i, pltpu.roll(n_i, batch, axis=0))
        hre_ref[pl.ds(r0, rows), :] = h_r
        him_ref[pl.ds(r0, rows), :] = h_i
        n_r, n_i = advance(h_r, h_i, s_r, s_i)
        return pltpu.roll(n_r, batch, axis=0), pltpu.roll(n_i, batch, axis=0)

    zero = jnp.zeros((rows, gb * STATE_LANES), F32)
    lax.fori_loop(0, n_chunks // 2, step, (zero, zero))

    for g in range(gb):
        h = jnp.concatenate([hre_ref[:, g * STATE_LANES:(g + 1) * STATE_LANES],
                             him_ref[:, g * STATE_LANES:(g + 1) * STATE_LANES]], axis=1)
        y = jnp.dot(u_of(g), m0_ref[g], preferred_element_type=F32)
        y = y + jnp.dot(h.astype(BF16), wout_ref[g], preferred_element_type=F32)
        for hf in range(2):
            r_ref[g, hf] = y[:, hf * LANE_TILE:(hf + 1) * LANE_TILE]

    for b in range(batch):
        for g in range(gb):
            a = jnp.concatenate([r_ref[g, hf, pl.ds(b, n_chunks, stride=batch), :] for hf in range(2)],
                                axis=1)
            at = a.T
            for tt in range(t):
                row0 = tt * LANE_TILE + g * SSM_GROUP
                t_ref[b, row0:row0 + SSM_GROUP, :] = at[tt * SSM_GROUP:(tt + 1) * SSM_GROUP, :]
    for b in range(batch):
        for tt in range(t):
            y_ref[b, pl.ds(tt, n_chunks, stride=t), :] = t_ref[b, tt * LANE_TILE:(tt + 1) * LANE_TILE, :].T


def _ssm(u_p, m0, w_in_state, w_out_state, lam_re, lam_im, layer):
    batch, seq, width = u_p.shape
    n_inst = batch * seq // SSM_CHUNK
    gb = GROUPS_PER_STEP
    assert gb * SSM_GROUP == LANE_TILE and n_inst // batch == LANE_TILE
    first = layer * (width // LANE_TILE)
    blk = pl.BlockSpec((batch, seq, LANE_TILE), lambda i: (0, 0, i))
    sq = pl.BlockSpec((gb, CHUNK_LANES, CHUNK_LANES), lambda i: (first + i, 0, 0))
    vec = pl.BlockSpec((1, 1, gb * STATE_LANES), lambda i: (first + i, 0, 0))
    return pl.pallas_call(
        functools.partial(_ssm_kernel, n_chunks=n_inst // batch, batch=batch),
        grid=(width // LANE_TILE,),
        in_specs=[blk, sq, sq, sq, vec, vec],
        out_specs=blk,
        out_shape=jax.ShapeDtypeStruct(u_p.shape, F32),
        scratch_shapes=[pltpu.VMEM((batch, gb * CHUNK_LANES, LANE_TILE), F32),
                        pltpu.VMEM((gb, 2, n_inst, LANE_TILE), F32)]
                       + [pltpu.VMEM((n_inst, gb * STATE_LANES), F32) for _ in range(4)],
        compiler_params=_params("arbitrary"),
        name="s5_chunked_scan",
    )(u_p, m0, w_in_state, w_out_state, lam_re, lam_im)


def _split_bf16(x):
    hi = x.astype(BF16)
    return hi, (x - hi.astype(F32)).astype(BF16)


def _dot_nt_f32(a, b):
    dn = (((1,), (1,)), ((), ()))
    a_hi, a_lo = _split_bf16(a)
    b_hi, b_lo = _split_bf16(b)

    def d(x, y):
        return lax.dot_general(x, y, dn, preferred_element_type=F32)

    return d(a_hi, b_hi) + d(a_hi, b_lo) + d(a_lo, b_hi)


def _ssm_prep_kernel(are_ref, aim_ref, ldt_ref, btr_ref, bti_ref, cre_ref, cim_ref,
                     m0_ref, win_ref, wout_ref, lre_ref, lim_ref):
    gb = are_ref.shape[0]
    t = SSM_CHUNK
    tau = lax.broadcasted_iota(jnp.int32, (PREP_ROWS, STATE_LANES), 0).astype(F32)
    lane = lax.broadcasted_iota(jnp.int32, (SSM_GROUP, CHUNK_LANES), 1)
    for g in range(gb):
        are = are_ref[g]
        aim = aim_ref[g]
        dt = jnp.exp(ldt_ref[g])
        mag = jnp.exp(tau * (dt * are))
        ang = tau * (dt * aim)
        pw_re = mag * jnp.cos(ang)
        pw_im = mag * jnp.sin(ang)
        nr = pw_re[1:2] - 1.0
        ni = pw_im[1:2]
        den = are * are + aim * aim
        cr = (nr * are + ni * aim) / den
        ci = (ni * are - nr * aim) / den
        btr = btr_ref[g]
        bti = bti_ref[g]
        bb_re = cr * btr - ci * bti
        bb_im = cr * bti + ci * btr
        cre = cre_ref[g]
        cim = cim_ref[g]

        win_re, win_im = [], []
        for s in range(t):
            pr = pw_re[t - 1 - s:t - s]
            pi = pw_im[t - 1 - s:t - s]
            win_re.append(pr * bb_re - pi * bb_im)
            win_im.append(pr * bb_im + pi * bb_re)
        win_ref[g] = jnp.concatenate(
            [jnp.concatenate(win_re, axis=0), jnp.concatenate(win_im, axis=0)], axis=1).astype(BF16)

        z_re, z_im = [], []
        for k in range(t + 1):
            pr = pw_re[k:k + 1]
            pi = pw_im[k:k + 1]
            z_re.append(cre * pr - cim * pi)
            z_im.append(cre * pi + cim * pr)

        zo_re = jnp.concatenate(z_re[1:], axis=0)
        zo_im = jnp.concatenate(z_im[1:], axis=0)
        wout_ref[g] = jnp.concatenate([zo_re.T, -(zo_im.T)], axis=0).astype(BF16)

        zk_re = jnp.concatenate(z_re[:t], axis=0)
        zk_im = jnp.concatenate(z_im[:t], axis=0)
        kt = _dot_nt_f32(bb_re, zk_re) - _dot_nt_f32(bb_im, zk_im)
        blocks = [kt]
        for s in range(1, t):
            shifted = pltpu.roll(kt, s * SSM_GROUP, axis=1)
            blocks.append(jnp.where(lane >= s * SSM_GROUP, shifted, 0.0))
        m0_ref[g] = jnp.concatenate(blocks, axis=0).astype(BF16)

        lre_ref[0, :, g * STATE_LANES:(g + 1) * STATE_LANES] = pw_re[t:t + 1]
        lim_ref[0, :, g * STATE_LANES:(g + 1) * STATE_LANES] = pw_im[t:t + 1]


def _ssm_weights(a_re, a_im, log_dt, b_re, b_im, c_re, c_im):
    a_re, a_im, b_re, b_im, c_re, c_im = (
        x.reshape((-1,) + x.shape[2:]) for x in (a_re, a_im, b_re, b_im, c_re, c_im))
    log_dt = log_dt.reshape(-1)
    g, p = a_re.shape
    gb = GROUPS_PER_STEP
    pad = STATE_LANES - p

    def lanes(x, value=0.0):
        return jnp.pad(x.astype(F32), [(0, 0)] * (x.ndim - 1) + [(0, pad)], constant_values=value)

    are = lanes(a_re, -1.0).reshape(g, 1, STATE_LANES)
    aim = lanes(a_im).reshape(g, 1, STATE_LANES)
    ldt = jnp.broadcast_to(log_dt.astype(F32)[:, None, None], (g, 1, STATE_LANES))
    btr = lanes(jnp.swapaxes(b_re, 1, 2))
    bti = lanes(jnp.swapaxes(b_im, 1, 2))
    cre = lanes(c_re)
    cim = lanes(c_im)
    vec = pl.BlockSpec((gb, 1, STATE_LANES), lambda i: (i, 0, 0))
    mat = pl.BlockSpec((gb, SSM_GROUP, STATE_LANES), lambda i: (i, 0, 0))
    sq = pl.BlockSpec((gb, CHUNK_LANES, CHUNK_LANES), lambda i: (i, 0, 0))
    row = pl.BlockSpec((1, 1, gb * STATE_LANES), lambda i: (i, 0, 0))
    sq_shape = jax.ShapeDtypeStruct((g, CHUNK_LANES, CHUNK_LANES), BF16)
    row_shape = jax.ShapeDtypeStruct((g // gb, 1, gb * STATE_LANES), F32)
    return pl.pallas_call(
        _ssm_prep_kernel,
        grid=(g // gb,),
        in_specs=[vec, vec, vec, mat, mat, mat, mat],
        out_specs=[sq, sq, sq, row, row],
        out_shape=[sq_shape, sq_shape, sq_shape, row_shape, row_shape],
        compiler_params=_params("arbitrary"),
        name="s5_chunk_operators",
    )(are, aim, ldt, btr, bti, cre, cim)


STAGE_ROWS = 256


def _tail_kernel(y_ref, u_ref, szb_ref, a_ref, sga_ref, sgb_ref, d_ref, bgl_ref,
                 wg32_ref, wa32_ref, wb32_ref, m_ref, wg_ref, wa_ref, wb_ref, *, sub):
    step = pl.program_id(0)
    stages = ((wg32_ref, wg_ref), (wa32_ref, wa_ref), (wb32_ref, wb_ref))
    first = 0
    for src, dst in stages:
        n_chunks = dst.shape[0] // STAGE_ROWS

        @pl.when((step >= first) & (step < first + n_chunks))
        def _(src=src, dst=dst, first=first):
            r0 = pl.multiple_of((step - first) * STAGE_ROWS, STAGE_ROWS)
            dst[pl.ds(r0, STAGE_ROWS), :] = src[...].astype(BF16)

        first += n_chunks

    @pl.when(step >= first)
    def _():
        rs = y_ref.shape[0] // sub
        for r in range(sub):
            rows = slice(r * rs, (r + 1) * rs)
            yb = jax.nn.gelu(y_ref[rows, :] + d_ref[...] * u_ref[rows, :])
            z = jnp.dot(yb.astype(BF16), wg_ref[...], preferred_element_type=F32) + bgl_ref[...]
            b = (yb * _sigmoid(z) * szb_ref[rows, :].astype(F32)).astype(BF16)
            ya = jnp.dot(a_ref[rows, :], wa_ref[...], preferred_element_type=F32)
            yb2 = jnp.dot(b, wb_ref[...], preferred_element_type=F32)
            m_ref[rows, :] = (sga_ref[rows, :].astype(F32) * ya
                              + sgb_ref[rows, :].astype(F32) * yb2).astype(m_ref.dtype)


def _tail(y, u, szb, a_in, sga, sgb, d_skip, b_glu, w_glu, w_a, w_b, layer, tm=256, sub=1):
    n, w = y.shape
    d = a_in.shape[1]
    counts = [rows // STAGE_ROWS for rows in (w, d, w)]
    starts = [sum(counts[:k]) for k in range(len(counts))]
    w_steps = sum(counts)

    def tile(width):
        return pl.BlockSpec((tm, width), lambda s: (jnp.maximum(s - w_steps, 0), 0))

    def staged(k, width):
        return pl.BlockSpec((None, STAGE_ROWS, width),
                            lambda s: (layer, jnp.clip(s - starts[k], 0, counts[k] - 1), 0))

    vec = pl.BlockSpec((None, 1, w), lambda s: (layer, 0, 0))
    return pl.pallas_call(
        functools.partial(_tail_kernel, sub=sub),
        grid=(w_steps + n // tm,),
        in_specs=[tile(w), tile(w), tile(w), tile(d), tile(d), tile(d), vec, vec,
                  staged(0, w), staged(1, d), staged(2, d)],
        out_specs=tile(d),
        out_shape=jax.ShapeDtypeStruct((n, d), BF16),
        scratch_shapes=[pltpu.VMEM((w, w), BF16), pltpu.VMEM((d, d), BF16), pltpu.VMEM((w, d), BF16)],
        compiler_params=_params("arbitrary"),
        name="glu_gated_merge",
    )(y, u, szb, a_in, sga, sgb, d_skip.reshape(-1, 1, w), b_glu.reshape(-1, 1, w), w_glu, w_a, w_b)


def _out_kernel(m_ref, wo32_ref, x_ref, g_ref, *refs, emit_x, sub, w_steps):
    o_refs, wo_ref = refs[:-1], refs[-1]
    step = pl.program_id(0)
    wr = wo32_ref.shape[0]

    @pl.when(step < w_steps)
    def _():
        r0 = pl.multiple_of(step * wr, wr)
        wo_ref[pl.ds(r0, wr), :] = wo32_ref[...].astype(BF16)

    @pl.when(step >= w_steps)
    def _():
        rs = m_ref.shape[0] // sub
        for r in range(sub):
            rows = slice(r * rs, (r + 1) * rs)
            xn = x_ref[rows, :] + jnp.dot(m_ref[rows, :], wo_ref[...], preferred_element_type=F32)
            ms = jnp.mean(xn * xn, axis=-1, keepdims=True)
            normed = xn * lax.rsqrt(ms + RMS_EPS) * g_ref[...]
            if emit_x:
                o_refs[0][rows, :] = xn
                o_refs[1][rows, :] = normed.astype(o_refs[1].dtype)
            else:
                o_refs[0][rows, :] = normed.astype(o_refs[0].dtype)


def _out_proj(m, w_o, layer, x, g, emit_x, tm=512, sub=2, w_steps=4):
    n, d = x.shape
    row = pl.BlockSpec((tm, d), lambda s: (jnp.maximum(s - w_steps, 0), 0))
    if emit_x:
        out_specs = [row, row]
        out_shape = [jax.ShapeDtypeStruct((n, d), F32), jax.ShapeDtypeStruct((n, d), BF16)]
    else:
        out_specs = row
        out_shape = jax.ShapeDtypeStruct((n, d), F32)
    w_spec = pl.BlockSpec((None, d // w_steps, d), lambda s: (layer, jnp.minimum(s, w_steps - 1), 0))
    return pl.pallas_call(
        functools.partial(_out_kernel, emit_x=emit_x, sub=sub, w_steps=w_steps),
        grid=(w_steps + n // tm,),
        in_specs=[row, w_spec, row, pl.BlockSpec((1, d), lambda s: (0, 0))],
        out_specs=out_specs,
        out_shape=out_shape,
        scratch_shapes=[pltpu.VMEM((d, d), BF16)],
        compiler_params=_params("arbitrary"),
        name="out_proj_residual_norm",
    )(m, w_o, x, g.reshape(1, d))


def kernel(x, norm_g, w_in, conv_w, w_out_a, a_re, a_im, log_dt, b_re, b_im, c_re, c_im,
           d_skip, w_glu, b_glu, w_out_b, w_o, final_g):
    bsz, seq, d = x.shape
    depth = norm_g.shape[0]
    n = bsz * seq
    n_chunks = seq // SSM_CHUNK
    assert 2 * bsz == 8 and n_chunks % 2 == 0
    xf = x.reshape(n, d)
    h = _rmsnorm(xf, norm_g[0])
    ssm_ops = _ssm_weights(a_re, a_im, log_dt, b_re, b_im, c_re, c_im)
    for l in range(depth):
        a_in, sga, sgb = _in_proj_a(h, w_in, conv_w, l, seq)
        u, szb = _in_proj_b(h, w_in, l)
        y = _ssm(u.reshape(bsz, seq, SSM_WIDTH), *ssm_ops, l).reshape(n, SSM_WIDTH)
        m = _tail(y, u, szb, a_in, sga, sgb, d_skip, b_glu, w_glu, w_out_a, w_out_b, l)
        if l + 1 < depth:
            xf, h = _out_proj(m, w_o, l, xf, norm_g[l + 1], True)
        else:
            out = _out_proj(m, w_o, l, xf, final_g, False)
    return out.reshape(bsz, seq, d)
```

```python
import functools

import jax
import jax.numpy as jnp
from jax import lax
from jax.experimental import pallas as pl
from jax.experimental.pallas import tpu as pltpu

F32 = jnp.float32
BF16 = jnp.bfloat16

D_MODEL = 2048
CONV_WIDTH = D_MODEL
CONV_K = 3
SSM_WIDTH = D_MODEL // 2
SSM_GROUP = 16
SSM_GROUPS = SSM_WIDTH // SSM_GROUP
SSM_STATE = 64
RMS_EPS = 1e-6

OFF_V = 0
OFF_BG = CONV_WIDTH
OFF_CG = 2 * CONV_WIDTH
OFF_ZA = 3 * CONV_WIDTH
OFF_U = 4 * CONV_WIDTH
OFF_ZB = OFF_U + SSM_WIDTH
OFF_G = OFF_ZB + SSM_WIDTH

SSM_CHUNK = 16
CHUNK_LANES = SSM_CHUNK * SSM_GROUP
LANE_TILE = 128
STATE_LANES = LANE_TILE
GROUPS_PER_STEP = 8
PREP_ROWS = 24

V7X_VMEM_LIMIT = 56 * 1024 * 1024


def _sigmoid(x):
    return 0.5 * (jnp.tanh(0.5 * x) + 1.0)


def _params(*sem):
    return pltpu.CompilerParams(dimension_semantics=sem, vmem_limit_bytes=V7X_VMEM_LIMIT)


def _in_proj_a_kernel(h_ref, wv_ref, wbg_ref, wcg_ref, wza_ref, wga_ref, wgb_ref, cw_ref,
                      a_ref, sga_ref, sgb_ref, carry_ref, w_ref, *, tiles_per_seq, sub):
    i = pl.program_id(1)

    @pl.when(i == 0)
    def _():
        for k, src in enumerate((wv_ref, wbg_ref, wcg_ref, wza_ref, wga_ref, wgb_ref)):
            w_ref[k] = src[...].astype(BF16)

    @pl.when(i % tiles_per_seq == 0)
    def _():
        carry_ref[...] = jnp.zeros_like(carry_ref)

    cw = cw_ref[...]
    rs = h_ref.shape[0] // sub
    row = lax.broadcasted_iota(jnp.int32, (rs, a_ref.shape[1]), 0)
    tail = carry_ref[...]
    for r in range(sub):
        rows = slice(r * rs, (r + 1) * rs)
        h = h_ref[rows, :]
        v = jnp.dot(h, w_ref[0], preferred_element_type=F32)
        bg = jnp.dot(h, w_ref[1], preferred_element_type=F32)
        cg = jnp.dot(h, w_ref[2], preferred_element_type=F32)
        za = jnp.dot(h, w_ref[3], preferred_element_type=F32)
        cv = cg * v
        prev1 = tail[7:8, :]
        prev2 = tail[6:7, :]
        cv1 = jnp.where(row == 0, prev1, pltpu.roll(cv, 1, axis=0))
        cv2 = jnp.where(row == 0, prev2, jnp.where(row == 1, prev1, pltpu.roll(cv, 2, axis=0)))
        conv = cw[0:1, :] * cv2 + cw[1:2, :] * cv1 + cw[2:3, :] * cv
        a_ref[rows, :] = (bg * conv * (za * _sigmoid(za))).astype(a_ref.dtype)
        tail = cv[rs - 8:rs, :]
        ga = jnp.dot(h, w_ref[4], preferred_element_type=F32)
        sga_ref[rows, :] = _sigmoid(ga).astype(sga_ref.dtype)
        gb = jnp.dot(h, w_ref[5], preferred_element_type=F32)
        sgb_ref[rows, :] = _sigmoid(gb).astype(sgb_ref.dtype)
    carry_ref[...] = tail


def _in_proj_a(h, w_in, conv_w, layer, seq_len, tm=1024, tn=256, sub=4):
    n, d = h.shape
    nj = CONV_WIDTH // tn
    offsets = (OFF_V, OFF_BG, OFF_CG, OFF_ZA, OFF_G, OFF_G + D_MODEL)

    def wspec(off):
        return pl.BlockSpec((None, d, tn), lambda j, i, o=off // tn: (layer, 0, o + j))

    out = pl.BlockSpec((tm, tn), lambda j, i: (i, j))
    shape = jax.ShapeDtypeStruct((n, CONV_WIDTH), BF16)
    return pl.pallas_call(
        functools.partial(_in_proj_a_kernel, tiles_per_seq=seq_len // tm, sub=sub),
        grid=(nj, n // tm),
        in_specs=[pl.BlockSpec((tm, d), lambda j, i: (i, 0))] + [wspec(o) for o in offsets]
                 + [pl.BlockSpec((None, CONV_K, tn), lambda j, i: (layer, 0, j))],
        out_specs=[out, out, out],
        out_shape=[shape, shape, shape],
        scratch_shapes=[pltpu.VMEM((8, tn), F32), pltpu.VMEM((len(offsets), d, tn), BF16)],
        compiler_params=_params("arbitrary", "arbitrary"),
        name="in_proj_conv_gates",
    )(h, *([w_in] * len(offsets)), conv_w)


def _in_proj_b_kernel(*refs, sub, norm):
    if norm:
        x_ref, g_ref, wu_ref, wzb_ref, u_ref, szb_ref, h_ref, w_ref = refs
    else:
        x_ref, wu_ref, wzb_ref, u_ref, szb_ref, w_ref = refs

    @pl.when(pl.program_id(1) == 0)
    def _():
        w_ref[0] = wu_ref[...].astype(BF16)
        w_ref[1] = wzb_ref[...].astype(BF16)

    rs = x_ref.shape[0] // sub
    for r in range(sub):
        rows = slice(r * rs, (r + 1) * rs)
        h = x_ref[rows, :]
        if norm:
            ms = jnp.mean(h * h, axis=-1, keepdims=True)
            h = (h * lax.rsqrt(ms + RMS_EPS) * g_ref[...]).astype(BF16)
            h_ref[rows, :] = h
        u_ref[rows, :] = jnp.dot(h, w_ref[0], preferred_element_type=F32)
        zb = jnp.dot(h, w_ref[1], preferred_element_type=F32)
        szb_ref[rows, :] = (zb * _sigmoid(zb)).astype(szb_ref.dtype)


def _in_proj_b(x, w_in, layer, norm_g=None, tn=512):
    n, d = x.shape
    norm = norm_g is not None
    tm, sub = (512, 2) if norm else (2048, 8)

    def wspec(off):
        return pl.BlockSpec((None, d, tn), lambda j, i, o=off // tn: (layer, 0, o + j))

    out = pl.BlockSpec((tm, tn), lambda j, i: (i, j))
    row = pl.BlockSpec((tm, d), lambda j, i: (i, 0))
    in_specs = [row] + ([pl.BlockSpec((1, d), lambda j, i: (0, 0))] if norm else []) + [wspec(OFF_U), wspec(OFF_ZB)]
    out_specs = [out, out] + ([row] if norm else [])
    out_shape = [jax.ShapeDtypeStruct((n, SSM_WIDTH), F32), jax.ShapeDtypeStruct((n, SSM_WIDTH), BF16)]
    out_shape += [jax.ShapeDtypeStruct((n, d), BF16)] if norm else []
    args = (x,) + ((norm_g.reshape(1, d),) if norm else ()) + (w_in, w_in)
    return pl.pallas_call(
        functools.partial(_in_proj_b_kernel, sub=sub, norm=norm),
        grid=(SSM_WIDTH // tn, n // tm),
        in_specs=in_specs,
        out_specs=out_specs,
        out_shape=out_shape,
        scratch_shapes=[pltpu.VMEM((2, d, tn), BF16)],
        compiler_params=_params("arbitrary", "arbitrary"),
        name="in_proj_ssm_norm" if norm else "in_proj_ssm",
    )(*args)


def _ssm_kernel(u_ref, m0_ref, win_ref, wout_ref, lre_ref, lim_ref, y_ref,
                t_ref, r_ref, sre_ref, sim_ref, hre_ref, him_ref, *, n_chunks, batch):
    gb = m0_ref.shape[0]
    t = SSM_CHUNK
    for b in range(batch):
        for s in range(t):
            xt = u_ref[b, pl.ds(s, n_chunks, stride=t), :].T
            for g in range(gb):
                row0 = g * CHUNK_LANES + s * SSM_GROUP
                t_ref[b, row0:row0 + SSM_GROUP, :] = xt[g * SSM_GROUP:(g + 1) * SSM_GROUP, :]
    for b in range(batch):
        for g in range(gb):
            a = t_ref[b, g * CHUNK_LANES:(g + 1) * CHUNK_LANES, :].T
            for hf in range(2):
                r_ref[g, hf, pl.ds(b, n_chunks, stride=batch), :] = a[:, hf * LANE_TILE:(hf + 1) * LANE_TILE]

    def u_of(g):
        return jnp.concatenate([r_ref[g, 0], r_ref[g, 1]], axis=1).astype(BF16)

    for g in range(gb):
        s = jnp.dot(u_of(g), win_ref[g], preferred_element_type=F32)
        sre_ref[:, g * STATE_LANES:(g + 1) * STATE_LANES] = s[:, :STATE_LANES]
        sim_ref[:, g * STATE_LANES:(g + 1) * STATE_LANES] = s[:, STATE_LANES:]
    lre = lre_ref[0]
    lim = lim_ref[0]
    rows = 2 * batch
    first = lax.broadcasted_iota(jnp.int32, (rows, gb * STATE_LANES), 0) < batch

    def advance(h_r, h_i, s_r, s_i):
        return lre * h_r - lim * h_i + s_r, lre * h_i + lim * h_r + s_i

    def step(j, carry):
        h_r, h_i = carry
        r0 = pl.multiple_of(j * rows, rows)
        s_r = sre_ref[pl.ds(r0, rows), :]
        s_i = sim_ref[pl.ds(r0, rows), :]
        n_r, n_i = advance(h_r, h_i, s_r, s_i)
        h_r = jnp.where(first, h_r, pltpu.roll(n_r, batch, axis=0))
        h_i = jnp.where(first, h_i, pltpu.roll(n_i, batch, axis=0))
        hre_ref[pl.ds(r0, rows), :] = h_r
        him_ref[pl.ds(r0, rows), :] = h_i
        n_r, n_i = advance(h_r, h_i, s_r, s_i)
        return pltpu.roll(n_r, batch, axis=0), pltpu.roll(n_i, batch, axis=0)

    zero = jnp.zeros((rows, gb * STATE_LANES), F32)
    lax.fori_loop(0, n_chunks // 2, step, (zero, zero))

    for g in range(gb):
        h = jnp.concatenate([hre_ref[:, g * STATE_LANES:(g + 1) * STATE_LANES],
                             him_ref[:, g * STATE_LANES:(g + 1) * STATE_LANES]], axis=1)
        y = jnp.dot(u_of(g), m0_ref[g], preferred_element_type=F32)
        y = y + jnp.dot(h.astype(BF16), wout_ref[g], preferred_element_type=F32)
        for hf in range(2):
            r_ref[g, hf] = y[:, hf * LANE_TILE:(hf + 1) * LANE_TILE]

    for b in range(batch):
        for g in range(gb):
            a = jnp.concatenate([r_ref[g, hf, pl.ds(b, n_chunks, stride=batch), :] for hf in range(2)],
                                axis=1)
            at = a.T
            for tt in range(t):
                row0 = tt * LANE_TILE + g * SSM_GROUP
                t_ref[b, row0:row0 + SSM_GROUP, :] = at[tt * SSM_GROUP:(tt + 1) * SSM_GROUP, :]
    for b in range(batch):
        for tt in range(t):
            y_ref[b, pl.ds(tt, n_chunks, stride=t), :] = t_ref[b, tt * LANE_TILE:(tt + 1) * LANE_TILE, :].T


def _ssm(u_p, m0, w_in_state, w_out_state, lam_re, lam_im, layer):
    batch, seq, width = u_p.shape
    n_inst = batch * seq // SSM_CHUNK
    gb = GROUPS_PER_STEP
    assert gb * SSM_GROUP == LANE_TILE and n_inst // batch == LANE_TILE
    first = layer * (width // LANE_TILE)
    blk = pl.BlockSpec((batch, seq, LANE_TILE), lambda i: (0, 0, i))
    sq = pl.BlockSpec((gb, CHUNK_LANES, CHUNK_LANES), lambda i: (first + i, 0, 0))
    vec = pl.BlockSpec((1, 1, gb * STATE_LANES), lambda i: (first + i, 0, 0))
    return pl.pallas_call(
        functools.partial(_ssm_kernel, n_chunks=n_inst // batch, batch=batch),
        grid=(width // LANE_TILE,),
        in_specs=[blk, sq, sq, sq, vec, vec],
        out_specs=blk,
        out_shape=jax.ShapeDtypeStruct(u_p.shape, F32),
        scratch_shapes=[pltpu.VMEM((batch, gb * CHUNK_LANES, LANE_TILE), F32),
                        pltpu.VMEM((gb, 2, n_inst, LANE_TILE), F32)]
                       + [pltpu.VMEM((n_inst, gb * STATE_LANES), F32) for _ in range(4)],
        compiler_params=_params("arbitrary"),
        name="s5_chunked_scan",
    )(u_p, m0, w_in_state, w_out_state, lam_re, lam_im)


def _split_bf16(x):
    hi = x.astype(BF16)
    return hi, (x - hi.astype(F32)).astype(BF16)


def _dot_nt_f32(a, b):
    dn = (((1,), (1,)), ((), ()))
    a_hi, a_lo = _split_bf16(a)
    b_hi, b_lo = _split_bf16(b)

    def d(x, y):
        return lax.dot_general(x, y, dn, preferred_element_type=F32)

    return d(a_hi, b_hi) + d(a_hi, b_lo) + d(a_lo, b_hi)


def _ssm_prep_kernel(are_ref, aim_ref, ldt_ref, btr_ref, bti_ref, cre_ref, cim_ref,
                     m0_ref, win_ref, wout_ref, lre_ref, lim_ref):
    gb = are_ref.shape[0]
    t = SSM_CHUNK
    tau = lax.broadcasted_iota(jnp.int32, (PREP_ROWS, STATE_LANES), 0).astype(F32)
    lane = lax.broadcasted_iota(jnp.int32, (SSM_GROUP, CHUNK_LANES), 1)
    for g in range(gb):
        are = are_ref[g]
        aim = aim_ref[g]
        dt = jnp.exp(ldt_ref[g])
        mag = jnp.exp(tau * (dt * are))
        ang = tau * (dt * aim)
        pw_re = mag * jnp.cos(ang)
        pw_im = mag * jnp.sin(ang)
        nr = pw_re[1:2] - 1.0
        ni = pw_im[1:2]
        den = are * are + aim * aim
        cr = (nr * are + ni * aim) / den
        ci = (ni * are - nr * aim) / den
        btr = btr_ref[g]
        bti = bti_ref[g]
        bb_re = cr * btr - ci * bti
        bb_im = cr * bti + ci * btr
        cre = cre_ref[g]
        cim = cim_ref[g]

        win_re, win_im = [], []
        for s in range(t):
            pr = pw_re[t - 1 - s:t - s]
            pi = pw_im[t - 1 - s:t - s]
            win_re.append(pr * bb_re - pi * bb_im)
            win_im.append(pr * bb_im + pi * bb_re)
        win_ref[g] = jnp.concatenate(
            [jnp.concatenate(win_re, axis=0), jnp.concatenate(win_im, axis=0)], axis=1).astype(BF16)

        z_re, z_im = [], []
        for k in range(t + 1):
            pr = pw_re[k:k + 1]
            pi = pw_im[k:k + 1]
            z_re.append(cre * pr - cim * pi)
            z_im.append(cre * pi + cim * pr)

        zo_re = jnp.concatenate(z_re[1:], axis=0)
        zo_im = jnp.concatenate(z_im[1:], axis=0)
        wout_ref[g] = jnp.concatenate([zo_re.T, -(zo_im.T)], axis=0).astype(BF16)

        zk_re = jnp.concatenate(z_re[:t], axis=0)
        zk_im = jnp.concatenate(z_im[:t], axis=0)
        kt = _dot_nt_f32(bb_re, zk_re) - _dot_nt_f32(bb_im, zk_im)
        blocks = [kt]
        for s in range(1, t):
            shifted = pltpu.roll(kt, s * SSM_GROUP, axis=1)
            blocks.append(jnp.where(lane >= s * SSM_GROUP, shifted, 0.0))
        m0_ref[g] = jnp.concatenate(blocks, axis=0).astype(BF16)

        lre_ref[0, :, g * STATE_LANES:(g + 1) * STATE_LANES] = pw_re[t:t + 1]
        lim_ref[0, :, g * STATE_LANES:(g + 1) * STATE_LANES] = pw_im[t:t + 1]


def _ssm_weights(a_re, a_im, log_dt, b_re, b_im, c_re, c_im):
    a_re, a_im, b_re, b_im, c_re, c_im = (
        x.reshape((-1,) + x.shape[2:]) for x in (a_re, a_im, b_re, b_im, c_re, c_im))
    log_dt = log_dt.reshape(-1)
    g, p = a_re.shape
    gb = GROUPS_PER_STEP
    pad = STATE_LANES - p

    def lanes(x, value=0.0):
        return jnp.pad(x.astype(F32), [(0, 0)] * (x.ndim - 1) + [(0, pad)], constant_values=value)

    are = lanes(a_re, -1.0).reshape(g, 1, STATE_LANES)
    aim = lanes(a_im).reshape(g, 1, STATE_LANES)
    ldt = jnp.broadcast_to(log_dt.astype(F32)[:, None, None], (g, 1, STATE_LANES))
    btr = lanes(jnp.swapaxes(b_re, 1, 2))
    bti = lanes(jnp.swapaxes(b_im, 1, 2))
    cre = lanes(c_re)
    cim = lanes(c_im)
    vec = pl.BlockSpec((gb, 1, STATE_LANES), lambda i: (i, 0, 0))
    mat = pl.BlockSpec((gb, SSM_GROUP, STATE_LANES), lambda i: (i, 0, 0))
    sq = pl.BlockSpec((gb, CHUNK_LANES, CHUNK_LANES), lambda i: (i, 0, 0))
    row = pl.BlockSpec((1, 1, gb * STATE_LANES), lambda i: (i, 0, 0))
    sq_shape = jax.ShapeDtypeStruct((g, CHUNK_LANES, CHUNK_LANES), BF16)
    row_shape = jax.ShapeDtypeStruct((g // gb, 1, gb * STATE_LANES), F32)
    return pl.pallas_call(
        _ssm_prep_kernel,
        grid=(g // gb,),
        in_specs=[vec, vec, vec, mat, mat, mat, mat],
        out_specs=[sq, sq, sq, row, row],
        out_shape=[sq_shape, sq_shape, sq_shape, row_shape, row_shape],
        compiler_params=_params("arbitrary"),
        name="s5_chunk_operators",
    )(are, aim, ldt, btr, bti, cre, cim)


STAGE_ROWS = 256


def _tail_kernel(y_ref, u_ref, szb_ref, a_ref, sga_ref, sgb_ref, d_ref, bgl_ref,
                 wg32_ref, wa32_ref, wb32_ref, m_ref, wg_ref, wa_ref, wb_ref, *, sub):
    step = pl.program_id(0)
    stages = ((wg32_ref, wg_ref), (wa32_ref, wa_ref), (wb32_ref, wb_ref))
    first = 0
    for src, dst in stages:
        n_chunks = dst.shape[0] // STAGE_ROWS

        @pl.when((step >= first) & (step < first + n_chunks))
        def _(src=src, dst=dst, first=first):
            r0 = pl.multiple_of((step - first) * STAGE_ROWS, STAGE_ROWS)
            dst[pl.ds(r0, STAGE_ROWS), :] = src[...].astype(BF16)

        first += n_chunks

    @pl.when(step >= first)
    def _():
        rs = y_ref.shape[0] // sub
        for r in range(sub):
            rows = slice(r * rs, (r + 1) * rs)
            yb = jax.nn.gelu(y_ref[rows, :] + d_ref[...] * u_ref[rows, :])
            z = jnp.dot(yb.astype(BF16), wg_ref[...], preferred_element_type=F32) + bgl_ref[...]
            b = (yb * _sigmoid(z) * szb_ref[rows, :].astype(F32)).astype(BF16)
            ya = jnp.dot(a_ref[rows, :], wa_ref[...], preferred_element_type=F32)
            yb2 = jnp.dot(b, wb_ref[...], preferred_element_type=F32)
            m_ref[rows, :] = (sga_ref[rows, :].astype(F32) * ya
                              + sgb_ref[rows, :].astype(F32) * yb2).astype(m_ref.dtype)


def _tail(y, u, szb, a_in, sga, sgb, d_skip, b_glu, w_glu, w_a, w_b, layer, tm=256, sub=1):
    n, w = y.shape
    d = a_in.shape[1]
    counts = [rows // STAGE_ROWS for rows in (w, d, w)]
    starts = [sum(counts[:k]) for k in range(len(counts))]
    w_steps = sum(counts)

    def tile(width):
        return pl.BlockSpec((tm, width), lambda s: (jnp.maximum(s - w_steps, 0), 0))

    def staged(k, width):
        return pl.BlockSpec((None, STAGE_ROWS, width),
                            lambda s: (layer, jnp.clip(s - starts[k], 0, counts[k] - 1), 0))

    vec = pl.BlockSpec((None, 1, w), lambda s: (layer, 0, 0))
    return pl.pallas_call(
        functools.partial(_tail_kernel, sub=sub),
        grid=(w_steps + n // tm,),
        in_specs=[tile(w), tile(w), tile(w), tile(d), tile(d), tile(d), vec, vec,
                  staged(0, w), staged(1, d), staged(2, d)],
        out_specs=tile(d),
        out_shape=jax.ShapeDtypeStruct((n, d), BF16),
        scratch_shapes=[pltpu.VMEM((w, w), BF16), pltpu.VMEM((d, d), BF16), pltpu.VMEM((w, d), BF16)],
        compiler_params=_params("arbitrary"),
        name="glu_gated_merge",
    )(y, u, szb, a_in, sga, sgb, d_skip.reshape(-1, 1, w), b_glu.reshape(-1, 1, w), w_glu, w_a, w_b)


def _out_kernel(m_ref, wo32_ref, x_ref, g_ref, *refs, emit_x, sub, w_steps):
    o_refs, wo_ref = refs[:-1], refs[-1]
    step = pl.program_id(0)
    wr = wo32_ref.shape[0]

    @pl.when(step < w_steps)
    def _():
        r0 = pl.multiple_of(step * wr, wr)
        wo_ref[pl.ds(r0, wr), :] = wo32_ref[...].astype(BF16)

    @pl.when(step >= w_steps)
    def _():
        rs = m_ref.shape[0] // sub
        for r in range(sub):
            rows = slice(r * rs, (r + 1) * rs)
            xn = x_ref[rows, :] + jnp.dot(m_ref[rows, :], wo_ref[...], preferred_element_type=F32)
            ms = jnp.mean(xn * xn, axis=-1, keepdims=True)
            normed = xn * lax.rsqrt(ms + RMS_EPS) * g_ref[...]
            if emit_x:
                o_refs[0][rows, :] = xn
                o_refs[1][rows, :] = normed.astype(o_refs[1].dtype)
            else:
                o_refs[0][rows, :] = normed.astype(o_refs[0].dtype)


def _out_proj(m, w_o, layer, x, g, emit_x, tm=512, sub=2, w_steps=4):
    n, d = x.shape
    row = pl.BlockSpec((tm, d), lambda s: (jnp.maximum(s - w_steps, 0), 0))
    if emit_x:
        out_specs = [row, row]
        out_shape = [jax.ShapeDtypeStruct((n, d), F32), jax.ShapeDtypeStruct((n, d), BF16)]
    else:
        out_specs = row
        out_shape = jax.ShapeDtypeStruct((n, d), F32)
    w_spec = pl.BlockSpec((None, d // w_steps, d), lambda s: (layer, jnp.minimum(s, w_steps - 1), 0))
    return pl.pallas_call(
        functools.partial(_out_kernel, emit_x=emit_x, sub=sub, w_steps=w_steps),
        grid=(w_steps + n // tm,),
        in_specs=[row, w_spec, row, pl.BlockSpec((1, d), lambda s: (0, 0))],
        out_specs=out_specs,
        out_shape=out_shape,
        scratch_shapes=[pltpu.VMEM((d, d), BF16)],
        compiler_params=_params("arbitrary"),
        name="out_proj_residual_norm",
    )(m, w_o, x, g.reshape(1, d))


def kernel(x, norm_g, w_in, conv_w, w_out_a, a_re, a_im, log_dt, b_re, b_im, c_re, c_im,
           d_skip, w_glu, b_glu, w_out_b, w_o, final_g):
    bsz, seq, d = x.shape
    depth = norm_g.shape[0]
    n = bsz * seq
    n_chunks = seq // SSM_CHUNK
    assert 2 * bsz == 8 and n_chunks % 2 == 0
    xf = x.reshape(n, d)
    ssm_ops = _ssm_weights(a_re, a_im, log_dt, b_re, b_im, c_re, c_im)
    for l in range(depth):
        if l == 0:
            u, szb, h = _in_proj_b(xf, w_in, l, norm_g[0])
        else:
            u, szb = _in_proj_b(h, w_in, l)
        a_in, sga, sgb = _in_proj_a(h, w_in, conv_w, l, seq)
        y = _ssm(u.reshape(bsz, seq, SSM_WIDTH), *ssm_ops, l).reshape(n, SSM_WIDTH)
        m = _tail(y, u, szb, a_in, sga, sgb, d_skip, b_glu, w_glu, w_out_a, w_out_b, l)
        if l + 1 < depth:
            xf, h = _out_proj(m, w_o, l, xf, norm_g[l + 1], True)
        else:
            out = _out_proj(m, w_o, l, xf, final_g, False)
    return out.reshape(bsz, seq, d)
```

```python
import functools

import jax
import jax.numpy as jnp
from jax import lax
from jax.experimental import pallas as pl
from jax.experimental.pallas import tpu as pltpu

F32 = jnp.float32
BF16 = jnp.bfloat16

D_MODEL = 2048
CONV_WIDTH = D_MODEL
CONV_K = 3
SSM_WIDTH = D_MODEL // 2
SSM_GROUP = 16
SSM_GROUPS = SSM_WIDTH // SSM_GROUP
SSM_STATE = 64
RMS_EPS = 1e-6

OFF_V = 0
OFF_BG = CONV_WIDTH
OFF_CG = 2 * CONV_WIDTH
OFF_ZA = 3 * CONV_WIDTH
OFF_U = 4 * CONV_WIDTH
OFF_ZB = OFF_U + SSM_WIDTH
OFF_G = OFF_ZB + SSM_WIDTH

SSM_CHUNK = 16
CHUNK_LANES = SSM_CHUNK * SSM_GROUP
LANE_TILE = 128
STATE_LANES = LANE_TILE
GROUPS_PER_STEP = 8
PREP_ROWS = 24

V7X_VMEM_LIMIT = 56 * 1024 * 1024


def _sigmoid(x):
    return 0.5 * (jnp.tanh(0.5 * x) + 1.0)


def _params(*sem):
    return pltpu.CompilerParams(dimension_semantics=sem, vmem_limit_bytes=V7X_VMEM_LIMIT)


def _rmsnorm_kernel(x_ref, g_ref, o_ref):
    x = x_ref[...]
    ms = jnp.mean(x * x, axis=-1, keepdims=True)
    o_ref[...] = (x * lax.rsqrt(ms + RMS_EPS) * g_ref[...]).astype(o_ref.dtype)


def _rmsnorm(x, g, tm=512):
    n, d = x.shape
    return pl.pallas_call(
        _rmsnorm_kernel,
        grid=(n // tm,),
        in_specs=[pl.BlockSpec((tm, d), lambda i: (i, 0)),
                  pl.BlockSpec((1, d), lambda i: (0, 0))],
        out_specs=pl.BlockSpec((tm, d), lambda i: (i, 0)),
        out_shape=jax.ShapeDtypeStruct((n, d), BF16),
        compiler_params=_params("arbitrary"),
        name="rmsnorm",
    )(x, g.reshape(1, d))


def _in_proj_a_kernel(h_ref, wv_ref, wbg_ref, wcg_ref, wza_ref, wga_ref, wgb_ref, cw_ref,
                      a_ref, sga_ref, sgb_ref, carry_ref, w_ref, *, tiles_per_seq, sub):
    i = pl.program_id(1)

    @pl.when(i == 0)
    def _():
        for k, src in enumerate((wv_ref, wbg_ref, wcg_ref, wza_ref, wga_ref, wgb_ref)):
            w_ref[k] = src[...].astype(BF16)

    @pl.when(i % tiles_per_seq == 0)
    def _():
        carry_ref[...] = jnp.zeros_like(carry_ref)

    cw = cw_ref[...]
    rs = h_ref.shape[0] // sub
    row = lax.broadcasted_iota(jnp.int32, (rs, a_ref.shape[1]), 0)
    tail = carry_ref[...]
    for r in range(sub):
        rows = slice(r * rs, (r + 1) * rs)
        h = h_ref[rows, :]
        v = jnp.dot(h, w_ref[0], preferred_element_type=F32)
        bg = jnp.dot(h, w_ref[1], preferred_element_type=F32)
        cg = jnp.dot(h, w_ref[2], preferred_element_type=F32)
        za = jnp.dot(h, w_ref[3], preferred_element_type=F32)
        cv = cg * v
        prev1 = tail[7:8, :]
        prev2 = tail[6:7, :]
        cv1 = jnp.where(row == 0, prev1, pltpu.roll(cv, 1, axis=0))
        cv2 = jnp.where(row == 0, prev2, jnp.where(row == 1, prev1, pltpu.roll(cv, 2, axis=0)))
        conv = cw[0:1, :] * cv2 + cw[1:2, :] * cv1 + cw[2:3, :] * cv
        a_ref[rows, :] = (bg * conv * (za * _sigmoid(za))).astype(a_ref.dtype)
        tail = cv[rs - 8:rs, :]
        ga = jnp.dot(h, w_ref[4], preferred_element_type=F32)
        sga_ref[rows, :] = _sigmoid(ga).astype(sga_ref.dtype)
        gb = jnp.dot(h, w_ref[5], preferred_element_type=F32)
        sgb_ref[rows, :] = _sigmoid(gb).astype(sgb_ref.dtype)
    carry_ref[...] = tail


def _in_proj_a(h, w_in, conv_w, layer, seq_len, tm=1024, tn=256, sub=4):
    n, d = h.shape
    nj = CONV_WIDTH // tn
    offsets = (OFF_V, OFF_BG, OFF_CG, OFF_ZA, OFF_G, OFF_G + D_MODEL)

    def wspec(off):
        return pl.BlockSpec((None, d, tn), lambda j, i, o=off // tn: (layer, 0, o + j))

    out = pl.BlockSpec((tm, tn), lambda j, i: (i, j))
    shape = jax.ShapeDtypeStruct((n, CONV_WIDTH), BF16)
    return pl.pallas_call(
        functools.partial(_in_proj_a_kernel, tiles_per_seq=seq_len // tm, sub=sub),
        grid=(nj, n // tm),
        in_specs=[pl.BlockSpec((tm, d), lambda j, i: (i, 0))] + [wspec(o) for o in offsets]
                 + [pl.BlockSpec((None, CONV_K, tn), lambda j, i: (layer, 0, j))],
        out_specs=[out, out, out],
        out_shape=[shape, shape, shape],
        scratch_shapes=[pltpu.VMEM((8, tn), F32), pltpu.VMEM((len(offsets), d, tn), BF16)],
        compiler_params=_params("arbitrary", "arbitrary"),
        name="in_proj_conv_gates",
    )(h, *([w_in] * len(offsets)), conv_w)


def _in_proj_b_kernel(h_ref, wu_ref, wzb_ref, u_ref, szb_ref, w_ref, *, sub):
    @pl.when(pl.program_id(1) == 0)
    def _():
        w_ref[0] = wu_ref[...].astype(BF16)
        w_ref[1] = wzb_ref[...].astype(BF16)

    rs = h_ref.shape[0] // sub
    for r in range(sub):
        rows = slice(r * rs, (r + 1) * rs)
        h = h_ref[rows, :]
        u_ref[rows, :] = jnp.dot(h, w_ref[0], preferred_element_type=F32)
        zb = jnp.dot(h, w_ref[1], preferred_element_type=F32)
        szb_ref[rows, :] = (zb * _sigmoid(zb)).astype(szb_ref.dtype)


def _in_proj_b(h, w_in, layer, tm=1024, tn=512, sub=4):
    n, d = h.shape

    def wspec(off):
        return pl.BlockSpec((None, d, tn), lambda j, i, o=off // tn: (layer, 0, o + j))

    out = pl.BlockSpec((tm, tn), lambda j, i: (i, j))
    return pl.pallas_call(
        functools.partial(_in_proj_b_kernel, sub=sub),
        grid=(SSM_WIDTH // tn, n // tm),
        in_specs=[pl.BlockSpec((tm, d), lambda j, i: (i, 0)), wspec(OFF_U), wspec(OFF_ZB)],
        out_specs=[out, out],
        out_shape=[jax.ShapeDtypeStruct((n, SSM_WIDTH), F32), jax.ShapeDtypeStruct((n, SSM_WIDTH), BF16)],
        scratch_shapes=[pltpu.VMEM((2, d, tn), BF16)],
        compiler_params=_params("arbitrary", "arbitrary"),
        name="in_proj_ssm",
    )(h, w_in, w_in)


def _ssm_kernel(u_ref, m0_ref, win_ref, wout_ref, lre_ref, lim_ref, d_ref, y_ref,
                t_ref, r_ref, sre_ref, sim_ref, hre_ref, him_ref, *, n_chunks, batch):
    gb = m0_ref.shape[0]
    t = SSM_CHUNK
    for b in range(batch):
        for s in range(t):
            xt = u_ref[b, pl.ds(s, n_chunks, stride=t), :].T
            for g in range(gb):
                row0 = g * CHUNK_LANES + s * SSM_GROUP
                t_ref[b, row0:row0 + SSM_GROUP, :] = xt[g * SSM_GROUP:(g + 1) * SSM_GROUP, :]
    for b in range(batch):
        for g in range(gb):
            a = t_ref[b, g * CHUNK_LANES:(g + 1) * CHUNK_LANES, :].T
            for hf in range(2):
                r_ref[g, hf, pl.ds(b, n_chunks, stride=batch), :] = a[:, hf * LANE_TILE:(hf + 1) * LANE_TILE]

    def u_of(g):
        return jnp.concatenate([r_ref[g, 0], r_ref[g, 1]], axis=1).astype(BF16)

    for g in range(gb):
        s = jnp.dot(u_of(g), win_ref[g], preferred_element_type=F32)
        sre_ref[:, g * STATE_LANES:(g + 1) * STATE_LANES] = s[:, :STATE_LANES]
        sim_ref[:, g * STATE_LANES:(g + 1) * STATE_LANES] = s[:, STATE_LANES:]
    lre = lre_ref[0]
    lim = lim_ref[0]
    rows = 2 * batch
    first = lax.broadcasted_iota(jnp.int32, (rows, gb * STATE_LANES), 0) < batch

    def advance(h_r, h_i, s_r, s_i):
        return lre * h_r - lim * h_i + s_r, lre * h_i + lim * h_r + s_i

    def step(j, carry):
        h_r, h_i = carry
        r0 = pl.multiple_of(j * rows, rows)
        s_r = sre_ref[pl.ds(r0, rows), :]
        s_i = sim_ref[pl.ds(r0, rows), :]
        n_r, n_i = advance(h_r, h_i, s_r, s_i)
        h_r = jnp.where(first, h_r, pltpu.roll(n_r, batch, axis=0))
        h_i = jnp.where(first, h_i, pltpu.roll(n_i, batch, axis=0))
        hre_ref[pl.ds(r0, rows), :] = h_r
        him_ref[pl.ds(r0, rows), :] = h_i
        n_r, n_i = advance(h_r, h_i, s_r, s_i)
        return pltpu.roll(n_r, batch, axis=0), pltpu.roll(n_i, batch, axis=0)

    zero = jnp.zeros((rows, gb * STATE_LANES), F32)
    lax.fori_loop(0, n_chunks // 2, step, (zero, zero))

    for g in range(gb):
        h = jnp.concatenate([hre_ref[:, g * STATE_LANES:(g + 1) * STATE_LANES],
                             him_ref[:, g * STATE_LANES:(g + 1) * STATE_LANES]], axis=1)
        y = jnp.dot(u_of(g), m0_ref[g], preferred_element_type=F32)
        y = y + jnp.dot(h.astype(BF16), wout_ref[g], preferred_element_type=F32)
        for hf in range(2):
            r_ref[g, hf] = y[:, hf * LANE_TILE:(hf + 1) * LANE_TILE]

    for b in range(batch):
        for g in range(gb):
            a = jnp.concatenate([r_ref[g, hf, pl.ds(b, n_chunks, stride=batch), :] for hf in range(2)],
                                axis=1)
            at = a.T
            for tt in range(t):
                row0 = tt * LANE_TILE + g * SSM_GROUP
                t_ref[b, row0:row0 + SSM_GROUP, :] = at[tt * SSM_GROUP:(tt + 1) * SSM_GROUP, :]
    for b in range(batch):
        for tt in range(t):
            y_ref[b, pl.ds(tt, n_chunks, stride=t), :] = t_ref[b, tt * LANE_TILE:(tt + 1) * LANE_TILE, :].T
        y_ref[b] = y_ref[b] + d_ref[...] * u_ref[b]


def _ssm(u_p, m0, w_in_state, w_out_state, lam_re, lam_im, d_skip, layer):
    batch, seq, width = u_p.shape
    n_inst = batch * seq // SSM_CHUNK
    gb = GROUPS_PER_STEP
    assert gb * SSM_GROUP == LANE_TILE and n_inst // batch == LANE_TILE
    first = layer * (width // LANE_TILE)
    blk = pl.BlockSpec((batch, seq, LANE_TILE), lambda i: (0, 0, i))
    sq = pl.BlockSpec((gb, CHUNK_LANES, CHUNK_LANES), lambda i: (first + i, 0, 0))
    vec = pl.BlockSpec((1, 1, gb * STATE_LANES), lambda i: (first + i, 0, 0))
    return pl.pallas_call(
        functools.partial(_ssm_kernel, n_chunks=n_inst // batch, batch=batch),
        grid=(width // LANE_TILE,),
        in_specs=[blk, sq, sq, sq, vec, vec, pl.BlockSpec((None, 1, LANE_TILE), lambda i: (layer, 0, i))],
        out_specs=blk,
        out_shape=jax.ShapeDtypeStruct(u_p.shape, F32),
        scratch_shapes=[pltpu.VMEM((batch, gb * CHUNK_LANES, LANE_TILE), F32),
                        pltpu.VMEM((gb, 2, n_inst, LANE_TILE), F32)]
                       + [pltpu.VMEM((n_inst, gb * STATE_LANES), F32) for _ in range(4)],
        compiler_params=_params("arbitrary"),
        name="s5_chunked_scan",
    )(u_p, m0, w_in_state, w_out_state, lam_re, lam_im, d_skip.reshape(-1, 1, width))


def _split_bf16(x):
    hi = x.astype(BF16)
    return hi, (x - hi.astype(F32)).astype(BF16)


def _dot_nt_f32(a, b):
    dn = (((1,), (1,)), ((), ()))
    a_hi, a_lo = _split_bf16(a)
    b_hi, b_lo = _split_bf16(b)

    def d(x, y):
        return lax.dot_general(x, y, dn, preferred_element_type=F32)

    return d(a_hi, b_hi) + d(a_hi, b_lo) + d(a_lo, b_hi)


def _ssm_prep_kernel(are_ref, aim_ref, ldt_ref, btr_ref, bti_ref, cre_ref, cim_ref,
                     m0_ref, win_ref, wout_ref, lre_ref, lim_ref):
    gb = are_ref.shape[0]
    t = SSM_CHUNK
    tau = lax.broadcasted_iota(jnp.int32, (PREP_ROWS, STATE_LANES), 0).astype(F32)
    lane = lax.broadcasted_iota(jnp.int32, (SSM_GROUP, CHUNK_LANES), 1)
    for g in range(gb):
        are = are_ref[g]
        aim = aim_ref[g]
        dt = jnp.exp(ldt_ref[g])
        mag = jnp.exp(tau * (dt * are))
        ang = tau * (dt * aim)
        pw_re = mag * jnp.cos(ang)
        pw_im = mag * jnp.sin(ang)
        nr = pw_re[1:2] - 1.0
        ni = pw_im[1:2]
        den = are * are + aim * aim
        cr = (nr * are + ni * aim) / den
        ci = (ni * are - nr * aim) / den
        btr = btr_ref[g]
        bti = bti_ref[g]
        bb_re = cr * btr - ci * bti
        bb_im = cr * bti + ci * btr
        cre = cre_ref[g]
        cim = cim_ref[g]

        win_re, win_im = [], []
        for s in range(t):
            pr = pw_re[t - 1 - s:t - s]
            pi = pw_im[t - 1 - s:t - s]
            win_re.append(pr * bb_re - pi * bb_im)
            win_im.append(pr * bb_im + pi * bb_re)
        win_ref[g] = jnp.concatenate(
            [jnp.concatenate(win_re, axis=0), jnp.concatenate(win_im, axis=0)], axis=1).astype(BF16)

        z_re, z_im = [], []
        for k in range(t + 1):
            pr = pw_re[k:k + 1]
            pi = pw_im[k:k + 1]
            z_re.append(cre * pr - cim * pi)
            z_im.append(cre * pi + cim * pr)

        zo_re = jnp.concatenate(z_re[1:], axis=0)
        zo_im = jnp.concatenate(z_im[1:], axis=0)
        wout_ref[g] = jnp.concatenate([zo_re.T, -(zo_im.T)], axis=0).astype(BF16)

        zk_re = jnp.concatenate(z_re[:t], axis=0)
        zk_im = jnp.concatenate(z_im[:t], axis=0)
        kt = _dot_nt_f32(bb_re, zk_re) - _dot_nt_f32(bb_im, zk_im)
        blocks = [kt]
        for s in range(1, t):
            shifted = pltpu.roll(kt, s * SSM_GROUP, axis=1)
            blocks.append(jnp.where(lane >= s * SSM_GROUP, shifted, 0.0))
        m0_ref[g] = jnp.concatenate(blocks, axis=0).astype(BF16)

        lre_ref[0, :, g * STATE_LANES:(g + 1) * STATE_LANES] = pw_re[t:t + 1]
        lim_ref[0, :, g * STATE_LANES:(g + 1) * STATE_LANES] = pw_im[t:t + 1]


def _ssm_weights(a_re, a_im, log_dt, b_re, b_im, c_re, c_im):
    a_re, a_im, b_re, b_im, c_re, c_im = (
        x.reshape((-1,) + x.shape[2:]) for x in (a_re, a_im, b_re, b_im, c_re, c_im))
    log_dt = log_dt.reshape(-1)
    g, p = a_re.shape
    gb = GROUPS_PER_STEP
    pad = STATE_LANES - p

    def lanes(x, value=0.0):
        return jnp.pad(x.astype(F32), [(0, 0)] * (x.ndim - 1) + [(0, pad)], constant_values=value)

    are = lanes(a_re, -1.0).reshape(g, 1, STATE_LANES)
    aim = lanes(a_im).reshape(g, 1, STATE_LANES)
    ldt = jnp.broadcast_to(log_dt.astype(F32)[:, None, None], (g, 1, STATE_LANES))
    btr = lanes(jnp.swapaxes(b_re, 1, 2))
    bti = lanes(jnp.swapaxes(b_im, 1, 2))
    cre = lanes(c_re)
    cim = lanes(c_im)
    vec = pl.BlockSpec((gb, 1, STATE_LANES), lambda i: (i, 0, 0))
    mat = pl.BlockSpec((gb, SSM_GROUP, STATE_LANES), lambda i: (i, 0, 0))
    sq = pl.BlockSpec((gb, CHUNK_LANES, CHUNK_LANES), lambda i: (i, 0, 0))
    row = pl.BlockSpec((1, 1, gb * STATE_LANES), lambda i: (i, 0, 0))
    sq_shape = jax.ShapeDtypeStruct((g, CHUNK_LANES, CHUNK_LANES), BF16)
    row_shape = jax.ShapeDtypeStruct((g // gb, 1, gb * STATE_LANES), F32)
    return pl.pallas_call(
        _ssm_prep_kernel,
        grid=(g // gb,),
        in_specs=[vec, vec, vec, mat, mat, mat, mat],
        out_specs=[sq, sq, sq, row, row],
        out_shape=[sq_shape, sq_shape, sq_shape, row_shape, row_shape],
        compiler_params=_params("arbitrary"),
        name="s5_chunk_operators",
    )(are, aim, ldt, btr, bti, cre, cim)


STAGE_ROWS = 256


def _tail_kernel(y_ref, szb_ref, a_ref, sga_ref, sgb_ref, bgl_ref,
                 wg32_ref, wa32_ref, wb32_ref, m_ref, wg_ref, wa_ref, wb_ref, *, sub):
    step = pl.program_id(0)
    stages = ((wg32_ref, wg_ref), (wa32_ref, wa_ref), (wb32_ref, wb_ref))
    first = 0
    for src, dst in stages:
        n_chunks = dst.shape[0] // STAGE_ROWS

        @pl.when((step >= first) & (step < first + n_chunks))
        def _(src=src, dst=dst, first=first):
            r0 = pl.multiple_of((step - first) * STAGE_ROWS, STAGE_ROWS)
            dst[pl.ds(r0, STAGE_ROWS), :] = src[...].astype(BF16)

        first += n_chunks

    @pl.when(step >= first)
    def _():
        rs = y_ref.shape[0] // sub
        for r in range(sub):
            rows = slice(r * rs, (r + 1) * rs)
            yb = jax.nn.gelu(y_ref[rows, :])
            z = jnp.dot(yb.astype(BF16), wg_ref[...], preferred_element_type=F32) + bgl_ref[...]
            b = (yb * _sigmoid(z) * szb_ref[rows, :].astype(F32)).astype(BF16)
            ya = jnp.dot(a_ref[rows, :], wa_ref[...], preferred_element_type=F32)
            yb2 = jnp.dot(b, wb_ref[...], preferred_element_type=F32)
            m_ref[rows, :] = (sga_ref[rows, :].astype(F32) * ya
                              + sgb_ref[rows, :].astype(F32) * yb2).astype(m_ref.dtype)


def _tail(y, szb, a_in, sga, sgb, b_glu, w_glu, w_a, w_b, layer, tm=512, sub=2):
    n, w = y.shape
    d = a_in.shape[1]
    counts = [rows // STAGE_ROWS for rows in (w, d, w)]
    starts = [sum(counts[:k]) for k in range(len(counts))]
    w_steps = sum(counts)

    def tile(width):
        return pl.BlockSpec((tm, width), lambda s: (jnp.maximum(s - w_steps, 0), 0))

    def staged(k, width):
        return pl.BlockSpec((None, STAGE_ROWS, width),
                            lambda s: (layer, jnp.clip(s - starts[k], 0, counts[k] - 1), 0))

    vec = pl.BlockSpec((None, 1, w), lambda s: (layer, 0, 0))
    return pl.pallas_call(
        functools.partial(_tail_kernel, sub=sub),
        grid=(w_steps + n // tm,),
        in_specs=[tile(w), tile(w), tile(d), tile(d), tile(d), vec,
                  staged(0, w), staged(1, d), staged(2, d)],
        out_specs=tile(d),
        out_shape=jax.ShapeDtypeStruct((n, d), BF16),
        scratch_shapes=[pltpu.VMEM((w, w), BF16), pltpu.VMEM((d, d), BF16), pltpu.VMEM((w, d), BF16)],
        compiler_params=_params("arbitrary"),
        name="glu_gated_merge",
    )(y, szb, a_in, sga, sgb, b_glu.reshape(-1, 1, w), w_glu, w_a, w_b)


def _out_kernel(m_ref, wo32_ref, x_ref, g_ref, *refs, emit_x, sub, w_steps):
    o_refs, wo_ref = refs[:-1], refs[-1]
    step = pl.program_id(0)
    wr = wo32_ref.shape[0]

    @pl.when(step < w_steps)
    def _():
        r0 = pl.multiple_of(step * wr, wr)
        wo_ref[pl.ds(r0, wr), :] = wo32_ref[...].astype(BF16)

    @pl.when(step >= w_steps)
    def _():
        rs = m_ref.shape[0] // sub
        for r in range(sub):
            rows = slice(r * rs, (r + 1) * rs)
            xn = x_ref[rows, :] + jnp.dot(m_ref[rows, :], wo_ref[...], preferred_element_type=F32)
            ms = jnp.mean(xn * xn, axis=-1, keepdims=True)
            normed = xn * lax.rsqrt(ms + RMS_EPS) * g_ref[...]
            if emit_x:
                o_refs[0][rows, :] = xn
                o_refs[1][rows, :] = normed.astype(o_refs[1].dtype)
            else:
                o_refs[0][rows, :] = normed.astype(o_refs[0].dtype)


def _out_proj(m, w_o, layer, x, g, emit_x, tm=512, sub=2, w_steps=4):
    n, d = x.shape
    row = pl.BlockSpec((tm, d), lambda s: (jnp.maximum(s - w_steps, 0), 0))
    if emit_x:
        out_specs = [row, row]
        out_shape = [jax.ShapeDtypeStruct((n, d), F32), jax.ShapeDtypeStruct((n, d), BF16)]
    else:
        out_specs = row
        out_shape = jax.ShapeDtypeStruct((n, d), F32)
    w_spec = pl.BlockSpec((None, d // w_steps, d), lambda s: (layer, jnp.minimum(s, w_steps - 1), 0))
    return pl.pallas_call(
        functools.partial(_out_kernel, emit_x=emit_x, sub=sub, w_steps=w_steps),
        grid=(w_steps + n // tm,),
        in_specs=[row, w_spec, row, pl.BlockSpec((1, d), lambda s: (0, 0))],
        out_specs=out_specs,
        out_shape=out_shape,
        scratch_shapes=[pltpu.VMEM((d, d), BF16)],
        compiler_params=_params("arbitrary"),
        name="out_proj_residual_norm",
    )(m, w_o, x, g.reshape(1, d))


def kernel(x, norm_g, w_in, conv_w, w_out_a, a_re, a_im, log_dt, b_re, b_im, c_re, c_im,
           d_skip, w_glu, b_glu, w_out_b, w_o, final_g):
    bsz, seq, d = x.shape
    depth = norm_g.shape[0]
    n = bsz * seq
    n_chunks = seq // SSM_CHUNK
    assert 2 * bsz == 8 and n_chunks % 2 == 0
    xf = x.reshape(n, d)
    h = _rmsnorm(xf, norm_g[0])
    ssm_ops = _ssm_weights(a_re, a_im, log_dt, b_re, b_im, c_re, c_im)
    for l in range(depth):
        a_in, sga, sgb = _in_proj_a(h, w_in, conv_w, l, seq)
        u, szb = _in_proj_b(h, w_in, l)
        y = _ssm(u.reshape(bsz, seq, SSM_WIDTH), *ssm_ops, d_skip, l).reshape(n, SSM_WIDTH)
        m = _tail(y, szb, a_in, sga, sgb, b_glu, w_glu, w_out_a, w_out_b, l)
        if l + 1 < depth:
            xf, h = _out_proj(m, w_o, l, xf, norm_g[l + 1], True)
        else:
            out = _out_proj(m, w_o, l, xf, final_g, False)
    return out.reshape(bsz, seq, d)
```

```python
import functools

import jax
import jax.numpy as jnp
from jax import lax
from jax.experimental import pallas as pl
from jax.experimental.pallas import tpu as pltpu

F32 = jnp.float32
BF16 = jnp.bfloat16

D_MODEL = 2048
CONV_WIDTH = D_MODEL
CONV_K = 3
SSM_WIDTH = D_MODEL // 2
SSM_GROUP = 16
SSM_GROUPS = SSM_WIDTH // SSM_GROUP
SSM_STATE = 64
RMS_EPS = 1e-6

OFF_V = 0
OFF_BG = CONV_WIDTH
OFF_CG = 2 * CONV_WIDTH
OFF_ZA = 3 * CONV_WIDTH
OFF_U = 4 * CONV_WIDTH
OFF_ZB = OFF_U + SSM_WIDTH
OFF_G = OFF_ZB + SSM_WIDTH

SSM_CHUNK = 16
CHUNK_LANES = SSM_CHUNK * SSM_GROUP
LANE_TILE = 128
STATE_LANES = LANE_TILE
GROUPS_PER_STEP = 8
PREP_ROWS = 24

V7X_VMEM_LIMIT = 56 * 1024 * 1024


def _sigmoid(x):
    return 0.5 * (jnp.tanh(0.5 * x) + 1.0)


def _params(*sem):
    return pltpu.CompilerParams(dimension_semantics=sem, vmem_limit_bytes=V7X_VMEM_LIMIT)


def _rmsnorm_kernel(x_ref, g_ref, o_ref):
    x = x_ref[...]
    ms = jnp.mean(x * x, axis=-1, keepdims=True)
    o_ref[...] = (x * lax.rsqrt(ms + RMS_EPS) * g_ref[...]).astype(o_ref.dtype)


def _rmsnorm(x, g, tm=512):
    n, d = x.shape
    return pl.pallas_call(
        _rmsnorm_kernel,
        grid=(n // tm,),
        in_specs=[pl.BlockSpec((tm, d), lambda i: (i, 0)),
                  pl.BlockSpec((1, d), lambda i: (0, 0))],
        out_specs=pl.BlockSpec((tm, d), lambda i: (i, 0)),
        out_shape=jax.ShapeDtypeStruct((n, d), BF16),
        compiler_params=_params("arbitrary"),
        name="rmsnorm",
    )(x, g.reshape(1, d))


def _in_proj_a_kernel(h_ref, wv_ref, wbg_ref, wcg_ref, wza_ref, wga_ref, wgb_ref, cw_ref,
                      a_ref, sga_ref, sgb_ref, carry_ref, w_ref, *, tiles_per_seq, sub):
    i = pl.program_id(1)

    @pl.when(i == 0)
    def _():
        for k, src in enumerate((wv_ref, wbg_ref, wcg_ref, wza_ref, wga_ref, wgb_ref)):
            w_ref[k] = src[...].astype(BF16)

    @pl.when(i % tiles_per_seq == 0)
    def _():
        carry_ref[...] = jnp.zeros_like(carry_ref)

    cw = cw_ref[...]
    rs = h_ref.shape[0] // sub
    row = lax.broadcasted_iota(jnp.int32, (rs, a_ref.shape[1]), 0)
    tail = carry_ref[...]
    for r in range(sub):
        rows = slice(r * rs, (r + 1) * rs)
        h = h_ref[rows, :]
        v = jnp.dot(h, w_ref[0], preferred_element_type=F32)
        bg = jnp.dot(h, w_ref[1], preferred_element_type=F32)
        cg = jnp.dot(h, w_ref[2], preferred_element_type=F32)
        za = jnp.dot(h, w_ref[3], preferred_element_type=F32)
        cv = cg * v
        prev1 = tail[7:8, :]
        prev2 = tail[6:7, :]
        cv1 = jnp.where(row == 0, prev1, pltpu.roll(cv, 1, axis=0))
        cv2 = jnp.where(row == 0, prev2, jnp.where(row == 1, prev1, pltpu.roll(cv, 2, axis=0)))
        conv = cw[0:1, :] * cv2 + cw[1:2, :] * cv1 + cw[2:3, :] * cv
        a_ref[rows, :] = (bg * conv * (za * _sigmoid(za))).astype(a_ref.dtype)
        tail = cv[rs - 8:rs, :]
        ga = jnp.dot(h, w_ref[4], preferred_element_type=F32)
        sga_ref[rows, :] = _sigmoid(ga).astype(sga_ref.dtype)
        gb = jnp.dot(h, w_ref[5], preferred_element_type=F32)
        sgb_ref[rows, :] = _sigmoid(gb).astype(sgb_ref.dtype)
    carry_ref[...] = tail


def _in_proj_a(h, w_in, conv_w, layer, seq_len, tm=1024, tn=256, sub=4):
    n, d = h.shape
    nj = CONV_WIDTH // tn
    offsets = (OFF_V, OFF_BG, OFF_CG, OFF_ZA, OFF_G, OFF_G + D_MODEL)

    def wspec(off):
        return pl.BlockSpec((None, d, tn), lambda j, i, o=off // tn: (layer, 0, o + j))

    out = pl.BlockSpec((tm, tn), lambda j, i: (i, j))
    shape = jax.ShapeDtypeStruct((n, CONV_WIDTH), BF16)
    return pl.pallas_call(
        functools.partial(_in_proj_a_kernel, tiles_per_seq=seq_len // tm, sub=sub),
        grid=(nj, n // tm),
        in_specs=[pl.BlockSpec((tm, d), lambda j, i: (i, 0))] + [wspec(o) for o in offsets]
                 + [pl.BlockSpec((None, CONV_K, tn), lambda j, i: (layer, 0, j))],
        out_specs=[out, out, out],
        out_shape=[shape, shape, shape],
        scratch_shapes=[pltpu.VMEM((8, tn), F32), pltpu.VMEM((len(offsets), d, tn), BF16)],
        compiler_params=_params("arbitrary", "arbitrary"),
        name="in_proj_conv_gates",
    )(h, *([w_in] * len(offsets)), conv_w)


def _in_proj_b_kernel(h_ref, wu_ref, wzb_ref, u_ref, szb_ref, w_ref, *, sub):
    @pl.when(pl.program_id(1) == 0)
    def _():
        w_ref[0] = wu_ref[...].astype(BF16)
        w_ref[1] = wzb_ref[...].astype(BF16)

    rs = h_ref.shape[0] // sub
    for r in range(sub):
        rows = slice(r * rs, (r + 1) * rs)
        h = h_ref[rows, :]
        u_ref[rows, :] = jnp.dot(h, w_ref[0], preferred_element_type=F32)
        zb = jnp.dot(h, w_ref[1], preferred_element_type=F32)
        szb_ref[rows, :] = (zb * _sigmoid(zb)).astype(szb_ref.dtype)


def _in_proj_b(h, w_in, layer, tm=1024, tn=512, sub=4):
    n, d = h.shape

    def wspec(off):
        return pl.BlockSpec((None, d, tn), lambda j, i, o=off // tn: (layer, 0, o + j))

    out = pl.BlockSpec((tm, tn), lambda j, i: (i, j))
    return pl.pallas_call(
        functools.partial(_in_proj_b_kernel, sub=sub),
        grid=(SSM_WIDTH // tn, n // tm),
        in_specs=[pl.BlockSpec((tm, d), lambda j, i: (i, 0)), wspec(OFF_U), wspec(OFF_ZB)],
        out_specs=[out, out],
        out_shape=[jax.ShapeDtypeStruct((n, SSM_WIDTH), F32), jax.ShapeDtypeStruct((n, SSM_WIDTH), BF16)],
        scratch_shapes=[pltpu.VMEM((2, d, tn), BF16)],
        compiler_params=_params("arbitrary", "arbitrary"),
        name="in_proj_ssm",
    )(h, w_in, w_in)


def _ssm_kernel(u_ref, m0_ref, win_ref, wout_ref, lre_ref, lim_ref, y_ref,
                t_ref, ub_ref, t2_ref, sre_ref, sim_ref, hre_ref, him_ref, *, n_chunks, batch):
    gb = m0_ref.shape[0]
    t = SSM_CHUNK
    for b in range(batch):
        for s in range(t):
            xt = u_ref[b, pl.ds(s, n_chunks, stride=t), :].astype(BF16).T
            for g in range(gb):
                row0 = g * CHUNK_LANES + s * SSM_GROUP
                t_ref[b, row0:row0 + SSM_GROUP, :] = xt[g * SSM_GROUP:(g + 1) * SSM_GROUP, :]
    for b in range(batch):
        for g in range(gb):
            ub_ref[g, b * n_chunks:(b + 1) * n_chunks, :] = t_ref[b, g * CHUNK_LANES:(g + 1) * CHUNK_LANES, :].T

    for g in range(gb):
        s = jnp.dot(ub_ref[g], win_ref[g], preferred_element_type=F32)
        for b in range(batch):
            sb = s[b * n_chunks:(b + 1) * n_chunks, :]
            sre_ref[g, pl.ds(b, n_chunks, stride=batch), :] = sb[:, :STATE_LANES]
            sim_ref[g, pl.ds(b, n_chunks, stride=batch), :] = sb[:, STATE_LANES:]
    lre = lre_ref[0]
    lim = lim_ref[0]
    rows = 2 * batch
    first = lax.broadcasted_iota(jnp.int32, (rows, gb * STATE_LANES), 0) < batch

    def advance(h_r, h_i, s_r, s_i):
        return lre * h_r - lim * h_i + s_r, lre * h_i + lim * h_r + s_i

    def slab(ref, r0):
        return jnp.concatenate([ref[g, pl.ds(r0, rows), :] for g in range(gb)], axis=1)

    def step(j, carry):
        h_r, h_i = carry
        r0 = pl.multiple_of(j * rows, rows)
        s_r = slab(sre_ref, r0)
        s_i = slab(sim_ref, r0)
        n_r, n_i = advance(h_r, h_i, s_r, s_i)
        h_r = jnp.where(first, h_r, pltpu.roll(n_r, batch, axis=0))
        h_i = jnp.where(first, h_i, pltpu.roll(n_i, batch, axis=0))
        for g in range(gb):
            hre_ref[g, pl.ds(r0, rows), :] = h_r[:, g * STATE_LANES:(g + 1) * STATE_LANES]
            him_ref[g, pl.ds(r0, rows), :] = h_i[:, g * STATE_LANES:(g + 1) * STATE_LANES]
        n_r, n_i = advance(h_r, h_i, s_r, s_i)
        return pltpu.roll(n_r, batch, axis=0), pltpu.roll(n_i, batch, axis=0)

    zero = jnp.zeros((rows, gb * STATE_LANES), F32)
    lax.fori_loop(0, n_chunks // 2, step, (zero, zero))

    for g in range(gb):
        h = jnp.concatenate(
            [jnp.concatenate([ref[g, pl.ds(b, n_chunks, stride=batch), :] for b in range(batch)], axis=0)
             for ref in (hre_ref, him_ref)], axis=1)
        y = jnp.dot(ub_ref[g], m0_ref[g], preferred_element_type=F32)
        y = y + jnp.dot(h.astype(BF16), wout_ref[g], preferred_element_type=F32)
        for b in range(batch):
            at = y[b * n_chunks:(b + 1) * n_chunks, :].T
            for tt in range(t):
                row0 = tt * LANE_TILE + g * SSM_GROUP
                t2_ref[b, row0:row0 + SSM_GROUP, :] = at[tt * SSM_GROUP:(tt + 1) * SSM_GROUP, :]
    for b in range(batch):
        for tt in range(t):
            y_ref[b, pl.ds(tt, n_chunks, stride=t), :] = t2_ref[b, tt * LANE_TILE:(tt + 1) * LANE_TILE, :].T


def _ssm(u_p, m0, w_in_state, w_out_state, lam_re, lam_im, layer):
    batch, seq, width = u_p.shape
    n_inst = batch * seq // SSM_CHUNK
    gb = GROUPS_PER_STEP
    assert gb * SSM_GROUP == LANE_TILE and n_inst // batch == LANE_TILE
    first = layer * (width // LANE_TILE)
    blk = pl.BlockSpec((batch, seq, LANE_TILE), lambda i: (0, 0, i))
    sq = pl.BlockSpec((gb, CHUNK_LANES, CHUNK_LANES), lambda i: (first + i, 0, 0))
    vec = pl.BlockSpec((1, 1, gb * STATE_LANES), lambda i: (first + i, 0, 0))
    return pl.pallas_call(
        functools.partial(_ssm_kernel, n_chunks=n_inst // batch, batch=batch),
        grid=(width // LANE_TILE,),
        in_specs=[blk, sq, sq, sq, vec, vec],
        out_specs=blk,
        out_shape=jax.ShapeDtypeStruct(u_p.shape, F32),
        scratch_shapes=[pltpu.VMEM((batch, gb * CHUNK_LANES, LANE_TILE), BF16),
                        pltpu.VMEM((gb, n_inst, CHUNK_LANES), BF16),
                        pltpu.VMEM((batch, gb * CHUNK_LANES, LANE_TILE), F32)]
                       + [pltpu.VMEM((gb, n_inst, STATE_LANES), F32) for _ in range(4)],
        compiler_params=_params("arbitrary"),
        name="s5_chunked_scan",
    )(u_p, m0, w_in_state, w_out_state, lam_re, lam_im)


def _split_bf16(x):
    hi = x.astype(BF16)
    return hi, (x - hi.astype(F32)).astype(BF16)


def _dot_nt_f32(a, b):
    dn = (((1,), (1,)), ((), ()))
    a_hi, a_lo = _split_bf16(a)
    b_hi, b_lo = _split_bf16(b)

    def d(x, y):
        return lax.dot_general(x, y, dn, preferred_element_type=F32)

    return d(a_hi, b_hi) + d(a_hi, b_lo) + d(a_lo, b_hi)


def _ssm_prep_kernel(are_ref, aim_ref, ldt_ref, btr_ref, bti_ref, cre_ref, cim_ref,
                     m0_ref, win_ref, wout_ref, lre_ref, lim_ref):
    gb = are_ref.shape[0]
    t = SSM_CHUNK
    tau = lax.broadcasted_iota(jnp.int32, (PREP_ROWS, STATE_LANES), 0).astype(F32)
    lane = lax.broadcasted_iota(jnp.int32, (SSM_GROUP, CHUNK_LANES), 1)
    for g in range(gb):
        are = are_ref[g]
        aim = aim_ref[g]
        dt = jnp.exp(ldt_ref[g])
        mag = jnp.exp(tau * (dt * are))
        ang = tau * (dt * aim)
        pw_re = mag * jnp.cos(ang)
        pw_im = mag * jnp.sin(ang)
        nr = pw_re[1:2] - 1.0
        ni = pw_im[1:2]
        den = are * are + aim * aim
        cr = (nr * are + ni * aim) / den
        ci = (ni * are - nr * aim) / den
        btr = btr_ref[g]
        bti = bti_ref[g]
        bb_re = cr * btr - ci * bti
        bb_im = cr * bti + ci * btr
        cre = cre_ref[g]
        cim = cim_ref[g]

        win_re, win_im = [], []
        for s in range(t):
            pr = pw_re[t - 1 - s:t - s]
            pi = pw_im[t - 1 - s:t - s]
            win_re.append(pr * bb_re - pi * bb_im)
            win_im.append(pr * bb_im + pi * bb_re)
        win_ref[g] = jnp.concatenate(
            [jnp.concatenate(win_re, axis=0), jnp.concatenate(win_im, axis=0)], axis=1).astype(BF16)

        z_re, z_im = [], []
        for k in range(t + 1):
            pr = pw_re[k:k + 1]
            pi = pw_im[k:k + 1]
            z_re.append(cre * pr - cim * pi)
            z_im.append(cre * pi + cim * pr)

        zo_re = jnp.concatenate(z_re[1:], axis=0)
        zo_im = jnp.concatenate(z_im[1:], axis=0)
        wout_ref[g] = jnp.concatenate([zo_re.T, -(zo_im.T)], axis=0).astype(BF16)

        zk_re = jnp.concatenate(z_re[:t], axis=0)
        zk_im = jnp.concatenate(z_im[:t], axis=0)
        kt = _dot_nt_f32(bb_re, zk_re) - _dot_nt_f32(bb_im, zk_im)
        blocks = [kt]
        for s in range(1, t):
            shifted = pltpu.roll(kt, s * SSM_GROUP, axis=1)
            blocks.append(jnp.where(lane >= s * SSM_GROUP, shifted, 0.0))
        m0_ref[g] = jnp.concatenate(blocks, axis=0).astype(BF16)

        lre_ref[0, :, g * STATE_LANES:(g + 1) * STATE_LANES] = pw_re[t:t + 1]
        lim_ref[0, :, g * STATE_LANES:(g + 1) * STATE_LANES] = pw_im[t:t + 1]


def _ssm_weights(a_re, a_im, log_dt, b_re, b_im, c_re, c_im):
    a_re, a_im, b_re, b_im, c_re, c_im = (
        x.reshape((-1,) + x.shape[2:]) for x in (a_re, a_im, b_re, b_im, c_re, c_im))
    log_dt = log_dt.reshape(-1)
    g, p = a_re.shape
    gb = GROUPS_PER_STEP
    pad = STATE_LANES - p

    def lanes(x, value=0.0):
        return jnp.pad(x.astype(F32), [(0, 0)] * (x.ndim - 1) + [(0, pad)], constant_values=value)

    are = lanes(a_re, -1.0).reshape(g, 1, STATE_LANES)
    aim = lanes(a_im).reshape(g, 1, STATE_LANES)
    ldt = jnp.broadcast_to(log_dt.astype(F32)[:, None, None], (g, 1, STATE_LANES))
    btr = lanes(jnp.swapaxes(b_re, 1, 2))
    bti = lanes(jnp.swapaxes(b_im, 1, 2))
    cre = lanes(c_re)
    cim = lanes(c_im)
    vec = pl.BlockSpec((gb, 1, STATE_LANES), lambda i: (i, 0, 0))
    mat = pl.BlockSpec((gb, SSM_GROUP, STATE_LANES), lambda i: (i, 0, 0))
    sq = pl.BlockSpec((gb, CHUNK_LANES, CHUNK_LANES), lambda i: (i, 0, 0))
    row = pl.BlockSpec((1, 1, gb * STATE_LANES), lambda i: (i, 0, 0))
    sq_shape = jax.ShapeDtypeStruct((g, CHUNK_LANES, CHUNK_LANES), BF16)
    row_shape = jax.ShapeDtypeStruct((g // gb, 1, gb * STATE_LANES), F32)
    return pl.pallas_call(
        _ssm_prep_kernel,
        grid=(g // gb,),
        in_specs=[vec, vec, vec, mat, mat, mat, mat],
        out_specs=[sq, sq, sq, row, row],
        out_shape=[sq_shape, sq_shape, sq_shape, row_shape, row_shape],
        compiler_params=_params("arbitrary"),
        name="s5_chunk_operators",
    )(are, aim, ldt, btr, bti, cre, cim)


STAGE_ROWS = 512


def _tail_kernel(y_ref, u_ref, szb_ref, a_ref, sga_ref, sgb_ref, d_ref, bgl_ref,
                 wg32_ref, wa32_ref, wb32_ref, m_ref, wg_ref, wa_ref, wb_ref, *, sub):
    step = pl.program_id(0)
    stages = ((wg32_ref, wg_ref), (wa32_ref, wa_ref), (wb32_ref, wb_ref))
    first = 0
    for src, dst in stages:
        n_chunks = dst.shape[0] // STAGE_ROWS

        @pl.when((step >= first) & (step < first + n_chunks))
        def _(src=src, dst=dst, first=first):
            r0 = pl.multiple_of((step - first) * STAGE_ROWS, STAGE_ROWS)
            dst[pl.ds(r0, STAGE_ROWS), :] = src[...].astype(BF16)

        first += n_chunks

    @pl.when(step >= first)
    def _():
        rs = y_ref.shape[0] // sub
        for r in range(sub):
            rows = slice(r * rs, (r + 1) * rs)
            yb = jax.nn.gelu(y_ref[rows, :] + d_ref[...] * u_ref[rows, :])
            z = jnp.dot(yb.astype(BF16), wg_ref[...], preferred_element_type=F32) + bgl_ref[...]
            b = (yb * _sigmoid(z) * szb_ref[rows, :].astype(F32)).astype(BF16)
            ya = jnp.dot(a_ref[rows, :], wa_ref[...], preferred_element_type=F32)
            yb2 = jnp.dot(b, wb_ref[...], preferred_element_type=F32)
            m_ref[rows, :] = (sga_ref[rows, :].astype(F32) * ya
                              + sgb_ref[rows, :].astype(F32) * yb2).astype(m_ref.dtype)


def _tail(y, u, szb, a_in, sga, sgb, d_skip, b_glu, w_glu, w_a, w_b, layer, tm=256, sub=1):
    n, w = y.shape
    d = a_in.shape[1]
    counts = [rows // STAGE_ROWS for rows in (w, d, w)]
    starts = [sum(counts[:k]) for k in range(len(counts))]
    w_steps = sum(counts)

    def tile(width):
        return pl.BlockSpec((tm, width), lambda s: (jnp.maximum(s - w_steps, 0), 0))

    def staged(k, width):
        return pl.BlockSpec((None, STAGE_ROWS, width),
                            lambda s: (layer, jnp.clip(s - starts[k], 0, counts[k] - 1), 0))

    vec = pl.BlockSpec((None, 1, w), lambda s: (layer, 0, 0))
    return pl.pallas_call(
        functools.partial(_tail_kernel, sub=sub),
        grid=(w_steps + n // tm,),
        in_specs=[tile(w), tile(w), tile(w), tile(d), tile(d), tile(d), vec, vec,
                  staged(0, w), staged(1, d), staged(2, d)],
        out_specs=tile(d),
        out_shape=jax.ShapeDtypeStruct((n, d), BF16),
        scratch_shapes=[pltpu.VMEM((w, w), BF16), pltpu.VMEM((d, d), BF16), pltpu.VMEM((w, d), BF16)],
        compiler_params=_params("arbitrary"),
        name="glu_gated_merge",
    )(y, u, szb, a_in, sga, sgb, d_skip.reshape(-1, 1, w), b_glu.reshape(-1, 1, w), w_glu, w_a, w_b)


def _out_kernel(m_ref, wo32_ref, x_ref, g_ref, *refs, emit_x, sub, w_steps):
    o_refs, wo_ref = refs[:-1], refs[-1]
    step = pl.program_id(0)
    wr = wo32_ref.shape[0]

    @pl.when(step < w_steps)
    def _():
        r0 = pl.multiple_of(step * wr, wr)
        wo_ref[pl.ds(r0, wr), :] = wo32_ref[...].astype(BF16)

    @pl.when(step >= w_steps)
    def _():
        rs = m_ref.shape[0] // sub
        for r in range(sub):
            rows = slice(r * rs, (r + 1) * rs)
            xn = x_ref[rows, :] + jnp.dot(m_ref[rows, :], wo_ref[...], preferred_element_type=F32)
            ms = jnp.mean(xn * xn, axis=-1, keepdims=True)
            normed = xn * lax.rsqrt(ms + RMS_EPS) * g_ref[...]
            if emit_x:
                o_refs[0][rows, :] = xn
                o_refs[1][rows, :] = normed.astype(o_refs[1].dtype)
            else:
                o_refs[0][rows, :] = normed.astype(o_refs[0].dtype)


def _out_proj(m, w_o, layer, x, g, emit_x, tm=512, sub=2, w_steps=2):
    n, d = x.shape
    row = pl.BlockSpec((tm, d), lambda s: (jnp.maximum(s - w_steps, 0), 0))
    if emit_x:
        out_specs = [row, row]
        out_shape = [jax.ShapeDtypeStruct((n, d), F32), jax.ShapeDtypeStruct((n, d), BF16)]
    else:
        out_specs = row
        out_shape = jax.ShapeDtypeStruct((n, d), F32)
    w_spec = pl.BlockSpec((None, d // w_steps, d), lambda s: (layer, jnp.minimum(s, w_steps - 1), 0))
    return pl.pallas_call(
        functools.partial(_out_kernel, emit_x=emit_x, sub=sub, w_steps=w_steps),
        grid=(w_steps + n // tm,),
        in_specs=[row, w_spec, row, pl.BlockSpec((1, d), lambda s: (0, 0))],
        out_specs=out_specs,
        out_shape=out_shape,
        scratch_shapes=[pltpu.VMEM((d, d), BF16)],
        compiler_params=_params("arbitrary"),
        name="out_proj_residual_norm",
    )(m, w_o, x, g.reshape(1, d))


def kernel(x, norm_g, w_in, conv_w, w_out_a, a_re, a_im, log_dt, b_re, b_im, c_re, c_im,
           d_skip, w_glu, b_glu, w_out_b, w_o, final_g):
    bsz, seq, d = x.shape
    depth = norm_g.shape[0]
    n = bsz * seq
    n_chunks = seq // SSM_CHUNK
    assert 2 * bsz == 8 and n_chunks % 2 == 0
    xf = x.reshape(n, d)
    h = _rmsnorm(xf, norm_g[0])
    ssm_ops = _ssm_weights(a_re, a_im, log_dt, b_re, b_im, c_re, c_im)
    for l in range(depth):
        a_in, sga, sgb = _in_proj_a(h, w_in, conv_w, l, seq)
        u, szb = _in_proj_b(h, w_in, l)
        y = _ssm(u.reshape(bsz, seq, SSM_WIDTH), *ssm_ops, l).reshape(n, SSM_WIDTH)
        m = _tail(y, u, szb, a_in, sga, sgb, d_skip, b_glu, w_glu, w_out_a, w_out_b, l)
        if l + 1 < depth:
            xf, h = _out_proj(m, w_o, l, xf, norm_g[l + 1], True)
        else:
            out = _out_proj(m, w_o, l, xf, final_g, False)
    return out.reshape(bsz, seq, d)
```

```python
import functools

import jax
import jax.numpy as jnp
from jax import lax
from jax.experimental import pallas as pl
from jax.experimental.pallas import tpu as pltpu

F32 = jnp.float32
BF16 = jnp.bfloat16

D_MODEL = 2048
CONV_WIDTH = D_MODEL
CONV_K = 3
SSM_WIDTH = D_MODEL // 2
SSM_GROUP = 16
SSM_GROUPS = SSM_WIDTH // SSM_GROUP
SSM_STATE = 64
RMS_EPS = 1e-6

OFF_V = 0
OFF_BG = CONV_WIDTH
OFF_CG = 2 * CONV_WIDTH
OFF_ZA = 3 * CONV_WIDTH
OFF_U = 4 * CONV_WIDTH
OFF_ZB = OFF_U + SSM_WIDTH
OFF_G = OFF_ZB + SSM_WIDTH

SSM_CHUNK = 16
CHUNK_LANES = SSM_CHUNK * SSM_GROUP
LANE_TILE = 128
STATE_LANES = LANE_TILE
GROUPS_PER_STEP = 8
PREP_ROWS = 24

V7X_VMEM_LIMIT = 56 * 1024 * 1024


def _sigmoid(x):
    return 0.5 * (jnp.tanh(0.5 * x) + 1.0)


def _params(*sem):
    return pltpu.CompilerParams(dimension_semantics=sem, vmem_limit_bytes=V7X_VMEM_LIMIT)


def _rmsnorm_kernel(x_ref, g_ref, o_ref):
    x = x_ref[...]
    ms = jnp.mean(x * x, axis=-1, keepdims=True)
    o_ref[...] = (x * lax.rsqrt(ms + RMS_EPS) * g_ref[...]).astype(o_ref.dtype)


def _rmsnorm(x, g, tm=512):
    n, d = x.shape
    return pl.pallas_call(
        _rmsnorm_kernel,
        grid=(n // tm,),
        in_specs=[pl.BlockSpec((tm, d), lambda i: (i, 0)),
                  pl.BlockSpec((1, d), lambda i: (0, 0))],
        out_specs=pl.BlockSpec((tm, d), lambda i: (i, 0)),
        out_shape=jax.ShapeDtypeStruct((n, d), BF16),
        compiler_params=_params("arbitrary"),
        name="rmsnorm",
    )(x, g.reshape(1, d))


def _in_proj_a_kernel(h_ref, w_hbm, cw_ref, a_ref, sga_ref, sgb_ref,
                      carry_ref, w_ref, stage_ref, sem, *, layer, offsets, n_blocks, tiles_per_seq, sub):
    j = pl.program_id(0)
    i = pl.program_id(1)
    tn = a_ref.shape[1]

    def weight_copies(block):
        return [pltpu.make_async_copy(
            w_hbm.at[layer, :, pl.ds(pl.multiple_of(off + block * tn, tn), tn)], stage_ref.at[k], sem.at[k])
            for k, off in enumerate(offsets)]

    @pl.when(i == 0)
    def _():
        @pl.when(j == 0)
        def _():
            for c in weight_copies(0):
                c.start()

        for c in weight_copies(j):
            c.wait()
        for k in range(len(offsets)):
            w_ref[k] = stage_ref[k].astype(BF16)

        @pl.when(j + 1 < n_blocks)
        def _():
            for c in weight_copies(j + 1):
                c.start()

    @pl.when(i % tiles_per_seq == 0)
    def _():
        carry_ref[...] = jnp.zeros_like(carry_ref)

    cw = cw_ref[...]
    rs = h_ref.shape[0] // sub
    row = lax.broadcasted_iota(jnp.int32, (rs, tn), 0)
    tail = carry_ref[...]
    for r in range(sub):
        rows = slice(r * rs, (r + 1) * rs)
        h = h_ref[rows, :]
        v = jnp.dot(h, w_ref[0], preferred_element_type=F32)
        bg = jnp.dot(h, w_ref[1], preferred_element_type=F32)
        cg = jnp.dot(h, w_ref[2], preferred_element_type=F32)
        za = jnp.dot(h, w_ref[3], preferred_element_type=F32)
        cv = cg * v
        prev1 = tail[7:8, :]
        prev2 = tail[6:7, :]
        cv1 = jnp.where(row == 0, prev1, pltpu.roll(cv, 1, axis=0))
        cv2 = jnp.where(row == 0, prev2, jnp.where(row == 1, prev1, pltpu.roll(cv, 2, axis=0)))
        conv = cw[0:1, :] * cv2 + cw[1:2, :] * cv1 + cw[2:3, :] * cv
        a_ref[rows, :] = (bg * conv * (za * _sigmoid(za))).astype(a_ref.dtype)
        tail = cv[rs - 8:rs, :]
        ga = jnp.dot(h, w_ref[4], preferred_element_type=F32)
        sga_ref[rows, :] = _sigmoid(ga).astype(sga_ref.dtype)
        gb = jnp.dot(h, w_ref[5], preferred_element_type=F32)
        sgb_ref[rows, :] = _sigmoid(gb).astype(sgb_ref.dtype)
    carry_ref[...] = tail


def _in_proj_a(h, w_in, conv_w, layer, seq_len, tm=2048, tn=256, sub=8):
    n, d = h.shape
    nj = CONV_WIDTH // tn
    offsets = (OFF_V, OFF_BG, OFF_CG, OFF_ZA, OFF_G, OFF_G + D_MODEL)
    out = pl.BlockSpec((tm, tn), lambda j, i: (i, j))
    shape = jax.ShapeDtypeStruct((n, CONV_WIDTH), BF16)
    return pl.pallas_call(
        functools.partial(_in_proj_a_kernel, layer=layer, offsets=offsets, n_blocks=nj,
                          tiles_per_seq=seq_len // tm, sub=sub),
        grid=(nj, n // tm),
        in_specs=[pl.BlockSpec((tm, d), lambda j, i: (i, 0)),
                  pl.BlockSpec(memory_space=pl.ANY),
                  pl.BlockSpec((None, CONV_K, tn), lambda j, i: (layer, 0, j))],
        out_specs=[out, out, out],
        out_shape=[shape, shape, shape],
        scratch_shapes=[pltpu.VMEM((8, tn), F32), pltpu.VMEM((len(offsets), d, tn), BF16),
                        pltpu.VMEM((len(offsets), d, tn), F32), pltpu.SemaphoreType.DMA((len(offsets),))],
        compiler_params=_params("arbitrary", "arbitrary"),
        name="in_proj_conv_gates",
    )(h, w_in, conv_w)


def _in_proj_b_kernel(h_ref, wu_ref, wzb_ref, u_ref, szb_ref, w_ref, *, sub):
    @pl.when(pl.program_id(1) == 0)
    def _():
        w_ref[0] = wu_ref[...].astype(BF16)
        w_ref[1] = wzb_ref[...].astype(BF16)

    rs = h_ref.shape[0] // sub
    for r in range(sub):
        rows = slice(r * rs, (r + 1) * rs)
        h = h_ref[rows, :]
        u_ref[rows, :] = jnp.dot(h, w_ref[0], preferred_element_type=F32)
        zb = jnp.dot(h, w_ref[1], preferred_element_type=F32)
        szb_ref[rows, :] = (zb * _sigmoid(zb)).astype(szb_ref.dtype)


def _in_proj_b(h, w_in, layer, tm=1024, tn=512, sub=4):
    n, d = h.shape

    def wspec(off):
        return pl.BlockSpec((None, d, tn), lambda j, i, o=off // tn: (layer, 0, o + j))

    out = pl.BlockSpec((tm, tn), lambda j, i: (i, j))
    return pl.pallas_call(
        functools.partial(_in_proj_b_kernel, sub=sub),
        grid=(SSM_WIDTH // tn, n // tm),
        in_specs=[pl.BlockSpec((tm, d), lambda j, i: (i, 0)), wspec(OFF_U), wspec(OFF_ZB)],
        out_specs=[out, out],
        out_shape=[jax.ShapeDtypeStruct((n, SSM_WIDTH), F32), jax.ShapeDtypeStruct((n, SSM_WIDTH), BF16)],
        scratch_shapes=[pltpu.VMEM((2, d, tn), BF16)],
        compiler_params=_params("arbitrary", "arbitrary"),
        name="in_proj_ssm",
    )(h, w_in, w_in)


def _ssm_kernel(u_ref, m0_ref, win_ref, wout_ref, lre_ref, lim_ref, y_ref,
                t_ref, ub_ref, t2_ref, sre_ref, sim_ref, hre_ref, him_ref, *, n_chunks, batch):
    gb = m0_ref.shape[0]
    t = SSM_CHUNK
    for b in range(batch):
        for s in range(t):
            xt = u_ref[b, pl.ds(s, n_chunks, stride=t), :].astype(BF16).T
            for g in range(gb):
                row0 = g * CHUNK_LANES + s * SSM_GROUP
                t_ref[b, row0:row0 + SSM_GROUP, :] = xt[g * SSM_GROUP:(g + 1) * SSM_GROUP, :]
    for b in range(batch):
        for g in range(gb):
            ub_ref[g, b * n_chunks:(b + 1) * n_chunks, :] = t_ref[b, g * CHUNK_LANES:(g + 1) * CHUNK_LANES, :].T

    for g in range(gb):
        s = jnp.dot(ub_ref[g], win_ref[g], preferred_element_type=F32)
        for b in range(batch):
            sb = s[b * n_chunks:(b + 1) * n_chunks, :]
            sre_ref[g, pl.ds(b, n_chunks, stride=batch), :] = sb[:, :STATE_LANES]
            sim_ref[g, pl.ds(b, n_chunks, stride=batch), :] = sb[:, STATE_LANES:]
    lre = lre_ref[0]
    lim = lim_ref[0]
    rows = 2 * batch
    first = lax.broadcasted_iota(jnp.int32, (rows, gb * STATE_LANES), 0) < batch

    def advance(h_r, h_i, s_r, s_i):
        return lre * h_r - lim * h_i + s_r, lre * h_i + lim * h_r + s_i

    def slab(ref, r0):
        return jnp.concatenate([ref[g, pl.ds(r0, rows), :] for g in range(gb)], axis=1)

    def step(j, carry):
        h_r, h_i = carry
        r0 = pl.multiple_of(j * rows, rows)
        s_r = slab(sre_ref, r0)
        s_i = slab(sim_ref, r0)
        n_r, n_i = advance(h_r, h_i, s_r, s_i)
        h_r = jnp.where(first, h_r, pltpu.roll(n_r, batch, axis=0))
        h_i = jnp.where(first, h_i, pltpu.roll(n_i, batch, axis=0))
        for g in range(gb):
            hre_ref[g, pl.ds(r0, rows), :] = h_r[:, g * STATE_LANES:(g + 1) * STATE_LANES]
            him_ref[g, pl.ds(r0, rows), :] = h_i[:, g * STATE_LANES:(g + 1) * STATE_LANES]
        n_r, n_i = advance(h_r, h_i, s_r, s_i)
        return pltpu.roll(n_r, batch, axis=0), pltpu.roll(n_i, batch, axis=0)

    zero = jnp.zeros((rows, gb * STATE_LANES), F32)
    lax.fori_loop(0, n_chunks // 2, step, (zero, zero))

    for g in range(gb):
        h = jnp.concatenate(
            [jnp.concatenate([ref[g, pl.ds(b, n_chunks, stride=batch), :] for b in range(batch)], axis=0)
             for ref in (hre_ref, him_ref)], axis=1)
        y = jnp.dot(ub_ref[g], m0_ref[g], preferred_element_type=F32)
        y = y + jnp.dot(h.astype(BF16), wout_ref[g], preferred_element_type=F32)
        for b in range(batch):
            at = y[b * n_chunks:(b + 1) * n_chunks, :].T
            for tt in range(t):
                row0 = tt * LANE_TILE + g * SSM_GROUP
                t2_ref[b, row0:row0 + SSM_GROUP, :] = at[tt * SSM_GROUP:(tt + 1) * SSM_GROUP, :]
    for b in range(batch):
        for tt in range(t):
            y_ref[b, pl.ds(tt, n_chunks, stride=t), :] = t2_ref[b, tt * LANE_TILE:(tt + 1) * LANE_TILE, :].T


def _ssm(u_p, m0, w_in_state, w_out_state, lam_re, lam_im, layer):
    batch, seq, width = u_p.shape
    n_inst = batch * seq // SSM_CHUNK
    gb = GROUPS_PER_STEP
    assert gb * SSM_GROUP == LANE_TILE and n_inst // batch == LANE_TILE
    first = layer * (width // LANE_TILE)
    blk = pl.BlockSpec((batch, seq, LANE_TILE), lambda i: (0, 0, i))
    sq = pl.BlockSpec((gb, CHUNK_LANES, CHUNK_LANES), lambda i: (first + i, 0, 0))
    vec = pl.BlockSpec((1, 1, gb * STATE_LANES), lambda i: (first + i, 0, 0))
    return pl.pallas_call(
        functools.partial(_ssm_kernel, n_chunks=n_inst // batch, batch=batch),
        grid=(width // LANE_TILE,),
        in_specs=[blk, sq, sq, sq, vec, vec],
        out_specs=blk,
        out_shape=jax.ShapeDtypeStruct(u_p.shape, F32),
        scratch_shapes=[pltpu.VMEM((batch, gb * CHUNK_LANES, LANE_TILE), BF16),
                        pltpu.VMEM((gb, n_inst, CHUNK_LANES), BF16),
                        pltpu.VMEM((batch, gb * CHUNK_LANES, LANE_TILE), F32)]
                       + [pltpu.VMEM((gb, n_inst, STATE_LANES), F32) for _ in range(4)],
        compiler_params=_params("arbitrary"),
        name="s5_chunked_scan",
    )(u_p, m0, w_in_state, w_out_state, lam_re, lam_im)


def _split_bf16(x):
    hi = x.astype(BF16)
    return hi, (x - hi.astype(F32)).astype(BF16)


def _dot_nt_f32(a, b):
    dn = (((1,), (1,)), ((), ()))
    a_hi, a_lo = _split_bf16(a)
    b_hi, b_lo = _split_bf16(b)

    def d(x, y):
        return lax.dot_general(x, y, dn, preferred_element_type=F32)

    return d(a_hi, b_hi) + d(a_hi, b_lo) + d(a_lo, b_hi)


def _ssm_prep_kernel(are_ref, aim_ref, ldt_ref, btr_ref, bti_ref, cre_ref, cim_ref,
                     m0_ref, win_ref, wout_ref, lre_ref, lim_ref):
    gb = are_ref.shape[0]
    t = SSM_CHUNK
    tau = lax.broadcasted_iota(jnp.int32, (PREP_ROWS, STATE_LANES), 0).astype(F32)
    lane = lax.broadcasted_iota(jnp.int32, (SSM_GROUP, CHUNK_LANES), 1)
    for g in range(gb):
        are = are_ref[g]
        aim = aim_ref[g]
        dt = jnp.exp(ldt_ref[g])
        mag = jnp.exp(tau * (dt * are))
        ang = tau * (dt * aim)
        pw_re = mag * jnp.cos(ang)
        pw_im = mag * jnp.sin(ang)
        nr = pw_re[1:2] - 1.0
        ni = pw_im[1:2]
        den = are * are + aim * aim
        cr = (nr * are + ni * aim) / den
        ci = (ni * are - nr * aim) / den
        btr = btr_ref[g]
        bti = bti_ref[g]
        bb_re = cr * btr - ci * bti
        bb_im = cr * bti + ci * btr
        cre = cre_ref[g]
        cim = cim_ref[g]

        win_re, win_im = [], []
        for s in range(t):
            pr = pw_re[t - 1 - s:t - s]
            pi = pw_im[t - 1 - s:t - s]
            win_re.append(pr * bb_re - pi * bb_im)
            win_im.append(pr * bb_im + pi * bb_re)
        win_ref[g] = jnp.concatenate(
            [jnp.concatenate(win_re, axis=0), jnp.concatenate(win_im, axis=0)], axis=1).astype(BF16)

        z_re, z_im = [], []
        for k in range(t + 1):
            pr = pw_re[k:k + 1]
            pi = pw_im[k:k + 1]
            z_re.append(cre * pr - cim * pi)
            z_im.append(cre * pi + cim * pr)

        zo_re = jnp.concatenate(z_re[1:], axis=0)
        zo_im = jnp.concatenate(z_im[1:], axis=0)
        wout_ref[g] = jnp.concatenate([zo_re.T, -(zo_im.T)], axis=0).astype(BF16)

        zk_re = jnp.concatenate(z_re[:t], axis=0)
        zk_im = jnp.concatenate(z_im[:t], axis=0)
        kt = _dot_nt_f32(bb_re, zk_re) - _dot_nt_f32(bb_im, zk_im)
        blocks = [kt]
        for s in range(1, t):
            shifted = pltpu.roll(kt, s * SSM_GROUP, axis=1)
            blocks.append(jnp.where(lane >= s * SSM_GROUP, shifted, 0.0))
        m0_ref[g] = jnp.concatenate(blocks, axis=0).astype(BF16)

        lre_ref[0, :, g * STATE_LANES:(g + 1) * STATE_LANES] = pw_re[t:t + 1]
        lim_ref[0, :, g * STATE_LANES:(g + 1) * STATE_LANES] = pw_im[t:t + 1]


def _ssm_weights(a_re, a_im, log_dt, b_re, b_im, c_re, c_im):
    a_re, a_im, b_re, b_im, c_re, c_im = (
        x.reshape((-1,) + x.shape[2:]) for x in (a_re, a_im, b_re, b_im, c_re, c_im))
    log_dt = log_dt.reshape(-1)
    g, p = a_re.shape
    gb = GROUPS_PER_STEP
    pad = STATE_LANES - p

    def lanes(x, value=0.0):
        return jnp.pad(x.astype(F32), [(0, 0)] * (x.ndim - 1) + [(0, pad)], constant_values=value)

    are = lanes(a_re, -1.0).reshape(g, 1, STATE_LANES)
    aim = lanes(a_im).reshape(g, 1, STATE_LANES)
    ldt = jnp.broadcast_to(log_dt.astype(F32)[:, None, None], (g, 1, STATE_LANES))
    btr = lanes(jnp.swapaxes(b_re, 1, 2))
    bti = lanes(jnp.swapaxes(b_im, 1, 2))
    cre = lanes(c_re)
    cim = lanes(c_im)
    vec = pl.BlockSpec((gb, 1, STATE_LANES), lambda i: (i, 0, 0))
    mat = pl.BlockSpec((gb, SSM_GROUP, STATE_LANES), lambda i: (i, 0, 0))
    sq = pl.BlockSpec((gb, CHUNK_LANES, CHUNK_LANES), lambda i: (i, 0, 0))
    row = pl.BlockSpec((1, 1, gb * STATE_LANES), lambda i: (i, 0, 0))
    sq_shape = jax.ShapeDtypeStruct((g, CHUNK_LANES, CHUNK_LANES), BF16)
    row_shape = jax.ShapeDtypeStruct((g // gb, 1, gb * STATE_LANES), F32)
    return pl.pallas_call(
        _ssm_prep_kernel,
        grid=(g // gb,),
        in_specs=[vec, vec, vec, mat, mat, mat, mat],
        out_specs=[sq, sq, sq, row, row],
        out_shape=[sq_shape, sq_shape, sq_shape, row_shape, row_shape],
        compiler_params=_params("arbitrary"),
        name="s5_chunk_operators",
    )(are, aim, ldt, btr, bti, cre, cim)


STAGE_ROWS = 512


def _tail_kernel(y_ref, u_ref, szb_ref, a_ref, sga_ref, sgb_ref, d_ref, bgl_ref,
                 wg32_ref, wa32_ref, wb32_ref, m_ref, wg_ref, wa_ref, wb_ref, *, sub):
    step = pl.program_id(0)
    stages = ((wg32_ref, wg_ref), (wa32_ref, wa_ref), (wb32_ref, wb_ref))
    first = 0
    for src, dst in stages:
        n_chunks = dst.shape[0] // STAGE_ROWS

        @pl.when((step >= first) & (step < first + n_chunks))
        def _(src=src, dst=dst, first=first):
            r0 = pl.multiple_of((step - first) * STAGE_ROWS, STAGE_ROWS)
            dst[pl.ds(r0, STAGE_ROWS), :] = src[...].astype(BF16)

        first += n_chunks

    @pl.when(step >= first)
    def _():
        rs = y_ref.shape[0] // sub
        for r in range(sub):
            rows = slice(r * rs, (r + 1) * rs)
            yb = jax.nn.gelu(y_ref[rows, :] + d_ref[...] * u_ref[rows, :])
            z = jnp.dot(yb.astype(BF16), wg_ref[...], preferred_element_type=F32) + bgl_ref[...]
            b = (yb * _sigmoid(z) * szb_ref[rows, :].astype(F32)).astype(BF16)
            ya = jnp.dot(a_ref[rows, :], wa_ref[...], preferred_element_type=F32)
            yb2 = jnp.dot(b, wb_ref[...], preferred_element_type=F32)
            m_ref[rows, :] = (sga_ref[rows, :].astype(F32) * ya
                              + sgb_ref[rows, :].astype(F32) * yb2).astype(m_ref.dtype)


def _tail(y, u, szb, a_in, sga, sgb, d_skip, b_glu, w_glu, w_a, w_b, layer, tm=256, sub=1):
    n, w = y.shape
    d = a_in.shape[1]
    counts = [rows // STAGE_ROWS for rows in (w, d, w)]
    starts = [sum(counts[:k]) for k in range(len(counts))]
    w_steps = sum(counts)

    def tile(width):
        return pl.BlockSpec((tm, width), lambda s: (jnp.maximum(s - w_steps, 0), 0))

    def staged(k, width):
        return pl.BlockSpec((None, STAGE_ROWS, width),
                            lambda s: (layer, jnp.clip(s - starts[k], 0, counts[k] - 1), 0))

    vec = pl.BlockSpec((None, 1, w), lambda s: (layer, 0, 0))
    return pl.pallas_call(
        functools.partial(_tail_kernel, sub=sub),
        grid=(w_steps + n // tm,),
        in_specs=[tile(w), tile(w), tile(w), tile(d), tile(d), tile(d), vec, vec,
                  staged(0, w), staged(1, d), staged(2, d)],
        out_specs=tile(d),
        out_shape=jax.ShapeDtypeStruct((n, d), BF16),
        scratch_shapes=[pltpu.VMEM((w, w), BF16), pltpu.VMEM((d, d), BF16), pltpu.VMEM((w, d), BF16)],
        compiler_params=_params("arbitrary"),
        name="glu_gated_merge",
    )(y, u, szb, a_in, sga, sgb, d_skip.reshape(-1, 1, w), b_glu.reshape(-1, 1, w), w_glu, w_a, w_b)


def _out_kernel(m_ref, wo32_ref, x_ref, g_ref, *refs, emit_x, sub, w_steps):
    o_refs, wo_ref = refs[:-1], refs[-1]
    step = pl.program_id(0)
    wr = wo32_ref.shape[0]

    @pl.when(step < w_steps)
    def _():
        r0 = pl.multiple_of(step * wr, wr)
        wo_ref[pl.ds(r0, wr), :] = wo32_ref[...].astype(BF16)

    @pl.when(step >= w_steps)
    def _():
        rs = m_ref.shape[0] // sub
        for r in range(sub):
            rows = slice(r * rs, (r + 1) * rs)
            xn = x_ref[rows, :] + jnp.dot(m_ref[rows, :], wo_ref[...], preferred_element_type=F32)
            ms = jnp.mean(xn * xn, axis=-1, keepdims=True)
            normed = xn * lax.rsqrt(ms + RMS_EPS) * g_ref[...]
            if emit_x:
                o_refs[0][rows, :] = xn
                o_refs[1][rows, :] = normed.astype(o_refs[1].dtype)
            else:
                o_refs[0][rows, :] = normed.astype(o_refs[0].dtype)


def _out_proj(m, w_o, layer, x, g, emit_x, tm=512, sub=2, w_steps=2):
    n, d = x.shape
    row = pl.BlockSpec((tm, d), lambda s: (jnp.maximum(s - w_steps, 0), 0))
    if emit_x:
        out_specs = [row, row]
        out_shape = [jax.ShapeDtypeStruct((n, d), F32), jax.ShapeDtypeStruct((n, d), BF16)]
    else:
        out_specs = row
        out_shape = jax.ShapeDtypeStruct((n, d), F32)
    w_spec = pl.BlockSpec((None, d // w_steps, d), lambda s: (layer, jnp.minimum(s, w_steps - 1), 0))
    return pl.pallas_call(
        functools.partial(_out_kernel, emit_x=emit_x, sub=sub, w_steps=w_steps),
        grid=(w_steps + n // tm,),
        in_specs=[row, w_spec, row, pl.BlockSpec((1, d), lambda s: (0, 0))],
        out_specs=out_specs,
        out_shape=out_shape,
        scratch_shapes=[pltpu.VMEM((d, d), BF16)],
        compiler_params=_params("arbitrary"),
        name="out_proj_residual_norm",
    )(m, w_o, x, g.reshape(1, d))


def kernel(x, norm_g, w_in, conv_w, w_out_a, a_re, a_im, log_dt, b_re, b_im, c_re, c_im,
           d_skip, w_glu, b_glu, w_out_b, w_o, final_g):
    bsz, seq, d = x.shape
    depth = norm_g.shape[0]
    n = bsz * seq
    n_chunks = seq // SSM_CHUNK
    assert 2 * bsz == 8 and n_chunks % 2 == 0
    xf = x.reshape(n, d)
    h = _rmsnorm(xf, norm_g[0])
    ssm_ops = _ssm_weights(a_re, a_im, log_dt, b_re, b_im, c_re, c_im)
    for l in range(depth):
        a_in, sga, sgb = _in_proj_a(h, w_in, conv_w, l, seq)
        u, szb = _in_proj_b(h, w_in, l)
        y = _ssm(u.reshape(bsz, seq, SSM_WIDTH), *ssm_ops, l).reshape(n, SSM_WIDTH)
        m = _tail(y, u, szb, a_in, sga, sgb, d_skip, b_glu, w_glu, w_out_a, w_out_b, l)
        if l + 1 < depth:
            xf, h = _out_proj(m, w_o, l, xf, norm_g[l + 1], True)
        else:
            out = _out_proj(m, w_o, l, xf, final_g, False)
    return out.reshape(bsz, seq, d)
```

```python
import functools

import jax
import jax.numpy as jnp
from jax import lax
from jax.experimental import pallas as pl
from jax.experimental.pallas import tpu as pltpu

F32 = jnp.float32
BF16 = jnp.bfloat16

D_MODEL = 2048
CONV_WIDTH = D_MODEL
CONV_K = 3
SSM_WIDTH = D_MODEL // 2
SSM_GROUP = 16
RMS_EPS = 1e-6

OFF_V = 0
OFF_BG = CONV_WIDTH
OFF_CG = 2 * CONV_WIDTH
OFF_ZA = 3 * CONV_WIDTH
OFF_U = 4 * CONV_WIDTH
OFF_ZB = OFF_U + SSM_WIDTH
OFF_G = OFF_ZB + SSM_WIDTH

SSM_CHUNK = 16
CHUNK_LANES = SSM_CHUNK * SSM_GROUP
LANE_TILE = 128
SUBLANES = 8
STATE_LANES = LANE_TILE
GROUPS_PER_STEP = LANE_TILE // SSM_GROUP
PREP_ROWS = 24

V7X_VMEM_LIMIT = 56 * 1024 * 1024


def _sigmoid(x):
    return 0.5 * (jnp.tanh(0.5 * x) + 1.0)


def _params(*sem):
    return pltpu.CompilerParams(dimension_semantics=sem, vmem_limit_bytes=V7X_VMEM_LIMIT)


def _rmsnorm_kernel(x_ref, g_ref, o_ref):
    x = x_ref[...]
    ms = jnp.mean(x * x, axis=-1, keepdims=True)
    o_ref[...] = (x * lax.rsqrt(ms + RMS_EPS) * g_ref[...]).astype(o_ref.dtype)


def _rmsnorm(x, g, tm=1024):
    n, d = x.shape
    return pl.pallas_call(
        _rmsnorm_kernel,
        grid=(n // tm,),
        in_specs=[pl.BlockSpec((tm, d), lambda i: (i, 0)),
                  pl.BlockSpec((1, d), lambda i: (0, 0))],
        out_specs=pl.BlockSpec((tm, d), lambda i: (i, 0)),
        out_shape=jax.ShapeDtypeStruct((n, d), BF16),
        compiler_params=_params("arbitrary"),
        name="rmsnorm",
    )(x, g.reshape(1, d))


def _in_proj_a_kernel(h_ref, w_hbm, cw_ref, a_ref, sga_ref, sgb_ref,
                      carry_ref, w_ref, stage_ref, sem, *, layer, offsets, n_blocks, tiles_per_seq, sub):
    j = pl.program_id(0)
    i = pl.program_id(1)
    tn = a_ref.shape[1]

    def weight_copies(block):
        return [pltpu.make_async_copy(
            w_hbm.at[layer, :, pl.ds(pl.multiple_of(off + block * tn, tn), tn)], stage_ref.at[k], sem.at[k])
            for k, off in enumerate(offsets)]

    @pl.when(i == 0)
    def _():
        @pl.when(j == 0)
        def _():
            for c in weight_copies(0):
                c.start()

        for c in weight_copies(j):
            c.wait()
        for k in range(len(offsets)):
            w_ref[k] = stage_ref[k].astype(BF16)

        @pl.when(j + 1 < n_blocks)
        def _():
            for c in weight_copies(j + 1):
                c.start()

    @pl.when(i % tiles_per_seq == 0)
    def _():
        carry_ref[...] = jnp.zeros_like(carry_ref)

    cw = cw_ref[...]
    rs = h_ref.shape[0] // sub
    row = lax.broadcasted_iota(jnp.int32, (rs, tn), 0)
    tail = carry_ref[...]
    for r in range(sub):
        rows = slice(r * rs, (r + 1) * rs)
        h = h_ref[rows, :]
        v = jnp.dot(h, w_ref[0], preferred_element_type=F32)
        bg = jnp.dot(h, w_ref[1], preferred_element_type=F32)
        cg = jnp.dot(h, w_ref[2], preferred_element_type=F32)
        za = jnp.dot(h, w_ref[3], preferred_element_type=F32)
        cv = cg * v
        prev1 = tail[SUBLANES - 1:SUBLANES, :]
        prev2 = tail[SUBLANES - 2:SUBLANES - 1, :]
        cv1 = jnp.where(row == 0, prev1, pltpu.roll(cv, 1, axis=0))
        cv2 = jnp.where(row == 0, prev2, jnp.where(row == 1, prev1, pltpu.roll(cv, 2, axis=0)))
        conv = cw[0:1, :] * cv2 + cw[1:2, :] * cv1 + cw[2:3, :] * cv
        a_ref[rows, :] = (bg * conv * (za * _sigmoid(za))).astype(a_ref.dtype)
        tail = cv[rs - SUBLANES:rs, :]
        ga = jnp.dot(h, w_ref[4], preferred_element_type=F32)
        sga_ref[rows, :] = _sigmoid(ga).astype(sga_ref.dtype)
        gb = jnp.dot(h, w_ref[5], preferred_element_type=F32)
        sgb_ref[rows, :] = _sigmoid(gb).astype(sgb_ref.dtype)
    carry_ref[...] = tail


def _in_proj_a(h, w_in, conv_w, layer, seq_len, tm=2048, tn=256, sub=8):
    n, d = h.shape
    nj = CONV_WIDTH // tn
    offsets = (OFF_V, OFF_BG, OFF_CG, OFF_ZA, OFF_G, OFF_G + D_MODEL)
    out = pl.BlockSpec((tm, tn), lambda j, i: (i, j))
    shape = jax.ShapeDtypeStruct((n, CONV_WIDTH), BF16)
    return pl.pallas_call(
        functools.partial(_in_proj_a_kernel, layer=layer, offsets=offsets, n_blocks=nj,
                          tiles_per_seq=seq_len // tm, sub=sub),
        grid=(nj, n // tm),
        in_specs=[pl.BlockSpec((tm, d), lambda j, i: (i, 0)),
                  pl.BlockSpec(memory_space=pl.ANY),
                  pl.BlockSpec((None, CONV_K, tn), lambda j, i: (layer, 0, j))],
        out_specs=[out, out, out],
        out_shape=[shape, shape, shape],
        scratch_shapes=[pltpu.VMEM((SUBLANES, tn), F32), pltpu.VMEM((len(offsets), d, tn), BF16),
                        pltpu.VMEM((len(offsets), d, tn), F32), pltpu.SemaphoreType.DMA((len(offsets),))],
        compiler_params=_params("arbitrary", "arbitrary"),
        name="in_proj_conv_gates",
    )(h, w_in, conv_w)


def _in_proj_b_kernel(h_ref, wu_ref, wzb_ref, u_ref, szb_ref, w_ref, *, sub):
    @pl.when(pl.program_id(1) == 0)
    def _():
        w_ref[0] = wu_ref[...].astype(BF16)
        w_ref[1] = wzb_ref[...].astype(BF16)

    rs = h_ref.shape[0] // sub
    for r in range(sub):
        rows = slice(r * rs, (r + 1) * rs)
        h = h_ref[rows, :]
        u_ref[rows, :] = jnp.dot(h, w_ref[0], preferred_element_type=F32)
        zb = jnp.dot(h, w_ref[1], preferred_element_type=F32)
        szb_ref[rows, :] = (zb * _sigmoid(zb)).astype(szb_ref.dtype)


def _in_proj_b(h, w_in, layer, tm=1024, tn=512, sub=4):
    n, d = h.shape

    def wspec(off):
        return pl.BlockSpec((None, d, tn), lambda j, i, o=off // tn: (layer, 0, o + j))

    out = pl.BlockSpec((tm, tn), lambda j, i: (i, j))
    return pl.pallas_call(
        functools.partial(_in_proj_b_kernel, sub=sub),
        grid=(SSM_WIDTH // tn, n // tm),
        in_specs=[pl.BlockSpec((tm, d), lambda j, i: (i, 0)), wspec(OFF_U), wspec(OFF_ZB)],
        out_specs=[out, out],
        out_shape=[jax.ShapeDtypeStruct((n, SSM_WIDTH), F32), jax.ShapeDtypeStruct((n, SSM_WIDTH), BF16)],
        scratch_shapes=[pltpu.VMEM((2, d, tn), BF16)],
        compiler_params=_params("arbitrary", "arbitrary"),
        name="in_proj_ssm",
    )(h, w_in, w_in)


def _ssm_kernel(u_ref, m0_ref, win_ref, wout_ref, lre_ref, lim_ref, y_ref,
                t_ref, ub_ref, t2_ref, sre_ref, sim_ref, hre_ref, him_ref, *, n_chunks, batch):
    gb = m0_ref.shape[0]
    t = SSM_CHUNK
    for b in range(batch):
        for s in range(t):
            xt = u_ref[b, pl.ds(s, n_chunks, stride=t), :].astype(BF16).T
            for g in range(gb):
                row0 = g * CHUNK_LANES + s * SSM_GROUP
                t_ref[b, row0:row0 + SSM_GROUP, :] = xt[g * SSM_GROUP:(g + 1) * SSM_GROUP, :]
    for b in range(batch):
        for g in range(gb):
            ub_ref[g, b * n_chunks:(b + 1) * n_chunks, :] = t_ref[b, g * CHUNK_LANES:(g + 1) * CHUNK_LANES, :].T

    for g in range(gb):
        s = jnp.dot(ub_ref[g], win_ref[g], preferred_element_type=F32)
        for b in range(batch):
            sb = s[b * n_chunks:(b + 1) * n_chunks, :]
            sre_ref[g, pl.ds(b, n_chunks, stride=batch), :] = sb[:, :STATE_LANES]
            sim_ref[g, pl.ds(b, n_chunks, stride=batch), :] = sb[:, STATE_LANES:]
    lre = lre_ref[0]
    lim = lim_ref[0]
    rows = 2 * batch
    first = lax.broadcasted_iota(jnp.int32, (rows, gb * STATE_LANES), 0) < batch

    def advance(h_r, h_i, s_r, s_i):
        return lre * h_r - lim * h_i + s_r, lre * h_i + lim * h_r + s_i

    def slab(ref, r0):
        return jnp.concatenate([ref[g, pl.ds(r0, rows), :] for g in range(gb)], axis=1)

    def step(j, carry):
        h_r, h_i = carry
        r0 = pl.multiple_of(j * rows, rows)
        s_r = slab(sre_ref, r0)
        s_i = slab(sim_ref, r0)
        n_r, n_i = advance(h_r, h_i, s_r, s_i)
        h_r = jnp.where(first, h_r, pltpu.roll(n_r, batch, axis=0))
        h_i = jnp.where(first, h_i, pltpu.roll(n_i, batch, axis=0))
        for g in range(gb):
            hre_ref[g, pl.ds(r0, rows), :] = h_r[:, g * STATE_LANES:(g + 1) * STATE_LANES]
            him_ref[g, pl.ds(r0, rows), :] = h_i[:, g * STATE_LANES:(g + 1) * STATE_LANES]
        n_r, n_i = advance(h_r, h_i, s_r, s_i)
        return pltpu.roll(n_r, batch, axis=0), pltpu.roll(n_i, batch, axis=0)

    zero = jnp.zeros((rows, gb * STATE_LANES), F32)
    lax.fori_loop(0, n_chunks // 2, step, (zero, zero))

    for g in range(gb):
        h = jnp.concatenate(
            [jnp.concatenate([ref[g, pl.ds(b, n_chunks, stride=batch), :] for b in range(batch)], axis=0)
             for ref in (hre_ref, him_ref)], axis=1)
        y = jnp.dot(ub_ref[g], m0_ref[g], preferred_element_type=F32)
        y = y + jnp.dot(h.astype(BF16), wout_ref[g], preferred_element_type=F32)
        for b in range(batch):
            at = y[b * n_chunks:(b + 1) * n_chunks, :].T
            for tt in range(t):
                row0 = tt * LANE_TILE + g * SSM_GROUP
                t2_ref[b, row0:row0 + SSM_GROUP, :] = at[tt * SSM_GROUP:(tt + 1) * SSM_GROUP, :]
    for b in range(batch):
        for tt in range(t):
            y_ref[b, pl.ds(tt, n_chunks, stride=t), :] = t2_ref[b, tt * LANE_TILE:(tt + 1) * LANE_TILE, :].T


def _ssm(u_p, m0, w_in_state, w_out_state, lam_re, lam_im, layer):
    batch, seq, width = u_p.shape
    n_inst = batch * seq // SSM_CHUNK
    gb = GROUPS_PER_STEP
    assert gb * SSM_GROUP == LANE_TILE and n_inst // batch == LANE_TILE
    first = layer * (width // LANE_TILE)
    blk = pl.BlockSpec((batch, seq, LANE_TILE), lambda i: (0, 0, i))
    sq = pl.BlockSpec((gb, CHUNK_LANES, CHUNK_LANES), lambda i: (first + i, 0, 0))
    vec = pl.BlockSpec((1, 1, gb * STATE_LANES), lambda i: (first + i, 0, 0))
    return pl.pallas_call(
        functools.partial(_ssm_kernel, n_chunks=n_inst // batch, batch=batch),
        grid=(width // LANE_TILE,),
        in_specs=[blk, sq, sq, sq, vec, vec],
        out_specs=blk,
        out_shape=jax.ShapeDtypeStruct(u_p.shape, F32),
        scratch_shapes=[pltpu.VMEM((batch, gb * CHUNK_LANES, LANE_TILE), BF16),
                        pltpu.VMEM((gb, n_inst, CHUNK_LANES), BF16),
                        pltpu.VMEM((batch, gb * CHUNK_LANES, LANE_TILE), F32)]
                       + [pltpu.VMEM((gb, n_inst, STATE_LANES), F32) for _ in range(4)],
        compiler_params=_params("arbitrary"),
        name="s5_chunked_scan",
    )(u_p, m0, w_in_state, w_out_state, lam_re, lam_im)


def _split_bf16(x):
    hi = x.astype(BF16)
    return hi, (x - hi.astype(F32)).astype(BF16)


def _dot_nt_f32(a, b):
    dn = (((1,), (1,)), ((), ()))
    a_hi, a_lo = _split_bf16(a)
    b_hi, b_lo = _split_bf16(b)

    def d(x, y):
        return lax.dot_general(x, y, dn, preferred_element_type=F32)

    return d(a_hi, b_hi) + d(a_hi, b_lo) + d(a_lo, b_hi)


def _ssm_prep_kernel(are_ref, aim_ref, ldt_ref, btr_ref, bti_ref, cre_ref, cim_ref,
                     m0_ref, win_ref, wout_ref, lre_ref, lim_ref):
    gb = are_ref.shape[0]
    t = SSM_CHUNK
    tau = lax.broadcasted_iota(jnp.int32, (PREP_ROWS, STATE_LANES), 0).astype(F32)
    lane = lax.broadcasted_iota(jnp.int32, (SSM_GROUP, CHUNK_LANES), 1)
    for g in range(gb):
        are = are_ref[g]
        aim = aim_ref[g]
        dt = jnp.exp(ldt_ref[g])
        mag = jnp.exp(tau * (dt * are))
        ang = tau * (dt * aim)
        pw_re = mag * jnp.cos(ang)
        pw_im = mag * jnp.sin(ang)
        nr = pw_re[1:2] - 1.0
        ni = pw_im[1:2]
        den = are * are + aim * aim
        cr = (nr * are + ni * aim) / den
        ci = (ni * are - nr * aim) / den
        btr = btr_ref[g]
        bti = bti_ref[g]
        bb_re = cr * btr - ci * bti
        bb_im = cr * bti + ci * btr
        cre = cre_ref[g]
        cim = cim_ref[g]

        win_re, win_im = [], []
        for s in range(t):
            pr = pw_re[t - 1 - s:t - s]
            pi = pw_im[t - 1 - s:t - s]
            win_re.append(pr * bb_re - pi * bb_im)
            win_im.append(pr * bb_im + pi * bb_re)
        win_ref[g] = jnp.concatenate(
            [jnp.concatenate(win_re, axis=0), jnp.concatenate(win_im, axis=0)], axis=1).astype(BF16)

        z_re, z_im = [], []
        for k in range(t + 1):
            pr = pw_re[k:k + 1]
            pi = pw_im[k:k + 1]
            z_re.append(cre * pr - cim * pi)
            z_im.append(cre * pi + cim * pr)

        zo_re = jnp.concatenate(z_re[1:], axis=0)
        zo_im = jnp.concatenate(z_im[1:], axis=0)
        wout_ref[g] = jnp.concatenate([zo_re.T, -(zo_im.T)], axis=0).astype(BF16)

        zk_re = jnp.concatenate(z_re[:t], axis=0)
        zk_im = jnp.concatenate(z_im[:t], axis=0)
        kt = _dot_nt_f32(bb_re, zk_re) - _dot_nt_f32(bb_im, zk_im)
        blocks = [kt]
        for s in range(1, t):
            shifted = pltpu.roll(kt, s * SSM_GROUP, axis=1)
            blocks.append(jnp.where(lane >= s * SSM_GROUP, shifted, 0.0))
        m0_ref[g] = jnp.concatenate(blocks, axis=0).astype(BF16)

        lre_ref[0, :, g * STATE_LANES:(g + 1) * STATE_LANES] = pw_re[t:t + 1]
        lim_ref[0, :, g * STATE_LANES:(g + 1) * STATE_LANES] = pw_im[t:t + 1]


def _ssm_weights(a_re, a_im, log_dt, b_re, b_im, c_re, c_im):
    a_re, a_im, b_re, b_im, c_re, c_im = (
        x.reshape((-1,) + x.shape[2:]) for x in (a_re, a_im, b_re, b_im, c_re, c_im))
    log_dt = log_dt.reshape(-1)
    g, p = a_re.shape
    gb = GROUPS_PER_STEP
    pad = STATE_LANES - p

    def lanes(x, value=0.0):
        return jnp.pad(x.astype(F32), [(0, 0)] * (x.ndim - 1) + [(0, pad)], constant_values=value)

    are = lanes(a_re, -1.0).reshape(g, 1, STATE_LANES)
    aim = lanes(a_im).reshape(g, 1, STATE_LANES)
    ldt = jnp.broadcast_to(log_dt.astype(F32)[:, None, None], (g, 1, STATE_LANES))
    btr = lanes(jnp.swapaxes(b_re, 1, 2))
    bti = lanes(jnp.swapaxes(b_im, 1, 2))
    cre = lanes(c_re)
    cim = lanes(c_im)
    vec = pl.BlockSpec((gb, 1, STATE_LANES), lambda i: (i, 0, 0))
    mat = pl.BlockSpec((gb, SSM_GROUP, STATE_LANES), lambda i: (i, 0, 0))
    sq = pl.BlockSpec((gb, CHUNK_LANES, CHUNK_LANES), lambda i: (i, 0, 0))
    row = pl.BlockSpec((1, 1, gb * STATE_LANES), lambda i: (i, 0, 0))
    sq_shape = jax.ShapeDtypeStruct((g, CHUNK_LANES, CHUNK_LANES), BF16)
    row_shape = jax.ShapeDtypeStruct((g // gb, 1, gb * STATE_LANES), F32)
    return pl.pallas_call(
        _ssm_prep_kernel,
        grid=(g // gb,),
        in_specs=[vec, vec, vec, mat, mat, mat, mat],
        out_specs=[sq, sq, sq, row, row],
        out_shape=[sq_shape, sq_shape, sq_shape, row_shape, row_shape],
        compiler_params=_params("arbitrary"),
        name="s5_chunk_operators",
    )(are, aim, ldt, btr, bti, cre, cim)


STAGE_ROWS = 512


def _tail_kernel(y_ref, u_ref, szb_ref, a_ref, sga_ref, sgb_ref, d_ref, bgl_ref,
                 wg32_ref, wa32_ref, wb32_ref, m_ref, wg_ref, wa_ref, wb_ref, *, sub):
    step = pl.program_id(0)
    stages = ((wg32_ref, wg_ref), (wa32_ref, wa_ref), (wb32_ref, wb_ref))
    first = 0
    for src, dst in stages:
        n_chunks = dst.shape[0] // STAGE_ROWS

        @pl.when((step >= first) & (step < first + n_chunks))
        def _(src=src, dst=dst, first=first):
            r0 = pl.multiple_of((step - first) * STAGE_ROWS, STAGE_ROWS)
            dst[pl.ds(r0, STAGE_ROWS), :] = src[...].astype(BF16)

        first += n_chunks

    @pl.when(step >= first)
    def _():
        rs = y_ref.shape[0] // sub
        for r in range(sub):
            rows = slice(r * rs, (r + 1) * rs)
            yb = jax.nn.gelu(y_ref[rows, :] + d_ref[...] * u_ref[rows, :])
            z = jnp.dot(yb.astype(BF16), wg_ref[...], preferred_element_type=F32) + bgl_ref[...]
            b = (yb * _sigmoid(z) * szb_ref[rows, :].astype(F32)).astype(BF16)
            ya = jnp.dot(a_ref[rows, :], wa_ref[...], preferred_element_type=F32)
            yb2 = jnp.dot(b, wb_ref[...], preferred_element_type=F32)
            m_ref[rows, :] = (sga_ref[rows, :].astype(F32) * ya
                              + sgb_ref[rows, :].astype(F32) * yb2).astype(m_ref.dtype)


def _tail(y, u, szb, a_in, sga, sgb, d_skip, b_glu, w_glu, w_a, w_b, layer, tm=256, sub=1):
    n, w = y.shape
    d = a_in.shape[1]
    counts = [rows // STAGE_ROWS for rows in (w, d, w)]
    starts = [sum(counts[:k]) for k in range(len(counts))]
    w_steps = sum(counts)

    def tile(width):
        return pl.BlockSpec((tm, width), lambda s: (jnp.maximum(s - w_steps, 0), 0))

    def staged(k, width):
        return pl.BlockSpec((None, STAGE_ROWS, width),
                            lambda s: (layer, jnp.clip(s - starts[k], 0, counts[k] - 1), 0))

    vec = pl.BlockSpec((None, 1, w), lambda s: (layer, 0, 0))
    return pl.pallas_call(
        functools.partial(_tail_kernel, sub=sub),
        grid=(w_steps + n // tm,),
        in_specs=[tile(w), tile(w), tile(w), tile(d), tile(d), tile(d), vec, vec,
                  staged(0, w), staged(1, d), staged(2, d)],
        out_specs=tile(d),
        out_shape=jax.ShapeDtypeStruct((n, d), BF16),
        scratch_shapes=[pltpu.VMEM((w, w), BF16), pltpu.VMEM((d, d), BF16), pltpu.VMEM((w, d), BF16)],
        compiler_params=_params("arbitrary"),
        name="glu_gated_merge",
    )(y, u, szb, a_in, sga, sgb, d_skip.reshape(-1, 1, w), b_glu.reshape(-1, 1, w), w_glu, w_a, w_b)


def _out_kernel(m_ref, wo32_ref, x_ref, g_ref, *refs, emit_x, sub, w_steps):
    o_refs, wo_ref = refs[:-1], refs[-1]
    step = pl.program_id(0)
    wr = wo32_ref.shape[0]

    @pl.when(step < w_steps)
    def _():
        r0 = pl.multiple_of(step * wr, wr)
        wo_ref[pl.ds(r0, wr), :] = wo32_ref[...].astype(BF16)

    @pl.when(step >= w_steps)
    def _():
        rs = m_ref.shape[0] // sub
        for r in range(sub):
            rows = slice(r * rs, (r + 1) * rs)
            xn = x_ref[rows, :] + jnp.dot(m_ref[rows, :], wo_ref[...], preferred_element_type=F32)
            ms = jnp.mean(xn * xn, axis=-1, keepdims=True)
            normed = xn * lax.rsqrt(ms + RMS_EPS) * g_ref[...]
            if emit_x:
                o_refs[0][rows, :] = xn
                o_refs[1][rows, :] = normed.astype(o_refs[1].dtype)
            else:
                o_refs[0][rows, :] = normed.astype(o_refs[0].dtype)


def _out_proj(m, w_o, layer, x, g, emit_x, tm=512, sub=2, w_steps=2):
    n, d = x.shape
    row = pl.BlockSpec((tm, d), lambda s: (jnp.maximum(s - w_steps, 0), 0))
    if emit_x:
        out_specs = [row, row]
        out_shape = [jax.ShapeDtypeStruct((n, d), F32), jax.ShapeDtypeStruct((n, d), BF16)]
    else:
        out_specs = row
        out_shape = jax.ShapeDtypeStruct((n, d), F32)
    w_spec = pl.BlockSpec((None, d // w_steps, d), lambda s: (layer, jnp.minimum(s, w_steps - 1), 0))
    return pl.pallas_call(
        functools.partial(_out_kernel, emit_x=emit_x, sub=sub, w_steps=w_steps),
        grid=(w_steps + n // tm,),
        in_specs=[row, w_spec, row, pl.BlockSpec((1, d), lambda s: (0, 0))],
        out_specs=out_specs,
        out_shape=out_shape,
        scratch_shapes=[pltpu.VMEM((d, d), BF16)],
        compiler_params=_params("arbitrary"),
        name="out_proj_residual_norm",
    )(m, w_o, x, g.reshape(1, d))


def kernel(x, norm_g, w_in, conv_w, w_out_a, a_re, a_im, log_dt, b_re, b_im, c_re, c_im,
           d_skip, w_glu, b_glu, w_out_b, w_o, final_g):
    bsz, seq, d = x.shape
    depth = norm_g.shape[0]
    n = bsz * seq
    n_chunks = seq // SSM_CHUNK
    assert 2 * bsz == SUBLANES and n_chunks % 2 == 0
    xf = x.reshape(n, d)
    h = _rmsnorm(xf, norm_g[0])
    ssm_ops = _ssm_weights(a_re, a_im, log_dt, b_re, b_im, c_re, c_im)
    for l in range(depth):
        a_in, sga, sgb = _in_proj_a(h, w_in, conv_w, l, seq)
        u, szb = _in_proj_b(h, w_in, l)
        y = _ssm(u.reshape(bsz, seq, SSM_WIDTH), *ssm_ops, l).reshape(n, SSM_WIDTH)
        m = _tail(y, u, szb, a_in, sga, sgb, d_skip, b_glu, w_glu, w_out_a, w_out_b, l)
        if l + 1 < depth:
            xf, h = _out_proj(m, w_o, l, xf, norm_g[l + 1], True)
        else:
            out = _out_proj(m, w_o, l, xf, final_g, False)
    return out.reshape(bsz, seq, d)
```

```python
import functools

import jax
import jax.numpy as jnp
from jax import lax
from jax.experimental import pallas as pl
from jax.experimental.pallas import tpu as pltpu

F32 = jnp.float32
BF16 = jnp.bfloat16

D_MODEL = 2048
CONV_WIDTH = D_MODEL
CONV_K = 3
SSM_WIDTH = D_MODEL // 2
SSM_GROUP = 16
RMS_EPS = 1e-6

OFF_V = 0
OFF_BG = CONV_WIDTH
OFF_CG = 2 * CONV_WIDTH
OFF_ZA = 3 * CONV_WIDTH
OFF_U = 4 * CONV_WIDTH
OFF_ZB = OFF_U + SSM_WIDTH
OFF_G = OFF_ZB + SSM_WIDTH

SSM_CHUNK = 16
CHUNK_LANES = SSM_CHUNK * SSM_GROUP
LANE_TILE = 128
SUBLANES = 8
STATE_LANES = LANE_TILE
GROUPS_PER_STEP = LANE_TILE // SSM_GROUP
PREP_ROWS = 24

V7X_VMEM_LIMIT = 56 * 1024 * 1024


def _sigmoid(x):
    return 0.5 * (jnp.tanh(0.5 * x) + 1.0)


def _params(*sem):
    return pltpu.CompilerParams(dimension_semantics=sem, vmem_limit_bytes=V7X_VMEM_LIMIT)


def _rmsnorm_kernel(x_ref, g_ref, o_ref):
    x = x_ref[...]
    ms = jnp.mean(x * x, axis=-1, keepdims=True)
    o_ref[...] = (x * lax.rsqrt(ms + RMS_EPS) * g_ref[...]).astype(o_ref.dtype)


def _rmsnorm(x, g, tm=1024):
    n, d = x.shape
    return pl.pallas_call(
        _rmsnorm_kernel,
        grid=(n // tm,),
        in_specs=[pl.BlockSpec((tm, d), lambda i: (i, 0)),
                  pl.BlockSpec((1, d), lambda i: (0, 0))],
        out_specs=pl.BlockSpec((tm, d), lambda i: (i, 0)),
        out_shape=jax.ShapeDtypeStruct((n, d), BF16),
        compiler_params=_params("arbitrary"),
        name="rmsnorm",
    )(x, g.reshape(1, d))


def _in_proj_a_kernel(h_ref, w_hbm, cw_ref, a_ref, sga_ref, sgb_ref,
                      carry_ref, w_ref, stage_ref, sem, *, layer, offsets, n_blocks, tiles_per_seq, sub):
    j = pl.program_id(0)
    i = pl.program_id(1)
    tn = a_ref.shape[1]

    def weight_copies(block):
        return [pltpu.make_async_copy(
            w_hbm.at[layer, :, pl.ds(pl.multiple_of(off + block * tn, tn), tn)], stage_ref.at[k], sem.at[k])
            for k, off in enumerate(offsets)]

    @pl.when(i == 0)
    def _():
        @pl.when(j == 0)
        def _():
            for c in weight_copies(0):
                c.start()

        for c in weight_copies(j):
            c.wait()
        for k in range(len(offsets)):
            w_ref[k] = stage_ref[k].astype(BF16)

        @pl.when(j + 1 < n_blocks)
        def _():
            for c in weight_copies(j + 1):
                c.start()

    @pl.when(i % tiles_per_seq == 0)
    def _():
        carry_ref[...] = jnp.zeros_like(carry_ref)

    cw = cw_ref[...]
    rs = h_ref.shape[0] // sub
    row = lax.broadcasted_iota(jnp.int32, (rs, tn), 0)
    tail = carry_ref[...]
    for r in range(sub):
        rows = slice(r * rs, (r + 1) * rs)
        h = h_ref[rows, :]
        v = jnp.dot(h, w_ref[0], preferred_element_type=F32)
        bg = jnp.dot(h, w_ref[1], preferred_element_type=F32)
        cg = jnp.dot(h, w_ref[2], preferred_element_type=F32)
        za = jnp.dot(h, w_ref[3], preferred_element_type=F32)
        cv = cg * v
        prev1 = tail[SUBLANES - 1:SUBLANES, :]
        prev2 = tail[SUBLANES - 2:SUBLANES - 1, :]
        cv1 = jnp.where(row == 0, prev1, pltpu.roll(cv, 1, axis=0))
        cv2 = jnp.where(row == 0, prev2, jnp.where(row == 1, prev1, pltpu.roll(cv, 2, axis=0)))
        conv = cw[0:1, :] * cv2 + cw[1:2, :] * cv1 + cw[2:3, :] * cv
        a_ref[rows, :] = (bg * conv * (za * _sigmoid(za))).astype(a_ref.dtype)
        tail = cv[rs - SUBLANES:rs, :]
        ga = jnp.dot(h, w_ref[4], preferred_element_type=F32)
        sga_ref[rows, :] = _sigmoid(ga).astype(sga_ref.dtype)
        gb = jnp.dot(h, w_ref[5], preferred_element_type=F32)
        sgb_ref[rows, :] = _sigmoid(gb).astype(sgb_ref.dtype)
    carry_ref[...] = tail


def _in_proj_a(h, w_in, conv_w, layer, seq_len, tm=2048, tn=256, sub=8):
    n, d = h.shape
    nj = CONV_WIDTH // tn
    offsets = (OFF_V, OFF_BG, OFF_CG, OFF_ZA, OFF_G, OFF_G + D_MODEL)
    out = pl.BlockSpec((tm, tn), lambda j, i: (i, j))
    shape = jax.ShapeDtypeStruct((n, CONV_WIDTH), BF16)
    return pl.pallas_call(
        functools.partial(_in_proj_a_kernel, layer=layer, offsets=offsets, n_blocks=nj,
                          tiles_per_seq=seq_len // tm, sub=sub),
        grid=(nj, n // tm),
        in_specs=[pl.BlockSpec((tm, d), lambda j, i: (i, 0)),
                  pl.BlockSpec(memory_space=pl.ANY),
                  pl.BlockSpec((None, CONV_K, tn), lambda j, i: (layer, 0, j))],
        out_specs=[out, out, out],
        out_shape=[shape, shape, shape],
        scratch_shapes=[pltpu.VMEM((SUBLANES, tn), F32), pltpu.VMEM((len(offsets), d, tn), BF16),
                        pltpu.VMEM((len(offsets), d, tn), F32), pltpu.SemaphoreType.DMA((len(offsets),))],
        compiler_params=_params("arbitrary", "arbitrary"),
        name="in_proj_conv_gates",
    )(h, w_in, conv_w)


def _in_proj_b_kernel(h_ref, wu_ref, wzb_ref, u_ref, szb_ref, w_ref, *, sub):
    @pl.when(pl.program_id(1) == 0)
    def _():
        w_ref[0] = wu_ref[...].astype(BF16)
        w_ref[1] = wzb_ref[...].astype(BF16)

    rs = h_ref.shape[0] // sub
    for r in range(sub):
        rows = slice(r * rs, (r + 1) * rs)
        h = h_ref[rows, :]
        u_ref[rows, :] = jnp.dot(h, w_ref[0], preferred_element_type=F32)
        zb = jnp.dot(h, w_ref[1], preferred_element_type=F32)
        szb_ref[rows, :] = (zb * _sigmoid(zb)).astype(szb_ref.dtype)


def _in_proj_b(h, w_in, layer, tm=1024, tn=512, sub=4):
    n, d = h.shape

    def wspec(off):
        return pl.BlockSpec((None, d, tn), lambda j, i, o=off // tn: (layer, 0, o + j))

    out = pl.BlockSpec((tm, tn), lambda j, i: (i, j))
    return pl.pallas_call(
        functools.partial(_in_proj_b_kernel, sub=sub),
        grid=(SSM_WIDTH // tn, n // tm),
        in_specs=[pl.BlockSpec((tm, d), lambda j, i: (i, 0)), wspec(OFF_U), wspec(OFF_ZB)],
        out_specs=[out, out],
        out_shape=[jax.ShapeDtypeStruct((n, SSM_WIDTH), F32), jax.ShapeDtypeStruct((n, SSM_WIDTH), BF16)],
        scratch_shapes=[pltpu.VMEM((2, d, tn), BF16)],
        compiler_params=_params("arbitrary", "arbitrary"),
        name="in_proj_ssm",
    )(h, w_in, w_in)


def _ssm_kernel(u_ref, m0_ref, win_ref, wout_ref, lre_ref, lim_ref, y_ref,
                t_ref, ub_ref, t2_ref, sre_ref, sim_ref, hre_ref, him_ref, *, n_chunks, batch):
    gb = m0_ref.shape[0]
    t = SSM_CHUNK
    for b in range(batch):
        for s in range(t):
            xt = u_ref[b, pl.ds(s, n_chunks, stride=t), :].astype(BF16).T
            for g in range(gb):
                row0 = g * CHUNK_LANES + s * SSM_GROUP
                t_ref[b, row0:row0 + SSM_GROUP, :] = xt[g * SSM_GROUP:(g + 1) * SSM_GROUP, :]
    for b in range(batch):
        for g in range(gb):
            ub_ref[g, b * n_chunks:(b + 1) * n_chunks, :] = t_ref[b, g * CHUNK_LANES:(g + 1) * CHUNK_LANES, :].T

    for g in range(gb):
        s = jnp.dot(ub_ref[g], win_ref[g], preferred_element_type=F32)
        for b in range(batch):
            sb = s[b * n_chunks:(b + 1) * n_chunks, :]
            sre_ref[g, pl.ds(b, n_chunks, stride=batch), :] = sb[:, :STATE_LANES]
            sim_ref[g, pl.ds(b, n_chunks, stride=batch), :] = sb[:, STATE_LANES:]
    lre = jnp.stack([lre_ref[0, :, g * STATE_LANES:(g + 1) * STATE_LANES] for g in range(gb)], axis=0)
    lim = jnp.stack([lim_ref[0, :, g * STATE_LANES:(g + 1) * STATE_LANES] for g in range(gb)], axis=0)
    rows = 2 * batch
    first = lax.broadcasted_iota(jnp.int32, (gb, rows, STATE_LANES), 1) < batch

    def advance(h_r, h_i, s_r, s_i):
        return lre * h_r - lim * h_i + s_r, lre * h_i + lim * h_r + s_i

    def step(j, carry):
        h_r, h_i = carry
        r0 = pl.multiple_of(j * rows, rows)
        s_r = sre_ref[:, pl.ds(r0, rows), :]
        s_i = sim_ref[:, pl.ds(r0, rows), :]
        n_r, n_i = advance(h_r, h_i, s_r, s_i)
        h_r = jnp.where(first, h_r, pltpu.roll(n_r, batch, axis=1))
        h_i = jnp.where(first, h_i, pltpu.roll(n_i, batch, axis=1))
        hre_ref[:, pl.ds(r0, rows), :] = h_r
        him_ref[:, pl.ds(r0, rows), :] = h_i
        n_r, n_i = advance(h_r, h_i, s_r, s_i)
        return pltpu.roll(n_r, batch, axis=1), pltpu.roll(n_i, batch, axis=1)

    zero = jnp.zeros((gb, rows, STATE_LANES), F32)
    lax.fori_loop(0, n_chunks // 2, step, (zero, zero))

    for g in range(gb):
        h = jnp.concatenate(
            [jnp.concatenate([ref[g, pl.ds(b, n_chunks, stride=batch), :] for b in range(batch)], axis=0)
             for ref in (hre_ref, him_ref)], axis=1)
        y = jnp.dot(ub_ref[g], m0_ref[g], preferred_element_type=F32)
        y = y + jnp.dot(h.astype(BF16), wout_ref[g], preferred_element_type=F32)
        for b in range(batch):
            at = y[b * n_chunks:(b + 1) * n_chunks, :].T
            for tt in range(t):
                row0 = tt * LANE_TILE + g * SSM_GROUP
                t2_ref[b, row0:row0 + SSM_GROUP, :] = at[tt * SSM_GROUP:(tt + 1) * SSM_GROUP, :]
    for b in range(batch):
        for tt in range(t):
            y_ref[b, pl.ds(tt, n_chunks, stride=t), :] = t2_ref[b, tt * LANE_TILE:(tt + 1) * LANE_TILE, :].T


def _ssm(u_p, m0, w_in_state, w_out_state, lam_re, lam_im, layer):
    batch, seq, width = u_p.shape
    n_inst = batch * seq // SSM_CHUNK
    gb = GROUPS_PER_STEP
    assert gb * SSM_GROUP == LANE_TILE and n_inst // batch == LANE_TILE
    first = layer * (width // LANE_TILE)
    blk = pl.BlockSpec((batch, seq, LANE_TILE), lambda i: (0, 0, i))
    sq = pl.BlockSpec((gb, CHUNK_LANES, CHUNK_LANES), lambda i: (first + i, 0, 0))
    vec = pl.BlockSpec((1, 1, gb * STATE_LANES), lambda i: (first + i, 0, 0))
    return pl.pallas_call(
        functools.partial(_ssm_kernel, n_chunks=n_inst // batch, batch=batch),
        grid=(width // LANE_TILE,),
        in_specs=[blk, sq, sq, sq, vec, vec],
        out_specs=blk,
        out_shape=jax.ShapeDtypeStruct(u_p.shape, F32),
        scratch_shapes=[pltpu.VMEM((batch, gb * CHUNK_LANES, LANE_TILE), BF16),
                        pltpu.VMEM((gb, n_inst, CHUNK_LANES), BF16),
                        pltpu.VMEM((batch, gb * CHUNK_LANES, LANE_TILE), F32)]
                       + [pltpu.VMEM((gb, n_inst, STATE_LANES), F32) for _ in range(4)],
        compiler_params=_params("arbitrary"),
        name="s5_chunked_scan",
    )(u_p, m0, w_in_state, w_out_state, lam_re, lam_im)


def _split_bf16(x):
    hi = x.astype(BF16)
    return hi, (x - hi.astype(F32)).astype(BF16)


def _dot_nt_f32(a, b):
    dn = (((1,), (1,)), ((), ()))
    a_hi, a_lo = _split_bf16(a)
    b_hi, b_lo = _split_bf16(b)

    def d(x, y):
        return lax.dot_general(x, y, dn, preferred_element_type=F32)

    return d(a_hi, b_hi) + d(a_hi, b_lo) + d(a_lo, b_hi)


def _ssm_prep_kernel(vec_ref, mat_ref, m0_ref, win_ref, wout_ref, lre_ref, lim_ref, *, n_state):
    gb = vec_ref.shape[0]
    t = SSM_CHUNK
    tau = lax.broadcasted_iota(jnp.int32, (PREP_ROWS, STATE_LANES), 0).astype(F32)
    lane = lax.broadcasted_iota(jnp.int32, (SSM_GROUP, CHUNK_LANES), 1)
    valid = lax.broadcasted_iota(jnp.int32, (1, STATE_LANES), 1) < n_state
    for g in range(gb):
        are = jnp.where(valid, vec_ref[g, 0:1, :], -1.0)
        aim = vec_ref[g, 1:2, :]
        dt = jnp.exp(vec_ref[g, 2:3, :])
        mag = jnp.exp(tau * (dt * are))
        ang = tau * (dt * aim)
        pw_re = mag * jnp.cos(ang)
        pw_im = mag * jnp.sin(ang)
        nr = pw_re[1:2] - 1.0
        ni = pw_im[1:2]
        den = are * are + aim * aim
        cr = (nr * are + ni * aim) / den
        ci = (ni * are - nr * aim) / den
        btr = mat_ref[g, 0]
        bti = mat_ref[g, 1]
        bb_re = cr * btr - ci * bti
        bb_im = cr * bti + ci * btr
        cre = mat_ref[g, 2]
        cim = mat_ref[g, 3]

        win_re, win_im = [], []
        for s in range(t):
            pr = pw_re[t - 1 - s:t - s]
            pi = pw_im[t - 1 - s:t - s]
            win_re.append(pr * bb_re - pi * bb_im)
            win_im.append(pr * bb_im + pi * bb_re)
        win_ref[g] = jnp.concatenate(
            [jnp.concatenate(win_re, axis=0), jnp.concatenate(win_im, axis=0)], axis=1).astype(BF16)

        z_re, z_im = [], []
        for k in range(t + 1):
            pr = pw_re[k:k + 1]
            pi = pw_im[k:k + 1]
            z_re.append(cre * pr - cim * pi)
            z_im.append(cre * pi + cim * pr)

        zo_re = jnp.concatenate(z_re[1:], axis=0)
        zo_im = jnp.concatenate(z_im[1:], axis=0)
        wout_ref[g] = jnp.concatenate([zo_re.T, -(zo_im.T)], axis=0).astype(BF16)

        zk_re = jnp.concatenate(z_re[:t], axis=0)
        zk_im = jnp.concatenate(z_im[:t], axis=0)
        kt = _dot_nt_f32(bb_re, zk_re) - _dot_nt_f32(bb_im, zk_im)
        blocks = [kt]
        for s in range(1, t):
            shifted = pltpu.roll(kt, s * SSM_GROUP, axis=1)
            blocks.append(jnp.where(lane >= s * SSM_GROUP, shifted, 0.0))
        m0_ref[g] = jnp.concatenate(blocks, axis=0).astype(BF16)

        lre_ref[0, :, g * STATE_LANES:(g + 1) * STATE_LANES] = pw_re[t:t + 1]
        lim_ref[0, :, g * STATE_LANES:(g + 1) * STATE_LANES] = pw_im[t:t + 1]


def _ssm_weights(a_re, a_im, log_dt, b_re, b_im, c_re, c_im):
    a_re, a_im, b_re, b_im, c_re, c_im = (
        x.reshape((-1,) + x.shape[2:]) for x in (a_re, a_im, b_re, b_im, c_re, c_im))
    log_dt = log_dt.reshape(-1)
    g, p = a_re.shape
    gb = GROUPS_PER_STEP
    pad = STATE_LANES - p

    def lanes(x):
        return jnp.pad(x.astype(F32), [(0, 0)] * (x.ndim - 1) + [(0, pad)])

    vecs = lanes(jnp.stack([a_re, a_im, jnp.broadcast_to(log_dt[:, None], a_re.shape)], axis=1))
    mats = lanes(jnp.stack([jnp.swapaxes(b_re, 1, 2), jnp.swapaxes(b_im, 1, 2), c_re, c_im], axis=1))
    vec = pl.BlockSpec((gb, 3, STATE_LANES), lambda i: (i, 0, 0))
    mat = pl.BlockSpec((gb, 4, SSM_GROUP, STATE_LANES), lambda i: (i, 0, 0, 0))
    sq = pl.BlockSpec((gb, CHUNK_LANES, CHUNK_LANES), lambda i: (i, 0, 0))
    row = pl.BlockSpec((1, 1, gb * STATE_LANES), lambda i: (i, 0, 0))
    sq_shape = jax.ShapeDtypeStruct((g, CHUNK_LANES, CHUNK_LANES), BF16)
    row_shape = jax.ShapeDtypeStruct((g // gb, 1, gb * STATE_LANES), F32)
    return pl.pallas_call(
        functools.partial(_ssm_prep_kernel, n_state=p),
        grid=(g // gb,),
        in_specs=[vec, mat],
        out_specs=[sq, sq, sq, row, row],
        out_shape=[sq_shape, sq_shape, sq_shape, row_shape, row_shape],
        compiler_params=_params("arbitrary"),
        name="s5_chunk_operators",
    )(vecs, mats)


STAGE_ROWS = 512


def _tail_kernel(y_ref, u_ref, szb_ref, a_ref, sga_ref, sgb_ref, d_ref, bgl_ref,
                 wg32_ref, wa32_ref, wb32_ref, m_ref, wg_ref, wa_ref, wb_ref, *, sub):
    step = pl.program_id(0)
    stages = ((wg32_ref, wg_ref), (wa32_ref, wa_ref), (wb32_ref, wb_ref))
    first = 0
    for src, dst in stages:
        n_chunks = dst.shape[0] // STAGE_ROWS

        @pl.when((step >= first) & (step < first + n_chunks))
        def _(src=src, dst=dst, first=first):
            r0 = pl.multiple_of((step - first) * STAGE_ROWS, STAGE_ROWS)
            dst[pl.ds(r0, STAGE_ROWS), :] = src[...].astype(BF16)

        first += n_chunks

    @pl.when(step >= first)
    def _():
        rs = y_ref.shape[0] // sub
        for r in range(sub):
            rows = slice(r * rs, (r + 1) * rs)
            yb = jax.nn.gelu(y_ref[rows, :] + d_ref[...] * u_ref[rows, :])
            z = jnp.dot(yb.astype(BF16), wg_ref[...], preferred_element_type=F32) + bgl_ref[...]
            b = (yb * _sigmoid(z) * szb_ref[rows, :].astype(F32)).astype(BF16)
            ya = jnp.dot(a_ref[rows, :], wa_ref[...], preferred_element_type=F32)
            yb2 = jnp.dot(b, wb_ref[...], preferred_element_type=F32)
            m_ref[rows, :] = (sga_ref[rows, :].astype(F32) * ya
                              + sgb_ref[rows, :].astype(F32) * yb2).astype(m_ref.dtype)


def _tail(y, u, szb, a_in, sga, sgb, d_skip, b_glu, w_glu, w_a, w_b, layer, tm=256, sub=1):
    n, w = y.shape
    d = a_in.shape[1]
    counts = [rows // STAGE_ROWS for rows in (w, d, w)]
    starts = [sum(counts[:k]) for k in range(len(counts))]
    w_steps = sum(counts)

    def tile(width):
        return pl.BlockSpec((tm, width), lambda s: (jnp.maximum(s - w_steps, 0), 0))

    def staged(k, width):
        return pl.BlockSpec((None, STAGE_ROWS, width),
                            lambda s: (layer, jnp.clip(s - starts[k], 0, counts[k] - 1), 0))

    vec = pl.BlockSpec((None, 1, w), lambda s: (layer, 0, 0))
    return pl.pallas_call(
        functools.partial(_tail_kernel, sub=sub),
        grid=(w_steps + n // tm,),
        in_specs=[tile(w), tile(w), tile(w), tile(d), tile(d), tile(d), vec, vec,
                  staged(0, w), staged(1, d), staged(2, d)],
        out_specs=tile(d),
        out_shape=jax.ShapeDtypeStruct((n, d), BF16),
        scratch_shapes=[pltpu.VMEM((w, w), BF16), pltpu.VMEM((d, d), BF16), pltpu.VMEM((w, d), BF16)],
        compiler_params=_params("arbitrary"),
        name="glu_gated_merge",
    )(y, u, szb, a_in, sga, sgb, d_skip.reshape(-1, 1, w), b_glu.reshape(-1, 1, w), w_glu, w_a, w_b)


def _out_kernel(m_ref, wo32_ref, x_ref, g_ref, *refs, emit_x, sub, w_steps):
    o_refs, wo_ref = refs[:-1], refs[-1]
    step = pl.program_id(0)
    wr = wo32_ref.shape[0]

    @pl.when(step < w_steps)
    def _():
        r0 = pl.multiple_of(step * wr, wr)
        wo_ref[pl.ds(r0, wr), :] = wo32_ref[...].astype(BF16)

    @pl.when(step >= w_steps)
    def _():
        rs = m_ref.shape[0] // sub
        for r in range(sub):
            rows = slice(r * rs, (r + 1) * rs)
            xn = x_ref[rows, :] + jnp.dot(m_ref[rows, :], wo_ref[...], preferred_element_type=F32)
            ms = jnp.mean(xn * xn, axis=-1, keepdims=True)
            normed = xn * lax.rsqrt(ms + RMS_EPS) * g_ref[...]
            if emit_x:
                o_refs[0][rows, :] = xn
                o_refs[1][rows, :] = normed.astype(o_refs[1].dtype)
            else:
                o_refs[0][rows, :] = normed.astype(o_refs[0].dtype)


def _out_proj(m, w_o, layer, x, g, emit_x, tm=512, sub=2, w_steps=2):
    n, d = x.shape
    row = pl.BlockSpec((tm, d), lambda s: (jnp.maximum(s - w_steps, 0), 0))
    if emit_x:
        out_specs = [row, row]
        out_shape = [jax.ShapeDtypeStruct((n, d), F32), jax.ShapeDtypeStruct((n, d), BF16)]
    else:
        out_specs = row
        out_shape = jax.ShapeDtypeStruct((n, d), F32)
    w_spec = pl.BlockSpec((None, d // w_steps, d), lambda s: (layer, jnp.minimum(s, w_steps - 1), 0))
    return pl.pallas_call(
        functools.partial(_out_kernel, emit_x=emit_x, sub=sub, w_steps=w_steps),
        grid=(w_steps + n // tm,),
        in_specs=[row, w_spec, row, pl.BlockSpec((1, d), lambda s: (0, 0))],
        out_specs=out_specs,
        out_shape=out_shape,
        scratch_shapes=[pltpu.VMEM((d, d), BF16)],
        compiler_params=_params("arbitrary"),
        name="out_proj_residual_norm",
    )(m, w_o, x, g.reshape(1, d))


def kernel(x, norm_g, w_in, conv_w, w_out_a, a_re, a_im, log_dt, b_re, b_im, c_re, c_im,
           d_skip, w_glu, b_glu, w_out_b, w_o, final_g):
    bsz, seq, d = x.shape
    depth = norm_g.shape[0]
    n = bsz * seq
    n_chunks = seq // SSM_CHUNK
    assert 2 * bsz == SUBLANES and n_chunks % 2 == 0
    xf = x.reshape(n, d)
    h = _rmsnorm(xf, norm_g[0])
    ssm_ops = _ssm_weights(a_re, a_im, log_dt, b_re, b_im, c_re, c_im)
    for l in range(depth):
        a_in, sga, sgb = _in_proj_a(h, w_in, conv_w, l, seq)
        u, szb = _in_proj_b(h, w_in, l)
        y = _ssm(u.reshape(bsz, seq, SSM_WIDTH), *ssm_ops, l).reshape(n, SSM_WIDTH)
        m = _tail(y, u, szb, a_in, sga, sgb, d_skip, b_glu, w_glu, w_out_a, w_out_b, l)
        if l + 1 < depth:
            xf, h = _out_proj(m, w_o, l, xf, norm_g[l + 1], True)
        else:
            out = _out_proj(m, w_o, l, xf, final_g, False)
    return out.reshape(bsz, seq, d)
```

```python
import functools

import jax
import jax.numpy as jnp
from jax import lax
from jax.experimental import pallas as pl
from jax.experimental.pallas import tpu as pltpu

F32 = jnp.float32
BF16 = jnp.bfloat16

D_MODEL = 2048
CONV_WIDTH = D_MODEL
CONV_K = 3
SSM_WIDTH = D_MODEL // 2
SSM_GROUP = 16
RMS_EPS = 1e-6

OFF_V = 0
OFF_BG = CONV_WIDTH
OFF_CG = 2 * CONV_WIDTH
OFF_ZA = 3 * CONV_WIDTH
OFF_U = 4 * CONV_WIDTH
OFF_ZB = OFF_U + SSM_WIDTH
OFF_G = OFF_ZB + SSM_WIDTH

SSM_CHUNK = 16
CHUNK_LANES = SSM_CHUNK * SSM_GROUP
LANE_TILE = 128
SUBLANES = 8
STATE_LANES = LANE_TILE
GROUPS_PER_STEP = LANE_TILE // SSM_GROUP
PREP_ROWS = 24

V7X_VMEM_LIMIT = 56 * 1024 * 1024


def _sigmoid(x):
    return 0.5 * (jnp.tanh(0.5 * x) + 1.0)


def _params(*sem):
    return pltpu.CompilerParams(dimension_semantics=sem, vmem_limit_bytes=V7X_VMEM_LIMIT)


def _rmsnorm_kernel(x_ref, g_ref, o_ref):
    x = x_ref[...]
    ms = jnp.mean(x * x, axis=-1, keepdims=True)
    o_ref[...] = (x * lax.rsqrt(ms + RMS_EPS) * g_ref[...]).astype(o_ref.dtype)


def _rmsnorm(x, g, tm=1024):
    n, d = x.shape
    return pl.pallas_call(
        _rmsnorm_kernel,
        grid=(n // tm,),
        in_specs=[pl.BlockSpec((tm, d), lambda i: (i, 0)),
                  pl.BlockSpec((1, d), lambda i: (0, 0))],
        out_specs=pl.BlockSpec((tm, d), lambda i: (i, 0)),
        out_shape=jax.ShapeDtypeStruct((n, d), BF16),
        compiler_params=_params("arbitrary"),
        name="rmsnorm",
    )(x, g.reshape(1, d))


def _in_proj_a_kernel(h_ref, w_hbm, cw_ref, a_ref, sga_ref, sgb_ref,
                      carry_ref, w_ref, stage_ref, sem, *, layer, offsets, n_blocks, tiles_per_seq, sub):
    j = pl.program_id(0)
    i = pl.program_id(1)
    tn = a_ref.shape[1]

    def weight_copies(block):
        return [pltpu.make_async_copy(
            w_hbm.at[layer, :, pl.ds(pl.multiple_of(off + block * tn, tn), tn)], stage_ref.at[k], sem.at[k])
            for k, off in enumerate(offsets)]

    @pl.when(i == 0)
    def _():
        @pl.when(j == 0)
        def _():
            for c in weight_copies(0):
                c.start()

        for c in weight_copies(j):
            c.wait()
        for k in range(len(offsets)):
            w_ref[k] = stage_ref[k].astype(BF16)

        @pl.when(j + 1 < n_blocks)
        def _():
            for c in weight_copies(j + 1):
                c.start()

    @pl.when(i % tiles_per_seq == 0)
    def _():
        carry_ref[...] = jnp.zeros_like(carry_ref)

    cw = cw_ref[...]
    rs = h_ref.shape[0] // sub
    row = lax.broadcasted_iota(jnp.int32, (rs, tn), 0)
    tail = carry_ref[...]
    for r in range(sub):
        rows = slice(r * rs, (r + 1) * rs)
        h = h_ref[rows, :]
        v = jnp.dot(h, w_ref[0], preferred_element_type=F32)
        cg = jnp.dot(h, w_ref[2], preferred_element_type=F32)
        cv = cg * v
        za = jnp.dot(h, w_ref[3], preferred_element_type=F32)
        sza = za * _sigmoid(za)
        bg = jnp.dot(h, w_ref[1], preferred_element_type=F32)
        prev1 = tail[SUBLANES - 1:SUBLANES, :]
        prev2 = tail[SUBLANES - 2:SUBLANES - 1, :]
        cv1 = jnp.where(row == 0, prev1, pltpu.roll(cv, 1, axis=0))
        cv2 = jnp.where(row == 0, prev2, jnp.where(row == 1, prev1, pltpu.roll(cv, 2, axis=0)))
        conv = cw[0:1, :] * cv2 + cw[1:2, :] * cv1 + cw[2:3, :] * cv
        a_ref[rows, :] = (bg * conv * sza).astype(a_ref.dtype)
        tail = cv[rs - SUBLANES:rs, :]
        ga = jnp.dot(h, w_ref[4], preferred_element_type=F32)
        sga_ref[rows, :] = _sigmoid(ga).astype(sga_ref.dtype)
        gb = jnp.dot(h, w_ref[5], preferred_element_type=F32)
        sgb_ref[rows, :] = _sigmoid(gb).astype(sgb_ref.dtype)
    carry_ref[...] = tail


def _in_proj_a(h, w_in, conv_w, layer, seq_len, tm=2048, tn=256, sub=8):
    n, d = h.shape
    nj = CONV_WIDTH // tn
    offsets = (OFF_V, OFF_BG, OFF_CG, OFF_ZA, OFF_G, OFF_G + D_MODEL)
    out = pl.BlockSpec((tm, tn), lambda j, i: (i, j))
    shape = jax.ShapeDtypeStruct((n, CONV_WIDTH), BF16)
    return pl.pallas_call(
        functools.partial(_in_proj_a_kernel, layer=layer, offsets=offsets, n_blocks=nj,
                          tiles_per_seq=seq_len // tm, sub=sub),
        grid=(nj, n // tm),
        in_specs=[pl.BlockSpec((tm, d), lambda j, i: (i, 0)),
                  pl.BlockSpec(memory_space=pl.ANY),
                  pl.BlockSpec((None, CONV_K, tn), lambda j, i: (layer, 0, j))],
        out_specs=[out, out, out],
        out_shape=[shape, shape, shape],
        scratch_shapes=[pltpu.VMEM((SUBLANES, tn), F32), pltpu.VMEM((len(offsets), d, tn), BF16),
                        pltpu.VMEM((len(offsets), d, tn), F32), pltpu.SemaphoreType.DMA((len(offsets),))],
        compiler_params=_params("arbitrary", "arbitrary"),
        name="in_proj_conv_gates",
    )(h, w_in, conv_w)


def _in_proj_b_kernel(h_ref, wu_ref, wzb_ref, u_ref, szb_ref, w_ref, *, sub):
    @pl.when(pl.program_id(1) == 0)
    def _():
        w_ref[0] = wu_ref[...].astype(BF16)
        w_ref[1] = wzb_ref[...].astype(BF16)

    rs = h_ref.shape[0] // sub
    for r in range(sub):
        rows = slice(r * rs, (r + 1) * rs)
        h = h_ref[rows, :]
        u_ref[rows, :] = jnp.dot(h, w_ref[0], preferred_element_type=F32)
        zb = jnp.dot(h, w_ref[1], preferred_element_type=F32)
        szb_ref[rows, :] = (zb * _sigmoid(zb)).astype(szb_ref.dtype)


def _in_proj_b(h, w_in, layer, tm=1024, tn=512, sub=4):
    n, d = h.shape

    def wspec(off):
        return pl.BlockSpec((None, d, tn), lambda j, i, o=off // tn: (layer, 0, o + j))

    out = pl.BlockSpec((tm, tn), lambda j, i: (i, j))
    return pl.pallas_call(
        functools.partial(_in_proj_b_kernel, sub=sub),
        grid=(SSM_WIDTH // tn, n // tm),
        in_specs=[pl.BlockSpec((tm, d), lambda j, i: (i, 0)), wspec(OFF_U), wspec(OFF_ZB)],
        out_specs=[out, out],
        out_shape=[jax.ShapeDtypeStruct((n, SSM_WIDTH), F32), jax.ShapeDtypeStruct((n, SSM_WIDTH), BF16)],
        scratch_shapes=[pltpu.VMEM((2, d, tn), BF16)],
        compiler_params=_params("arbitrary", "arbitrary"),
        name="in_proj_ssm",
    )(h, w_in, w_in)


def _ssm_kernel(u_ref, m0_ref, win_ref, wout_ref, lre_ref, lim_ref, y_ref,
                t_ref, ub_ref, t2_ref, sre_ref, sim_ref, hre_ref, him_ref, *, n_chunks, batch):
    gb = m0_ref.shape[0]
    t = SSM_CHUNK
    for b in range(batch):
        for s in range(t):
            xt = u_ref[b, pl.ds(s, n_chunks, stride=t), :].astype(BF16).T
            for g in range(gb):
                row0 = g * CHUNK_LANES + s * SSM_GROUP
                t_ref[b, row0:row0 + SSM_GROUP, :] = xt[g * SSM_GROUP:(g + 1) * SSM_GROUP, :]
    for b in range(batch):
        for g in range(gb):
            ub_ref[g, b * n_chunks:(b + 1) * n_chunks, :] = t_ref[b, g * CHUNK_LANES:(g + 1) * CHUNK_LANES, :].T

    for g in range(gb):
        s = jnp.dot(ub_ref[g], win_ref[g], preferred_element_type=F32)
        for b in range(batch):
            sb = s[b * n_chunks:(b + 1) * n_chunks, :]
            sre_ref[g, pl.ds(b, n_chunks, stride=batch), :] = sb[:, :STATE_LANES]
            sim_ref[g, pl.ds(b, n_chunks, stride=batch), :] = sb[:, STATE_LANES:]
    lre = jnp.stack([lre_ref[0, :, g * STATE_LANES:(g + 1) * STATE_LANES] for g in range(gb)], axis=0)
    lim = jnp.stack([lim_ref[0, :, g * STATE_LANES:(g + 1) * STATE_LANES] for g in range(gb)], axis=0)
    rows = 2 * batch
    first = lax.broadcasted_iota(jnp.int32, (gb, rows, STATE_LANES), 1) < batch

    def advance(h_r, h_i, s_r, s_i):
        return lre * h_r - lim * h_i + s_r, lre * h_i + lim * h_r + s_i

    def step(j, carry):
        h_r, h_i = carry
        r0 = pl.multiple_of(j * rows, rows)
        s_r = sre_ref[:, pl.ds(r0, rows), :]
        s_i = sim_ref[:, pl.ds(r0, rows), :]
        n_r, n_i = advance(h_r, h_i, s_r, s_i)
        h_r = jnp.where(first, h_r, pltpu.roll(n_r, batch, axis=1))
        h_i = jnp.where(first, h_i, pltpu.roll(n_i, batch, axis=1))
        hre_ref[:, pl.ds(r0, rows), :] = h_r
        him_ref[:, pl.ds(r0, rows), :] = h_i
        n_r, n_i = advance(h_r, h_i, s_r, s_i)
        return pltpu.roll(n_r, batch, axis=1), pltpu.roll(n_i, batch, axis=1)

    zero = jnp.zeros((gb, rows, STATE_LANES), F32)
    lax.fori_loop(0, n_chunks // 2, step, (zero, zero))

    for g in range(gb):
        h = jnp.concatenate(
            [jnp.concatenate([ref[g, pl.ds(b, n_chunks, stride=batch), :] for b in range(batch)], axis=0)
             for ref in (hre_ref, him_ref)], axis=1)
        y = jnp.dot(ub_ref[g], m0_ref[g], preferred_element_type=F32)
        y = y + jnp.dot(h.astype(BF16), wout_ref[g], preferred_element_type=F32)
        for b in range(batch):
            at = y[b * n_chunks:(b + 1) * n_chunks, :].T
            for tt in range(t):
                row0 = tt * LANE_TILE + g * SSM_GROUP
                t2_ref[b, row0:row0 + SSM_GROUP, :] = at[tt * SSM_GROUP:(tt + 1) * SSM_GROUP, :]
    for b in range(batch):
        for tt in range(t):
            y_ref[b, pl.ds(tt, n_chunks, stride=t), :] = t2_ref[b, tt * LANE_TILE:(tt + 1) * LANE_TILE, :].T


def _ssm(u_p, m0, w_in_state, w_out_state, lam_re, lam_im, layer):
    batch, seq, width = u_p.shape
    n_inst = batch * seq // SSM_CHUNK
    gb = GROUPS_PER_STEP
    assert gb * SSM_GROUP == LANE_TILE and n_inst // batch == LANE_TILE
    first = layer * (width // LANE_TILE)
    blk = pl.BlockSpec((batch, seq, LANE_TILE), lambda i: (0, 0, i))
    sq = pl.BlockSpec((gb, CHUNK_LANES, CHUNK_LANES), lambda i: (first + i, 0, 0))
    vec = pl.BlockSpec((1, 1, gb * STATE_LANES), lambda i: (first + i, 0, 0))
    return pl.pallas_call(
        functools.partial(_ssm_kernel, n_chunks=n_inst // batch, batch=batch),
        grid=(width // LANE_TILE,),
        in_specs=[blk, sq, sq, sq, vec, vec],
        out_specs=blk,
        out_shape=jax.ShapeDtypeStruct(u_p.shape, F32),
        scratch_shapes=[pltpu.VMEM((batch, gb * CHUNK_LANES, LANE_TILE), BF16),
                        pltpu.VMEM((gb, n_inst, CHUNK_LANES), BF16),
                        pltpu.VMEM((batch, gb * CHUNK_LANES, LANE_TILE), F32)]
                       + [pltpu.VMEM((gb, n_inst, STATE_LANES), F32) for _ in range(4)],
        compiler_params=_params("arbitrary"),
        name="s5_chunked_scan",
    )(u_p, m0, w_in_state, w_out_state, lam_re, lam_im)


def _split_bf16(x):
    hi = x.astype(BF16)
    return hi, (x - hi.astype(F32)).astype(BF16)


def _dot_nt_f32(a, b):
    dn = (((1,), (1,)), ((), ()))
    a_hi, a_lo = _split_bf16(a)
    b_hi, b_lo = _split_bf16(b)

    def d(x, y):
        return lax.dot_general(x, y, dn, preferred_element_type=F32)

    return d(a_hi, b_hi) + d(a_hi, b_lo) + d(a_lo, b_hi)


def _ssm_prep_kernel(vec_ref, mat_ref, m0_ref, win_ref, wout_ref, lre_ref, lim_ref, *, n_state):
    gb = vec_ref.shape[0]
    t = SSM_CHUNK
    tau = lax.broadcasted_iota(jnp.int32, (PREP_ROWS, STATE_LANES), 0).astype(F32)
    lane = lax.broadcasted_iota(jnp.int32, (SSM_GROUP, CHUNK_LANES), 1)
    valid = lax.broadcasted_iota(jnp.int32, (1, STATE_LANES), 1) < n_state
    for g in range(gb):
        are = jnp.where(valid, vec_ref[g, 0:1, :], -1.0)
        aim = vec_ref[g, 1:2, :]
        dt = jnp.exp(vec_ref[g, 2:3, :])
        mag = jnp.exp(tau * (dt * are))
        ang = tau * (dt * aim)
        pw_re = mag * jnp.cos(ang)
        pw_im = mag * jnp.sin(ang)
        nr = pw_re[1:2] - 1.0
        ni = pw_im[1:2]
        den = are * are + aim * aim
        cr = (nr * are + ni * aim) / den
        ci = (ni * are - nr * aim) / den
        btr = mat_ref[g, 0]
        bti = mat_ref[g, 1]
        bb_re = cr * btr - ci * bti
        bb_im = cr * bti + ci * btr
        cre = mat_ref[g, 2]
        cim = mat_ref[g, 3]

        win_re, win_im = [], []
        for s in range(t):
            pr = pw_re[t - 1 - s:t - s]
            pi = pw_im[t - 1 - s:t - s]
            win_re.append(pr * bb_re - pi * bb_im)
            win_im.append(pr * bb_im + pi * bb_re)
        win_ref[g] = jnp.concatenate(
            [jnp.concatenate(win_re, axis=0), jnp.concatenate(win_im, axis=0)], axis=1).astype(BF16)

        z_re, z_im = [], []
        for k in range(t + 1):
            pr = pw_re[k:k + 1]
            pi = pw_im[k:k + 1]
            z_re.append(cre * pr - cim * pi)
            z_im.append(cre * pi + cim * pr)

        zo_re = jnp.concatenate(z_re[1:], axis=0)
        zo_im = jnp.concatenate(z_im[1:], axis=0)
        wout_ref[g] = jnp.concatenate([zo_re.T, -(zo_im.T)], axis=0).astype(BF16)

        zk_re = jnp.concatenate(z_re[:t], axis=0)
        zk_im = jnp.concatenate(z_im[:t], axis=0)
        kt = _dot_nt_f32(bb_re, zk_re) - _dot_nt_f32(bb_im, zk_im)
        blocks = [kt]
        for s in range(1, t):
            shifted = pltpu.roll(kt, s * SSM_GROUP, axis=1)
            blocks.append(jnp.where(lane >= s * SSM_GROUP, shifted, 0.0))
        m0_ref[g] = jnp.concatenate(blocks, axis=0).astype(BF16)

        lre_ref[0, :, g * STATE_LANES:(g + 1) * STATE_LANES] = pw_re[t:t + 1]
        lim_ref[0, :, g * STATE_LANES:(g + 1) * STATE_LANES] = pw_im[t:t + 1]


def _ssm_weights(a_re, a_im, log_dt, b_re, b_im, c_re, c_im):
    a_re, a_im, b_re, b_im, c_re, c_im = (
        x.reshape((-1,) + x.shape[2:]) for x in (a_re, a_im, b_re, b_im, c_re, c_im))
    log_dt = log_dt.reshape(-1)
    g, p = a_re.shape
    gb = GROUPS_PER_STEP
    pad = STATE_LANES - p

    def lanes(x):
        return jnp.pad(x.astype(F32), [(0, 0)] * (x.ndim - 1) + [(0, pad)])

    vecs = lanes(jnp.stack([a_re, a_im, jnp.broadcast_to(log_dt[:, None], a_re.shape)], axis=1))
    mats = lanes(jnp.stack([jnp.swapaxes(b_re, 1, 2), jnp.swapaxes(b_im, 1, 2), c_re, c_im], axis=1))
    vec = pl.BlockSpec((gb, 3, STATE_LANES), lambda i: (i, 0, 0))
    mat = pl.BlockSpec((gb, 4, SSM_GROUP, STATE_LANES), lambda i: (i, 0, 0, 0))
    sq = pl.BlockSpec((gb, CHUNK_LANES, CHUNK_LANES), lambda i: (i, 0, 0))
    row = pl.BlockSpec((1, 1, gb * STATE_LANES), lambda i: (i, 0, 0))
    sq_shape = jax.ShapeDtypeStruct((g, CHUNK_LANES, CHUNK_LANES), BF16)
    row_shape = jax.ShapeDtypeStruct((g // gb, 1, gb * STATE_LANES), F32)
    return pl.pallas_call(
        functools.partial(_ssm_prep_kernel, n_state=p),
        grid=(g // gb,),
        in_specs=[vec, mat],
        out_specs=[sq, sq, sq, row, row],
        out_shape=[sq_shape, sq_shape, sq_shape, row_shape, row_shape],
        compiler_params=_params("arbitrary"),
        name="s5_chunk_operators",
    )(vecs, mats)


STAGE_ROWS = 512


def _tail_kernel(y_ref, u_ref, szb_ref, a_ref, sga_ref, sgb_ref, d_ref, bgl_ref,
                 wg32_ref, wa32_ref, wb32_ref, m_ref, wg_ref, wa_ref, wb_ref, *, sub):
    step = pl.program_id(0)
    stages = ((wg32_ref, wg_ref), (wa32_ref, wa_ref), (wb32_ref, wb_ref))
    first = 0
    for src, dst in stages:
        n_chunks = dst.shape[0] // STAGE_ROWS

        @pl.when((step >= first) & (step < first + n_chunks))
        def _(src=src, dst=dst, first=first):
            r0 = pl.multiple_of((step - first) * STAGE_ROWS, STAGE_ROWS)
            dst[pl.ds(r0, STAGE_ROWS), :] = src[...].astype(BF16)

        first += n_chunks

    @pl.when(step >= first)
    def _():
        rs = y_ref.shape[0] // sub
        for r in range(sub):
            rows = slice(r * rs, (r + 1) * rs)
            yb = jax.nn.gelu(y_ref[rows, :] + d_ref[...] * u_ref[rows, :])
            z = jnp.dot(yb.astype(BF16), wg_ref[...], preferred_element_type=F32) + bgl_ref[...]
            b = (yb * _sigmoid(z) * szb_ref[rows, :].astype(F32)).astype(BF16)
            ya = jnp.dot(a_ref[rows, :], wa_ref[...], preferred_element_type=F32)
            yb2 = jnp.dot(b, wb_ref[...], preferred_element_type=F32)
            m_ref[rows, :] = (sga_ref[rows, :].astype(F32) * ya
                              + sgb_ref[rows, :].astype(F32) * yb2).astype(m_ref.dtype)


def _tail(y, u, szb, a_in, sga, sgb, d_skip, b_glu, w_glu, w_a, w_b, layer, tm=256, sub=1):
    n, w = y.shape
    d = a_in.shape[1]
    counts = [rows // STAGE_ROWS for rows in (w, d, w)]
    starts = [sum(counts[:k]) for k in range(len(counts))]
    w_steps = sum(counts)

    def tile(width):
        return pl.BlockSpec((tm, width), lambda s: (jnp.maximum(s - w_steps, 0), 0))

    def staged(k, width):
        return pl.BlockSpec((None, STAGE_ROWS, width),
                            lambda s: (layer, jnp.clip(s - starts[k], 0, counts[k] - 1), 0))

    vec = pl.BlockSpec((None, 1, w), lambda s: (layer, 0, 0))
    return pl.pallas_call(
        functools.partial(_tail_kernel, sub=sub),
        grid=(w_steps + n // tm,),
        in_specs=[tile(w), tile(w), tile(w), tile(d), tile(d), tile(d), vec, vec,
                  staged(0, w), staged(1, d), staged(2, d)],
        out_specs=tile(d),
        out_shape=jax.ShapeDtypeStruct((n, d), BF16),
        scratch_shapes=[pltpu.VMEM((w, w), BF16), pltpu.VMEM((d, d), BF16), pltpu.VMEM((w, d), BF16)],
        compiler_params=_params("arbitrary"),
        name="glu_gated_merge",
    )(y, u, szb, a_in, sga, sgb, d_skip.reshape(-1, 1, w), b_glu.reshape(-1, 1, w), w_glu, w_a, w_b)


def _out_kernel(m_ref, wo32_ref, x_ref, g_ref, *refs, emit_x, sub, w_steps):
    o_refs, wo_ref = refs[:-1], refs[-1]
    step = pl.program_id(0)
    wr = wo32_ref.shape[0]

    @pl.when(step < w_steps)
    def _():
        r0 = pl.multiple_of(step * wr, wr)
        wo_ref[pl.ds(r0, wr), :] = wo32_ref[...].astype(BF16)

    @pl.when(step >= w_steps)
    def _():
        rs = m_ref.shape[0] // sub
        for r in range(sub):
            rows = slice(r * rs, (r + 1) * rs)
            xn = x_ref[rows, :] + jnp.dot(m_ref[rows, :], wo_ref[...], preferred_element_type=F32)
            ms = jnp.mean(xn * xn, axis=-1, keepdims=True)
            normed = xn * lax.rsqrt(ms + RMS_EPS) * g_ref[...]
            if emit_x:
                o_refs[0][rows, :] = xn
                o_refs[1][rows, :] = normed.astype(o_refs[1].dtype)
            else:
                o_refs[0][rows, :] = normed.astype(o_refs[0].dtype)


def _out_proj(m, w_o, layer, x, g, emit_x, tm=512, sub=2, w_steps=2):
    n, d = x.shape
    row = pl.BlockSpec((tm, d), lambda s: (jnp.maximum(s - w_steps, 0), 0))
    if emit_x:
        out_specs = [row, row]
        out_shape = [jax.ShapeDtypeStruct((n, d), F32), jax.ShapeDtypeStruct((n, d), BF16)]
    else:
        out_specs = row
        out_shape = jax.ShapeDtypeStruct((n, d), F32)
    w_spec = pl.BlockSpec((None, d // w_steps, d), lambda s: (layer, jnp.minimum(s, w_steps - 1), 0))
    return pl.pallas_call(
        functools.partial(_out_kernel, emit_x=emit_x, sub=sub, w_steps=w_steps),
        grid=(w_steps + n // tm,),
        in_specs=[row, w_spec, row, pl.BlockSpec((1, d), lambda s: (0, 0))],
        out_specs=out_specs,
        out_shape=out_shape,
        scratch_shapes=[pltpu.VMEM((d, d), BF16)],
        compiler_params=_params("arbitrary"),
        name="out_proj_residual_norm",
    )(m, w_o, x, g.reshape(1, d))


def kernel(x, norm_g, w_in, conv_w, w_out_a, a_re, a_im, log_dt, b_re, b_im, c_re, c_im,
           d_skip, w_glu, b_glu, w_out_b, w_o, final_g):
    bsz, seq, d = x.shape
    depth = norm_g.shape[0]
    n = bsz * seq
    n_chunks = seq // SSM_CHUNK
    assert 2 * bsz == SUBLANES and n_chunks % 2 == 0
    xf = x.reshape(n, d)
    h = _rmsnorm(xf, norm_g[0])
    ssm_ops = _ssm_weights(a_re, a_im, log_dt, b_re, b_im, c_re, c_im)
    for l in range(depth):
        a_in, sga, sgb = _in_proj_a(h, w_in, conv_w, l, seq)
        u, szb = _in_proj_b(h, w_in, l)
        y = _ssm(u.reshape(bsz, seq, SSM_WIDTH), *ssm_ops, l).reshape(n, SSM_WIDTH)
        m = _tail(y, u, szb, a_in, sga, sgb, d_skip, b_glu, w_glu, w_out_a, w_out_b, l)
        if l + 1 < depth:
            xf, h = _out_proj(m, w_o, l, xf, norm_g[l + 1], True)
        else:
            out = _out_proj(m, w_o, l, xf, final_g, False)
    return out.reshape(bsz, seq, d)
```

```python
import functools

import jax
import jax.numpy as jnp
from jax import lax
from jax.experimental import pallas as pl
from jax.experimental.pallas import tpu as pltpu

F32 = jnp.float32
BF16 = jnp.bfloat16

D_MODEL = 2048
CONV_WIDTH = D_MODEL
CONV_K = 3
SSM_WIDTH = D_MODEL // 2
SSM_GROUP = 16
RMS_EPS = 1e-6

OFF_V = 0
OFF_BG = CONV_WIDTH
OFF_CG = 2 * CONV_WIDTH
OFF_ZA = 3 * CONV_WIDTH
OFF_U = 4 * CONV_WIDTH
OFF_ZB = OFF_U + SSM_WIDTH
OFF_G = OFF_ZB + SSM_WIDTH

SSM_CHUNK = 16
CHUNK_LANES = SSM_CHUNK * SSM_GROUP
LANE_TILE = 128
SUBLANES = 8
STATE_LANES = LANE_TILE
GROUPS_PER_STEP = LANE_TILE // SSM_GROUP
PREP_ROWS = 24

V7X_VMEM_LIMIT = 56 * 1024 * 1024


def _sigmoid(x):
    return 0.5 * (jnp.tanh(0.5 * x) + 1.0)


def _params(*sem):
    return pltpu.CompilerParams(dimension_semantics=sem, vmem_limit_bytes=V7X_VMEM_LIMIT)


def _rmsnorm_kernel(x_ref, g_ref, o_ref):
    x = x_ref[...]
    ms = jnp.mean(x * x, axis=-1, keepdims=True)
    o_ref[...] = (x * lax.rsqrt(ms + RMS_EPS) * g_ref[...]).astype(o_ref.dtype)


def _rmsnorm(x, g, tm=1024):
    n, d = x.shape
    return pl.pallas_call(
        _rmsnorm_kernel,
        grid=(n // tm,),
        in_specs=[pl.BlockSpec((tm, d), lambda i: (i, 0)),
                  pl.BlockSpec((1, d), lambda i: (0, 0))],
        out_specs=pl.BlockSpec((tm, d), lambda i: (i, 0)),
        out_shape=jax.ShapeDtypeStruct((n, d), BF16),
        compiler_params=_params("arbitrary"),
        name="rmsnorm",
    )(x, g.reshape(1, d))


def _in_proj_a_kernel(h_ref, w_hbm, cw_ref, a_ref, sga_ref, sgb_ref,
                      carry_ref, w_ref, stage_ref, sem, *, layer, offsets, n_blocks, tiles_per_seq, sub):
    j = pl.program_id(0)
    i = pl.program_id(1)
    tn = a_ref.shape[1]

    def weight_copies(block):
        return [pltpu.make_async_copy(
            w_hbm.at[layer, :, pl.ds(pl.multiple_of(off + block * tn, tn), tn)], stage_ref.at[k], sem.at[k])
            for k, off in enumerate(offsets)]

    @pl.when(i == 0)
    def _():
        @pl.when(j == 0)
        def _():
            for c in weight_copies(0):
                c.start()

        for c in weight_copies(j):
            c.wait()
        for k in range(len(offsets)):
            w_ref[k] = stage_ref[k].astype(BF16)

        @pl.when(j + 1 < n_blocks)
        def _():
            for c in weight_copies(j + 1):
                c.start()

    @pl.when(i % tiles_per_seq == 0)
    def _():
        carry_ref[...] = jnp.zeros_like(carry_ref)

    cw = cw_ref[...]
    rs = h_ref.shape[0] // sub
    row = lax.broadcasted_iota(jnp.int32, (rs, tn), 0)
    tail = carry_ref[...]
    for r in range(sub):
        rows = slice(r * rs, (r + 1) * rs)
        h = h_ref[rows, :]
        v = jnp.dot(h, w_ref[0], preferred_element_type=F32)
        cg = jnp.dot(h, w_ref[2], preferred_element_type=F32)
        cv = cg * v
        za = jnp.dot(h, w_ref[3], preferred_element_type=F32)
        sza = za * _sigmoid(za)
        bg = jnp.dot(h, w_ref[1], preferred_element_type=F32)
        prev1 = tail[SUBLANES - 1:SUBLANES, :]
        prev2 = tail[SUBLANES - 2:SUBLANES - 1, :]
        cv1 = jnp.where(row == 0, prev1, pltpu.roll(cv, 1, axis=0))
        cv2 = jnp.where(row == 0, prev2, jnp.where(row == 1, prev1, pltpu.roll(cv, 2, axis=0)))
        conv = cw[0:1, :] * cv2 + cw[1:2, :] * cv1 + cw[2:3, :] * cv
        a_ref[rows, :] = (bg * conv * sza).astype(a_ref.dtype)
        tail = cv[rs - SUBLANES:rs, :]
        ga = jnp.dot(h, w_ref[4], preferred_element_type=F32)
        sga_ref[rows, :] = _sigmoid(ga).astype(sga_ref.dtype)
        gb = jnp.dot(h, w_ref[5], preferred_element_type=F32)
        sgb_ref[rows, :] = _sigmoid(gb).astype(sgb_ref.dtype)
    carry_ref[...] = tail


def _in_proj_a(h, w_in, conv_w, layer, seq_len, tm=2048, tn=256, sub=16):
    n, d = h.shape
    nj = CONV_WIDTH // tn
    offsets = (OFF_V, OFF_BG, OFF_CG, OFF_ZA, OFF_G, OFF_G + D_MODEL)
    out = pl.BlockSpec((tm, tn), lambda j, i: (i, j))
    shape = jax.ShapeDtypeStruct((n, CONV_WIDTH), BF16)
    return pl.pallas_call(
        functools.partial(_in_proj_a_kernel, layer=layer, offsets=offsets, n_blocks=nj,
                          tiles_per_seq=seq_len // tm, sub=sub),
        grid=(nj, n // tm),
        in_specs=[pl.BlockSpec((tm, d), lambda j, i: (i, 0)),
                  pl.BlockSpec(memory_space=pl.ANY),
                  pl.BlockSpec((None, CONV_K, tn), lambda j, i: (layer, 0, j))],
        out_specs=[out, out, out],
        out_shape=[shape, shape, shape],
        scratch_shapes=[pltpu.VMEM((SUBLANES, tn), F32), pltpu.VMEM((len(offsets), d, tn), BF16),
                        pltpu.VMEM((len(offsets), d, tn), F32), pltpu.SemaphoreType.DMA((len(offsets),))],
        compiler_params=_params("arbitrary", "arbitrary"),
        name="in_proj_conv_gates",
    )(h, w_in, conv_w)


def _in_proj_b_kernel(h_ref, wu_ref, wzb_ref, u_ref, szb_ref, w_ref, *, sub):
    @pl.when(pl.program_id(1) == 0)
    def _():
        w_ref[0] = wu_ref[...].astype(BF16)
        w_ref[1] = wzb_ref[...].astype(BF16)

    rs = h_ref.shape[0] // sub
    for r in range(sub):
        rows = slice(r * rs, (r + 1) * rs)
        h = h_ref[rows, :]
        u_ref[rows, :] = jnp.dot(h, w_ref[0], preferred_element_type=F32)
        zb = jnp.dot(h, w_ref[1], preferred_element_type=F32)
        szb_ref[rows, :] = (zb * _sigmoid(zb)).astype(szb_ref.dtype)


def _in_proj_b(h, w_in, layer, tm=1024, tn=512, sub=4):
    n, d = h.shape

    def wspec(off):
        return pl.BlockSpec((None, d, tn), lambda j, i, o=off // tn: (layer, 0, o + j))

    out = pl.BlockSpec((tm, tn), lambda j, i: (i, j))
    return pl.pallas_call(
        functools.partial(_in_proj_b_kernel, sub=sub),
        grid=(SSM_WIDTH // tn, n // tm),
        in_specs=[pl.BlockSpec((tm, d), lambda j, i: (i, 0)), wspec(OFF_U), wspec(OFF_ZB)],
        out_specs=[out, out],
        out_shape=[jax.ShapeDtypeStruct((n, SSM_WIDTH), F32), jax.ShapeDtypeStruct((n, SSM_WIDTH), BF16)],
        scratch_shapes=[pltpu.VMEM((2, d, tn), BF16)],
        compiler_params=_params("arbitrary", "arbitrary"),
        name="in_proj_ssm",
    )(h, w_in, w_in)


def _ssm_kernel(u_ref, m0_ref, win_ref, wout_ref, lre_ref, lim_ref, y_ref,
                t_ref, ub_ref, t2_ref, sre_ref, sim_ref, hre_ref, him_ref, *, n_chunks, batch):
    gb = m0_ref.shape[0]
    t = SSM_CHUNK
    for b in range(batch):
        for s in range(t):
            xt = u_ref[b, pl.ds(s, n_chunks, stride=t), :].astype(BF16).T
            for g in range(gb):
                row0 = g * CHUNK_LANES + s * SSM_GROUP
                t_ref[b, row0:row0 + SSM_GROUP, :] = xt[g * SSM_GROUP:(g + 1) * SSM_GROUP, :]
    for b in range(batch):
        for g in range(gb):
            ub_ref[g, b * n_chunks:(b + 1) * n_chunks, :] = t_ref[b, g * CHUNK_LANES:(g + 1) * CHUNK_LANES, :].T

    for g in range(gb):
        s = jnp.dot(ub_ref[g], win_ref[g], preferred_element_type=F32)
        for b in range(batch):
            sb = s[b * n_chunks:(b + 1) * n_chunks, :]
            sre_ref[g, pl.ds(b, n_chunks, stride=batch), :] = sb[:, :STATE_LANES]
            sim_ref[g, pl.ds(b, n_chunks, stride=batch), :] = sb[:, STATE_LANES:]
    lre = jnp.stack([lre_ref[0, :, g * STATE_LANES:(g + 1) * STATE_LANES] for g in range(gb)], axis=0)
    lim = jnp.stack([lim_ref[0, :, g * STATE_LANES:(g + 1) * STATE_LANES] for g in range(gb)], axis=0)
    rows = 2 * batch
    first = lax.broadcasted_iota(jnp.int32, (gb, rows, STATE_LANES), 1) < batch

    def advance(h_r, h_i, s_r, s_i):
        return lre * h_r - lim * h_i + s_r, lre * h_i + lim * h_r + s_i

    def step(j, carry):
        h_r, h_i = carry
        r0 = pl.multiple_of(j * rows, rows)
        s_r = sre_ref[:, pl.ds(r0, rows), :]
        s_i = sim_ref[:, pl.ds(r0, rows), :]
        n_r, n_i = advance(h_r, h_i, s_r, s_i)
        h_r = jnp.where(first, h_r, pltpu.roll(n_r, batch, axis=1))
        h_i = jnp.where(first, h_i, pltpu.roll(n_i, batch, axis=1))
        hre_ref[:, pl.ds(r0, rows), :] = h_r
        him_ref[:, pl.ds(r0, rows), :] = h_i
        n_r, n_i = advance(h_r, h_i, s_r, s_i)
        return pltpu.roll(n_r, batch, axis=1), pltpu.roll(n_i, batch, axis=1)

    zero = jnp.zeros((gb, rows, STATE_LANES), F32)
    lax.fori_loop(0, n_chunks // 2, step, (zero, zero))

    for g in range(gb):
        h = jnp.concatenate(
            [jnp.concatenate([ref[g, pl.ds(b, n_chunks, stride=batch), :] for b in range(batch)], axis=0)
             for ref in (hre_ref, him_ref)], axis=1)
        y = jnp.dot(ub_ref[g], m0_ref[g], preferred_element_type=F32)
        y = y + jnp.dot(h.astype(BF16), wout_ref[g], preferred_element_type=F32)
        for b in range(batch):
            at = y[b * n_chunks:(b + 1) * n_chunks, :].T
            for tt in range(t):
                row0 = tt * LANE_TILE + g * SSM_GROUP
                t2_ref[b, row0:row0 + SSM_GROUP, :] = at[tt * SSM_GROUP:(tt + 1) * SSM_GROUP, :]
    for b in range(batch):
        for tt in range(t):
            y_ref[b, pl.ds(tt, n_chunks, stride=t), :] = t2_ref[b, tt * LANE_TILE:(tt + 1) * LANE_TILE, :].T


def _ssm(u_p, m0, w_in_state, w_out_state, lam_re, lam_im, layer):
    batch, seq, width = u_p.shape
    n_inst = batch * seq // SSM_CHUNK
    gb = GROUPS_PER_STEP
    assert gb * SSM_GROUP == LANE_TILE and n_inst // batch == LANE_TILE
    first = layer * (width // LANE_TILE)
    blk = pl.BlockSpec((batch, seq, LANE_TILE), lambda i: (0, 0, i))
    sq = pl.BlockSpec((gb, CHUNK_LANES, CHUNK_LANES), lambda i: (first + i, 0, 0))
    vec = pl.BlockSpec((1, 1, gb * STATE_LANES), lambda i: (first + i, 0, 0))
    return pl.pallas_call(
        functools.partial(_ssm_kernel, n_chunks=n_inst // batch, batch=batch),
        grid=(width // LANE_TILE,),
        in_specs=[blk, sq, sq, sq, vec, vec],
        out_specs=blk,
        out_shape=jax.ShapeDtypeStruct(u_p.shape, F32),
        scratch_shapes=[pltpu.VMEM((batch, gb * CHUNK_LANES, LANE_TILE), BF16),
                        pltpu.VMEM((gb, n_inst, CHUNK_LANES), BF16),
                        pltpu.VMEM((batch, gb * CHUNK_LANES, LANE_TILE), F32)]
                       + [pltpu.VMEM((gb, n_inst, STATE_LANES), F32) for _ in range(4)],
        compiler_params=_params("arbitrary"),
        name="s5_chunked_scan",
    )(u_p, m0, w_in_state, w_out_state, lam_re, lam_im)


def _split_bf16(x):
    hi = x.astype(BF16)
    return hi, (x - hi.astype(F32)).astype(BF16)


def _dot_nt_f32(a, b):
    dn = (((1,), (1,)), ((), ()))
    a_hi, a_lo = _split_bf16(a)
    b_hi, b_lo = _split_bf16(b)

    def d(x, y):
        return lax.dot_general(x, y, dn, preferred_element_type=F32)

    return d(a_hi, b_hi) + d(a_hi, b_lo) + d(a_lo, b_hi)


def _ssm_prep_kernel(vec_ref, mat_ref, m0_ref, win_ref, wout_ref, lre_ref, lim_ref, *, n_state):
    gb = vec_ref.shape[0]
    t = SSM_CHUNK
    tau = lax.broadcasted_iota(jnp.int32, (PREP_ROWS, STATE_LANES), 0).astype(F32)
    lane = lax.broadcasted_iota(jnp.int32, (SSM_GROUP, CHUNK_LANES), 1)
    valid = lax.broadcasted_iota(jnp.int32, (1, STATE_LANES), 1) < n_state
    for g in range(gb):
        are = jnp.where(valid, vec_ref[g, 0:1, :], -1.0)
        aim = vec_ref[g, 1:2, :]
        dt = jnp.exp(vec_ref[g, 2:3, :])
        mag = jnp.exp(tau * (dt * are))
        ang = tau * (dt * aim)
        pw_re = mag * jnp.cos(ang)
        pw_im = mag * jnp.sin(ang)
        nr = pw_re[1:2] - 1.0
        ni = pw_im[1:2]
        den = are * are + aim * aim
        cr = (nr * are + ni * aim) / den
        ci = (ni * are - nr * aim) / den
        btr = mat_ref[g, 0]
        bti = mat_ref[g, 1]
        bb_re = cr * btr - ci * bti
        bb_im = cr * bti + ci * btr
        cre = mat_ref[g, 2]
        cim = mat_ref[g, 3]

        win_re, win_im = [], []
        for s in range(t):
            pr = pw_re[t - 1 - s:t - s]
            pi = pw_im[t - 1 - s:t - s]
            win_re.append(pr * bb_re - pi * bb_im)
            win_im.append(pr * bb_im + pi * bb_re)
        win_ref[g] = jnp.concatenate(
            [jnp.concatenate(win_re, axis=0), jnp.concatenate(win_im, axis=0)], axis=1).astype(BF16)

        z_re, z_im = [], []
        for k in range(t + 1):
            pr = pw_re[k:k + 1]
            pi = pw_im[k:k + 1]
            z_re.append(cre * pr - cim * pi)
            z_im.append(cre * pi + cim * pr)

        zo_re = jnp.concatenate(z_re[1:], axis=0)
        zo_im = jnp.concatenate(z_im[1:], axis=0)
        wout_ref[g] = jnp.concatenate([zo_re.T, -(zo_im.T)], axis=0).astype(BF16)

        zk_re = jnp.concatenate(z_re[:t], axis=0)
        zk_im = jnp.concatenate(z_im[:t], axis=0)
        kt = _dot_nt_f32(bb_re, zk_re) - _dot_nt_f32(bb_im, zk_im)
        blocks = [kt]
        for s in range(1, t):
            shifted = pltpu.roll(kt, s * SSM_GROUP, axis=1)
            blocks.append(jnp.where(lane >= s * SSM_GROUP, shifted, 0.0))
        m0_ref[g] = jnp.concatenate(blocks, axis=0).astype(BF16)

        lre_ref[0, :, g * STATE_LANES:(g + 1) * STATE_LANES] = pw_re[t:t + 1]
        lim_ref[0, :, g * STATE_LANES:(g + 1) * STATE_LANES] = pw_im[t:t + 1]


def _ssm_weights(a_re, a_im, log_dt, b_re, b_im, c_re, c_im):
    a_re, a_im, b_re, b_im, c_re, c_im = (
        x.reshape((-1,) + x.shape[2:]) for x in (a_re, a_im, b_re, b_im, c_re, c_im))
    log_dt = log_dt.reshape(-1)
    g, p = a_re.shape
    gb = GROUPS_PER_STEP
    pad = STATE_LANES - p

    def lanes(x):
        return jnp.pad(x.astype(F32), [(0, 0)] * (x.ndim - 1) + [(0, pad)])

    vecs = lanes(jnp.stack([a_re, a_im, jnp.broadcast_to(log_dt[:, None], a_re.shape)], axis=1))
    mats = lanes(jnp.stack([jnp.swapaxes(b_re, 1, 2), jnp.swapaxes(b_im, 1, 2), c_re, c_im], axis=1))
    vec = pl.BlockSpec((gb, 3, STATE_LANES), lambda i: (i, 0, 0))
    mat = pl.BlockSpec((gb, 4, SSM_GROUP, STATE_LANES), lambda i: (i, 0, 0, 0))
    sq = pl.BlockSpec((gb, CHUNK_LANES, CHUNK_LANES), lambda i: (i, 0, 0))
    row = pl.BlockSpec((1, 1, gb * STATE_LANES), lambda i: (i, 0, 0))
    sq_shape = jax.ShapeDtypeStruct((g, CHUNK_LANES, CHUNK_LANES), BF16)
    row_shape = jax.ShapeDtypeStruct((g // gb, 1, gb * STATE_LANES), F32)
    return pl.pallas_call(
        functools.partial(_ssm_prep_kernel, n_state=p),
        grid=(g // gb,),
        in_specs=[vec, mat],
        out_specs=[sq, sq, sq, row, row],
        out_shape=[sq_shape, sq_shape, sq_shape, row_shape, row_shape],
        compiler_params=_params("arbitrary"),
        name="s5_chunk_operators",
    )(vecs, mats)


STAGE_ROWS = 512


def _tail_kernel(y_ref, u_ref, szb_ref, a_ref, sga_ref, sgb_ref, d_ref, bgl_ref,
                 wg32_ref, wa32_ref, wb32_ref, m_ref, wg_ref, wa_ref, wb_ref, *, sub):
    step = pl.program_id(0)
    stages = ((wg32_ref, wg_ref), (wa32_ref, wa_ref), (wb32_ref, wb_ref))
    first = 0
    for src, dst in stages:
        n_chunks = dst.shape[0] // STAGE_ROWS

        @pl.when((step >= first) & (step < first + n_chunks))
        def _(src=src, dst=dst, first=first):
            r0 = pl.multiple_of((step - first) * STAGE_ROWS, STAGE_ROWS)
            dst[pl.ds(r0, STAGE_ROWS), :] = src[...].astype(BF16)

        first += n_chunks

    @pl.when(step >= first)
    def _():
        rs = y_ref.shape[0] // sub
        for r in range(sub):
            rows = slice(r * rs, (r + 1) * rs)
            yb = jax.nn.gelu(y_ref[rows, :] + d_ref[...] * u_ref[rows, :])
            z = jnp.dot(yb.astype(BF16), wg_ref[...], preferred_element_type=F32) + bgl_ref[...]
            b = (yb * _sigmoid(z) * szb_ref[rows, :].astype(F32)).astype(BF16)
            ya = jnp.dot(a_ref[rows, :], wa_ref[...], preferred_element_type=F32)
            yb2 = jnp.dot(b, wb_ref[...], preferred_element_type=F32)
            m_ref[rows, :] = (sga_ref[rows, :].astype(F32) * ya
                              + sgb_ref[rows, :].astype(F32) * yb2).astype(m_ref.dtype)


def _tail(y, u, szb, a_in, sga, sgb, d_skip, b_glu, w_glu, w_a, w_b, layer, tm=256, sub=1):
    n, w = y.shape
    d = a_in.shape[1]
    counts = [rows // STAGE_ROWS for rows in (w, d, w)]
    starts = [sum(counts[:k]) for k in range(len(counts))]
    w_steps = sum(counts)

    def tile(width):
        return pl.BlockSpec((tm, width), lambda s: (jnp.maximum(s - w_steps, 0), 0))

    def staged(k, width):
        return pl.BlockSpec((None, STAGE_ROWS, width),
                            lambda s: (layer, jnp.clip(s - starts[k], 0, counts[k] - 1), 0))

    vec = pl.BlockSpec((None, 1, w), lambda s: (layer, 0, 0))
    return pl.pallas_call(
        functools.partial(_tail_kernel, sub=sub),
        grid=(w_steps + n // tm,),
        in_specs=[tile(w), tile(w), tile(w), tile(d), tile(d), tile(d), vec, vec,
                  staged(0, w), staged(1, d), staged(2, d)],
        out_specs=tile(d),
        out_shape=jax.ShapeDtypeStruct((n, d), BF16),
        scratch_shapes=[pltpu.VMEM((w, w), BF16), pltpu.VMEM((d, d), BF16), pltpu.VMEM((w, d), BF16)],
        compiler_params=_params("arbitrary"),
        name="glu_gated_merge",
    )(y, u, szb, a_in, sga, sgb, d_skip.reshape(-1, 1, w), b_glu.reshape(-1, 1, w), w_glu, w_a, w_b)


def _out_kernel(m_ref, wo32_ref, x_ref, g_ref, *refs, emit_x, sub, w_steps):
    o_refs, wo_ref = refs[:-1], refs[-1]
    step = pl.program_id(0)
    wr = wo32_ref.shape[0]

    @pl.when(step < w_steps)
    def _():
        r0 = pl.multiple_of(step * wr, wr)
        wo_ref[pl.ds(r0, wr), :] = wo32_ref[...].astype(BF16)

    @pl.when(step >= w_steps)
    def _():
        rs = m_ref.shape[0] // sub
        for r in range(sub):
            rows = slice(r * rs, (r + 1) * rs)
            xn = x_ref[rows, :] + jnp.dot(m_ref[rows, :], wo_ref[...], preferred_element_type=F32)
            ms = jnp.mean(xn * xn, axis=-1, keepdims=True)
            normed = xn * lax.rsqrt(ms + RMS_EPS) * g_ref[...]
            if emit_x:
                o_refs[0][rows, :] = xn
                o_refs[1][rows, :] = normed.astype(o_refs[1].dtype)
            else:
                o_refs[0][rows, :] = normed.astype(o_refs[0].dtype)


def _out_proj(m, w_o, layer, x, g, emit_x, tm=512, sub=2, w_steps=2):
    n, d = x.shape
    row = pl.BlockSpec((tm, d), lambda s: (jnp.maximum(s - w_steps, 0), 0))
    if emit_x:
        out_specs = [row, row]
        out_shape = [jax.ShapeDtypeStruct((n, d), F32), jax.ShapeDtypeStruct((n, d), BF16)]
    else:
        out_specs = row
        out_shape = jax.ShapeDtypeStruct((n, d), F32)
    w_spec = pl.BlockSpec((None, d // w_steps, d), lambda s: (layer, jnp.minimum(s, w_steps - 1), 0))
    return pl.pallas_call(
        functools.partial(_out_kernel, emit_x=emit_x, sub=sub, w_steps=w_steps),
        grid=(w_steps + n // tm,),
        in_specs=[row, w_spec, row, pl.BlockSpec((1, d), lambda s: (0, 0))],
        out_specs=out_specs,
        out_shape=out_shape,
        scratch_shapes=[pltpu.VMEM((d, d), BF16)],
        compiler_params=_params("arbitrary"),
        name="out_proj_residual_norm",
    )(m, w_o, x, g.reshape(1, d))


def kernel(x, norm_g, w_in, conv_w, w_out_a, a_re, a_im, log_dt, b_re, b_im, c_re, c_im,
           d_skip, w_glu, b_glu, w_out_b, w_o, final_g):
    bsz, seq, d = x.shape
    depth = norm_g.shape[0]
    n = bsz * seq
    n_chunks = seq // SSM_CHUNK
    assert 2 * bsz == SUBLANES and n_chunks % 2 == 0
    xf = x.reshape(n, d)
    h = _rmsnorm(xf, norm_g[0])
    ssm_ops = _ssm_weights(a_re, a_im, log_dt, b_re, b_im, c_re, c_im)
    for l in range(depth):
        a_in, sga, sgb = _in_proj_a(h, w_in, conv_w, l, seq)
        u, szb = _in_proj_b(h, w_in, l)
        y = _ssm(u.reshape(bsz, seq, SSM_WIDTH), *ssm_ops, l).reshape(n, SSM_WIDTH)
        m = _tail(y, u, szb, a_in, sga, sgb, d_skip, b_glu, w_glu, w_out_a, w_out_b, l)
        if l + 1 < depth:
            xf, h = _out_proj(m, w_o, l, xf, norm_g[l + 1], True)
        else:
            out = _out_proj(m, w_o, l, xf, final_g, False)
    return out.reshape(bsz, seq, d)
```

```python
import functools

import jax
import jax.numpy as jnp
from jax import lax
from jax.experimental import pallas as pl
from jax.experimental.pallas import tpu as pltpu

F32 = jnp.float32
BF16 = jnp.bfloat16

D_MODEL = 2048
CONV_WIDTH = D_MODEL
CONV_K = 3
SSM_WIDTH = D_MODEL // 2
SSM_GROUP = 16
RMS_EPS = 1e-6

OFF_V = 0
OFF_BG = CONV_WIDTH
OFF_CG = 2 * CONV_WIDTH
OFF_ZA = 3 * CONV_WIDTH
OFF_U = 4 * CONV_WIDTH
OFF_ZB = OFF_U + SSM_WIDTH
OFF_G = OFF_ZB + SSM_WIDTH

SSM_CHUNK = 16
CHUNK_LANES = SSM_CHUNK * SSM_GROUP
LANE_TILE = 128
SUBLANES = 8
STATE_LANES = LANE_TILE
GROUPS_PER_STEP = LANE_TILE // SSM_GROUP
PREP_ROWS = 24

V7X_VMEM_LIMIT = 56 * 1024 * 1024


def _sigmoid(x):
    return 0.5 * (jnp.tanh(0.5 * x) + 1.0)


def _params(*sem):
    return pltpu.CompilerParams(dimension_semantics=sem, vmem_limit_bytes=V7X_VMEM_LIMIT)


def _rmsnorm_kernel(x_ref, g_ref, o_ref):
    x = x_ref[...]
    ms = jnp.mean(x * x, axis=-1, keepdims=True)
    o_ref[...] = (x * lax.rsqrt(ms + RMS_EPS) * g_ref[...]).astype(o_ref.dtype)


def _rmsnorm(x, g, tm=1024):
    n, d = x.shape
    return pl.pallas_call(
        _rmsnorm_kernel,
        grid=(n // tm,),
        in_specs=[pl.BlockSpec((tm, d), lambda i: (i, 0)),
                  pl.BlockSpec((1, d), lambda i: (0, 0))],
        out_specs=pl.BlockSpec((tm, d), lambda i: (i, 0)),
        out_shape=jax.ShapeDtypeStruct((n, d), BF16),
        compiler_params=_params("arbitrary"),
        name="rmsnorm",
    )(x, g.reshape(1, d))


def _in_proj_a_kernel(h_ref, w_hbm, cw_ref, a_ref, sga_ref, sgb_ref,
                      carry_ref, w_ref, stage_ref, sem, *, layer, offsets, n_blocks, tiles_per_seq, sub):
    j = pl.program_id(0)
    i = pl.program_id(1)
    tn = a_ref.shape[1]

    def weight_copies(block):
        return [pltpu.make_async_copy(
            w_hbm.at[layer, :, pl.ds(pl.multiple_of(off + block * tn, tn), tn)], stage_ref.at[k], sem.at[k])
            for k, off in enumerate(offsets)]

    @pl.when(i == 0)
    def _():
        @pl.when(j == 0)
        def _():
            for c in weight_copies(0):
                c.start()

        for c in weight_copies(j):
            c.wait()
        for k in range(len(offsets)):
            w_ref[k] = stage_ref[k].astype(BF16)

        @pl.when(j + 1 < n_blocks)
        def _():
            for c in weight_copies(j + 1):
                c.start()

    @pl.when(i % tiles_per_seq == 0)
    def _():
        carry_ref[...] = jnp.zeros_like(carry_ref)

    cw = cw_ref[...]
    rs = h_ref.shape[0] // sub
    row = lax.broadcasted_iota(jnp.int32, (rs, tn), 0)
    tail = carry_ref[...]
    for r in range(sub):
        rows = slice(r * rs, (r + 1) * rs)
        h = h_ref[rows, :]
        v = jnp.dot(h, w_ref[0], preferred_element_type=F32)
        cg = jnp.dot(h, w_ref[2], preferred_element_type=F32)
        cv = cg * v
        za = jnp.dot(h, w_ref[3], preferred_element_type=F32)
        sza = za * _sigmoid(za)
        bg = jnp.dot(h, w_ref[1], preferred_element_type=F32)
        prev1 = tail[SUBLANES - 1:SUBLANES, :]
        prev2 = tail[SUBLANES - 2:SUBLANES - 1, :]
        cv1 = jnp.where(row == 0, prev1, pltpu.roll(cv, 1, axis=0))
        cv2 = jnp.where(row == 0, prev2, jnp.where(row == 1, prev1, pltpu.roll(cv, 2, axis=0)))
        conv = cw[0:1, :] * cv2 + cw[1:2, :] * cv1 + cw[2:3, :] * cv
        a_ref[rows, :] = (bg * conv * sza).astype(a_ref.dtype)
        tail = cv[rs - SUBLANES:rs, :]
        ga = jnp.dot(h, w_ref[4], preferred_element_type=F32)
        sga_ref[rows, :] = _sigmoid(ga).astype(sga_ref.dtype)
        gb = jnp.dot(h, w_ref[5], preferred_element_type=F32)
        sgb_ref[rows, :] = _sigmoid(gb).astype(sgb_ref.dtype)
    carry_ref[...] = tail


def _in_proj_a(h, w_in, conv_w, layer, seq_len, tm=2048, tn=256, sub=16):
    n, d = h.shape
    nj = CONV_WIDTH // tn
    offsets = (OFF_V, OFF_BG, OFF_CG, OFF_ZA, OFF_G, OFF_G + D_MODEL)
    out = pl.BlockSpec((tm, tn), lambda j, i: (i, j))
    shape = jax.ShapeDtypeStruct((n, CONV_WIDTH), BF16)
    return pl.pallas_call(
        functools.partial(_in_proj_a_kernel, layer=layer, offsets=offsets, n_blocks=nj,
                          tiles_per_seq=seq_len // tm, sub=sub),
        grid=(nj, n // tm),
        in_specs=[pl.BlockSpec((tm, d), lambda j, i: (i, 0)),
                  pl.BlockSpec(memory_space=pl.ANY),
                  pl.BlockSpec((None, CONV_K, tn), lambda j, i: (layer, 0, j))],
        out_specs=[out, out, out],
        out_shape=[shape, shape, shape],
        scratch_shapes=[pltpu.VMEM((SUBLANES, tn), F32), pltpu.VMEM((len(offsets), d, tn), BF16),
                        pltpu.VMEM((len(offsets), d, tn), F32), pltpu.SemaphoreType.DMA((len(offsets),))],
        compiler_params=_params("arbitrary", "arbitrary"),
        name="in_proj_conv_gates",
    )(h, w_in, conv_w)


def _in_proj_b_kernel(h_ref, wu_ref, wzb_ref, u_ref, szb_ref, w_ref, *, sub):
    @pl.when(pl.program_id(1) == 0)
    def _():
        w_ref[0] = wu_ref[...].astype(BF16)
        w_ref[1] = wzb_ref[...].astype(BF16)

    rs = h_ref.shape[0] // sub
    for r in range(sub):
        rows = slice(r * rs, (r + 1) * rs)
        h = h_ref[rows, :]
        u_ref[rows, :] = jnp.dot(h, w_ref[0], preferred_element_type=F32)
        zb = jnp.dot(h, w_ref[1], preferred_element_type=F32)
        szb_ref[rows, :] = (zb * _sigmoid(zb)).astype(szb_ref.dtype)


def _in_proj_b(h, w_in, layer, tm=1024, tn=512, sub=8):
    n, d = h.shape

    def wspec(off):
        return pl.BlockSpec((None, d, tn), lambda j, i, o=off // tn: (layer, 0, o + j))

    out = pl.BlockSpec((tm, tn), lambda j, i: (i, j))
    return pl.pallas_call(
        functools.partial(_in_proj_b_kernel, sub=sub),
        grid=(SSM_WIDTH // tn, n // tm),
        in_specs=[pl.BlockSpec((tm, d), lambda j, i: (i, 0)), wspec(OFF_U), wspec(OFF_ZB)],
        out_specs=[out, out],
        out_shape=[jax.ShapeDtypeStruct((n, SSM_WIDTH), F32), jax.ShapeDtypeStruct((n, SSM_WIDTH), BF16)],
        scratch_shapes=[pltpu.VMEM((2, d, tn), BF16)],
        compiler_params=_params("arbitrary", "arbitrary"),
        name="in_proj_ssm",
    )(h, w_in, w_in)


def _ssm_kernel(u_ref, m0_ref, win_ref, wout_ref, lre_ref, lim_ref, y_ref,
                t_ref, ub_ref, t2_ref, sre_ref, sim_ref, hre_ref, him_ref, *, n_chunks, batch):
    gb = m0_ref.shape[0]
    t = SSM_CHUNK
    for b in range(batch):
        for s in range(t):
            xt = u_ref[b, pl.ds(s, n_chunks, stride=t), :].astype(BF16).T
            for g in range(gb):
                row0 = g * CHUNK_LANES + s * SSM_GROUP
                t_ref[b, row0:row0 + SSM_GROUP, :] = xt[g * SSM_GROUP:(g + 1) * SSM_GROUP, :]
    for b in range(batch):
        for g in range(gb):
            ub_ref[g, b * n_chunks:(b + 1) * n_chunks, :] = t_ref[b, g * CHUNK_LANES:(g + 1) * CHUNK_LANES, :].T

    for g in range(gb):
        s = jnp.dot(ub_ref[g], win_ref[g], preferred_element_type=F32)
        for b in range(batch):
            sb = s[b * n_chunks:(b + 1) * n_chunks, :]
            sre_ref[g, pl.ds(b, n_chunks, stride=batch), :] = sb[:, :STATE_LANES]
            sim_ref[g, pl.ds(b, n_chunks, stride=batch), :] = sb[:, STATE_LANES:]
    lre = jnp.stack([lre_ref[0, :, g * STATE_LANES:(g + 1) * STATE_LANES] for g in range(gb)], axis=0)
    lim = jnp.stack([lim_ref[0, :, g * STATE_LANES:(g + 1) * STATE_LANES] for g in range(gb)], axis=0)
    rows = 2 * batch
    first = lax.broadcasted_iota(jnp.int32, (gb, rows, STATE_LANES), 1) < batch

    def advance(h_r, h_i, s_r, s_i):
        return lre * h_r - lim * h_i + s_r, lre * h_i + lim * h_r + s_i

    def step(j, carry):
        h_r, h_i = carry
        r0 = pl.multiple_of(j * rows, rows)
        s_r = sre_ref[:, pl.ds(r0, rows), :]
        s_i = sim_ref[:, pl.ds(r0, rows), :]
        n_r, n_i = advance(h_r, h_i, s_r, s_i)
        h_r = jnp.where(first, h_r, pltpu.roll(n_r, batch, axis=1))
        h_i = jnp.where(first, h_i, pltpu.roll(n_i, batch, axis=1))
        hre_ref[:, pl.ds(r0, rows), :] = h_r
        him_ref[:, pl.ds(r0, rows), :] = h_i
        n_r, n_i = advance(h_r, h_i, s_r, s_i)
        return pltpu.roll(n_r, batch, axis=1), pltpu.roll(n_i, batch, axis=1)

    zero = jnp.zeros((gb, rows, STATE_LANES), F32)
    lax.fori_loop(0, n_chunks // 2, step, (zero, zero))

    for g in range(gb):
        h = jnp.concatenate(
            [jnp.concatenate([ref[g, pl.ds(b, n_chunks, stride=batch), :] for b in range(batch)], axis=0)
             for ref in (hre_ref, him_ref)], axis=1)
        y = jnp.dot(ub_ref[g], m0_ref[g], preferred_element_type=F32)
        y = y + jnp.dot(h.astype(BF16), wout_ref[g], preferred_element_type=F32)
        for b in range(batch):
            at = y[b * n_chunks:(b + 1) * n_chunks, :].T
            for tt in range(t):
                row0 = tt * LANE_TILE + g * SSM_GROUP
                t2_ref[b, row0:row0 + SSM_GROUP, :] = at[tt * SSM_GROUP:(tt + 1) * SSM_GROUP, :]
    for b in range(batch):
        for tt in range(t):
            y_ref[b, pl.ds(tt, n_chunks, stride=t), :] = t2_ref[b, tt * LANE_TILE:(tt + 1) * LANE_TILE, :].T


def _ssm(u_p, m0, w_in_state, w_out_state, lam_re, lam_im, layer):
    batch, seq, width = u_p.shape
    n_inst = batch * seq // SSM_CHUNK
    gb = GROUPS_PER_STEP
    assert gb * SSM_GROUP == LANE_TILE and n_inst // batch == LANE_TILE
    first = layer * (width // LANE_TILE)
    blk = pl.BlockSpec((batch, seq, LANE_TILE), lambda i: (0, 0, i))
    sq = pl.BlockSpec((gb, CHUNK_LANES, CHUNK_LANES), lambda i: (first + i, 0, 0))
    vec = pl.BlockSpec((1, 1, gb * STATE_LANES), lambda i: (first + i, 0, 0))
    return pl.pallas_call(
        functools.partial(_ssm_kernel, n_chunks=n_inst // batch, batch=batch),
        grid=(width // LANE_TILE,),
        in_specs=[blk, sq, sq, sq, vec, vec],
        out_specs=blk,
        out_shape=jax.ShapeDtypeStruct(u_p.shape, F32),
        scratch_shapes=[pltpu.VMEM((batch, gb * CHUNK_LANES, LANE_TILE), BF16),
                        pltpu.VMEM((gb, n_inst, CHUNK_LANES), BF16),
                        pltpu.VMEM((batch, gb * CHUNK_LANES, LANE_TILE), F32)]
                       + [pltpu.VMEM((gb, n_inst, STATE_LANES), F32) for _ in range(4)],
        compiler_params=_params("arbitrary"),
        name="s5_chunked_scan",
    )(u_p, m0, w_in_state, w_out_state, lam_re, lam_im)


def _split_bf16(x):
    hi = x.astype(BF16)
    return hi, (x - hi.astype(F32)).astype(BF16)


def _dot_nt_f32(a, b):
    dn = (((1,), (1,)), ((), ()))
    a_hi, a_lo = _split_bf16(a)
    b_hi, b_lo = _split_bf16(b)

    def d(x, y):
        return lax.dot_general(x, y, dn, preferred_element_type=F32)

    return d(a_hi, b_hi) + d(a_hi, b_lo) + d(a_lo, b_hi)


def _ssm_prep_kernel(vec_ref, mat_ref, m0_ref, win_ref, wout_ref, lre_ref, lim_ref, *, n_state):
    gb = vec_ref.shape[0]
    t = SSM_CHUNK
    tau = lax.broadcasted_iota(jnp.int32, (PREP_ROWS, STATE_LANES), 0).astype(F32)
    lane = lax.broadcasted_iota(jnp.int32, (SSM_GROUP, CHUNK_LANES), 1)
    valid = lax.broadcasted_iota(jnp.int32, (1, STATE_LANES), 1) < n_state
    for g in range(gb):
        are = jnp.where(valid, vec_ref[g, 0:1, :], -1.0)
        aim = vec_ref[g, 1:2, :]
        dt = jnp.exp(vec_ref[g, 2:3, :])
        mag = jnp.exp(tau * (dt * are))
        ang = tau * (dt * aim)
        pw_re = mag * jnp.cos(ang)
        pw_im = mag * jnp.sin(ang)
        nr = pw_re[1:2] - 1.0
        ni = pw_im[1:2]
        den = are * are + aim * aim
        cr = (nr * are + ni * aim) / den
        ci = (ni * are - nr * aim) / den
        btr = mat_ref[g, 0]
        bti = mat_ref[g, 1]
        bb_re = cr * btr - ci * bti
        bb_im = cr * bti + ci * btr
        cre = mat_ref[g, 2]
        cim = mat_ref[g, 3]

        win_re, win_im = [], []
        for s in range(t):
            pr = pw_re[t - 1 - s:t - s]
            pi = pw_im[t - 1 - s:t - s]
            win_re.append(pr * bb_re - pi * bb_im)
            win_im.append(pr * bb_im + pi * bb_re)
        win_ref[g] = jnp.concatenate(
            [jnp.concatenate(win_re, axis=0), jnp.concatenate(win_im, axis=0)], axis=1).astype(BF16)

        z_re, z_im = [], []
        for k in range(t + 1):
            pr = pw_re[k:k + 1]
            pi = pw_im[k:k + 1]
            z_re.append(cre * pr - cim * pi)
            z_im.append(cre * pi + cim * pr)

        zo_re = jnp.concatenate(z_re[1:], axis=0)
        zo_im = jnp.concatenate(z_im[1:], axis=0)
        wout_ref[g] = jnp.concatenate([zo_re.T, -(zo_im.T)], axis=0).astype(BF16)

        zk_re = jnp.concatenate(z_re[:t], axis=0)
        zk_im = jnp.concatenate(z_im[:t], axis=0)
        kt = _dot_nt_f32(bb_re, zk_re) - _dot_nt_f32(bb_im, zk_im)
        blocks = [kt]
        for s in range(1, t):
            shifted = pltpu.roll(kt, s * SSM_GROUP, axis=1)
            blocks.append(jnp.where(lane >= s * SSM_GROUP, shifted, 0.0))
        m0_ref[g] = jnp.concatenate(blocks, axis=0).astype(BF16)

        lre_ref[0, :, g * STATE_LANES:(g + 1) * STATE_LANES] = pw_re[t:t + 1]
        lim_ref[0, :, g * STATE_LANES:(g + 1) * STATE_LANES] = pw_im[t:t + 1]


def _ssm_weights(a_re, a_im, log_dt, b_re, b_im, c_re, c_im):
    a_re, a_im, b_re, b_im, c_re, c_im = (
        x.reshape((-1,) + x.shape[2:]) for x in (a_re, a_im, b_re, b_im, c_re, c_im))
    log_dt = log_dt.reshape(-1)
    g, p = a_re.shape
    gb = GROUPS_PER_STEP
    pad = STATE_LANES - p

    def lanes(x):
        return jnp.pad(x.astype(F32), [(0, 0)] * (x.ndim - 1) + [(0, pad)])

    vecs = lanes(jnp.stack([a_re, a_im, jnp.broadcast_to(log_dt[:, None], a_re.shape)], axis=1))
    mats = lanes(jnp.stack([jnp.swapaxes(b_re, 1, 2), jnp.swapaxes(b_im, 1, 2), c_re, c_im], axis=1))
    vec = pl.BlockSpec((gb, 3, STATE_LANES), lambda i: (i, 0, 0))
    mat = pl.BlockSpec((gb, 4, SSM_GROUP, STATE_LANES), lambda i: (i, 0, 0, 0))
    sq = pl.BlockSpec((gb, CHUNK_LANES, CHUNK_LANES), lambda i: (i, 0, 0))
    row = pl.BlockSpec((1, 1, gb * STATE_LANES), lambda i: (i, 0, 0))
    sq_shape = jax.ShapeDtypeStruct((g, CHUNK_LANES, CHUNK_LANES), BF16)
    row_shape = jax.ShapeDtypeStruct((g // gb, 1, gb * STATE_LANES), F32)
    return pl.pallas_call(
        functools.partial(_ssm_prep_kernel, n_state=p),
        grid=(g // gb,),
        in_specs=[vec, mat],
        out_specs=[sq, sq, sq, row, row],
        out_shape=[sq_shape, sq_shape, sq_shape, row_shape, row_shape],
        compiler_params=_params("arbitrary"),
        name="s5_chunk_operators",
    )(vecs, mats)


STAGE_ROWS = 512


def _tail_kernel(y_ref, u_ref, szb_ref, a_ref, sga_ref, sgb_ref, d_ref, bgl_ref,
                 wg32_ref, wa32_ref, wb32_ref, m_ref, wg_ref, wa_ref, wb_ref, *, sub):
    step = pl.program_id(0)
    stages = ((wg32_ref, wg_ref), (wa32_ref, wa_ref), (wb32_ref, wb_ref))
    first = 0
    for src, dst in stages:
        n_chunks = dst.shape[0] // STAGE_ROWS

        @pl.when((step >= first) & (step < first + n_chunks))
        def _(src=src, dst=dst, first=first):
            r0 = pl.multiple_of((step - first) * STAGE_ROWS, STAGE_ROWS)
            dst[pl.ds(r0, STAGE_ROWS), :] = src[...].astype(BF16)

        first += n_chunks

    @pl.when(step >= first)
    def _():
        rs = y_ref.shape[0] // sub
        for r in range(sub):
            rows = slice(r * rs, (r + 1) * rs)
            yb = jax.nn.gelu(y_ref[rows, :] + d_ref[...] * u_ref[rows, :])
            z = jnp.dot(yb.astype(BF16), wg_ref[...], preferred_element_type=F32) + bgl_ref[...]
            b = (yb * _sigmoid(z) * szb_ref[rows, :].astype(F32)).astype(BF16)
            ya = jnp.dot(a_ref[rows, :], wa_ref[...], preferred_element_type=F32)
            yb2 = jnp.dot(b, wb_ref[...], preferred_element_type=F32)
            m_ref[rows, :] = (sga_ref[rows, :].astype(F32) * ya
                              + sgb_ref[rows, :].astype(F32) * yb2).astype(m_ref.dtype)


def _tail(y, u, szb, a_in, sga, sgb, d_skip, b_glu, w_glu, w_a, w_b, layer, tm=256, sub=2):
    n, w = y.shape
    d = a_in.shape[1]
    counts = [rows // STAGE_ROWS for rows in (w, d, w)]
    starts = [sum(counts[:k]) for k in range(len(counts))]
    w_steps = sum(counts)

    def tile(width):
        return pl.BlockSpec((tm, width), lambda s: (jnp.maximum(s - w_steps, 0), 0))

    def staged(k, width):
        return pl.BlockSpec((None, STAGE_ROWS, width),
                            lambda s: (layer, jnp.clip(s - starts[k], 0, counts[k] - 1), 0))

    vec = pl.BlockSpec((None, 1, w), lambda s: (layer, 0, 0))
    return pl.pallas_call(
        functools.partial(_tail_kernel, sub=sub),
        grid=(w_steps + n // tm,),
        in_specs=[tile(w), tile(w), tile(w), tile(d), tile(d), tile(d), vec, vec,
                  staged(0, w), staged(1, d), staged(2, d)],
        out_specs=tile(d),
        out_shape=jax.ShapeDtypeStruct((n, d), BF16),
        scratch_shapes=[pltpu.VMEM((w, w), BF16), pltpu.VMEM((d, d), BF16), pltpu.VMEM((w, d), BF16)],
        compiler_params=_params("arbitrary"),
        name="glu_gated_merge",
    )(y, u, szb, a_in, sga, sgb, d_skip.reshape(-1, 1, w), b_glu.reshape(-1, 1, w), w_glu, w_a, w_b)


def _out_kernel(m_ref, wo32_ref, x_ref, g_ref, *refs, emit_x, sub, w_steps):
    o_refs, wo_ref = refs[:-1], refs[-1]
    step = pl.program_id(0)
    wr = wo32_ref.shape[0]

    @pl.when(step < w_steps)
    def _():
        r0 = pl.multiple_of(step * wr, wr)
        wo_ref[pl.ds(r0, wr), :] = wo32_ref[...].astype(BF16)

    @pl.when(step >= w_steps)
    def _():
        rs = m_ref.shape[0] // sub
        for r in range(sub):
            rows = slice(r * rs, (r + 1) * rs)
            xn = x_ref[rows, :] + jnp.dot(m_ref[rows, :], wo_ref[...], preferred_element_type=F32)
            ms = jnp.mean(xn * xn, axis=-1, keepdims=True)
            normed = xn * lax.rsqrt(ms + RMS_EPS) * g_ref[...]
            if emit_x:
                o_refs[0][rows, :] = xn
                o_refs[1][rows, :] = normed.astype(o_refs[1].dtype)
            else:
                o_refs[0][rows, :] = normed.astype(o_refs[0].dtype)


def _out_proj(m, w_o, layer, x, g, emit_x, tm=512, sub=4, w_steps=2):
    n, d = x.shape
    row = pl.BlockSpec((tm, d), lambda s: (jnp.maximum(s - w_steps, 0), 0))
    if emit_x:
        out_specs = [row, row]
        out_shape = [jax.ShapeDtypeStruct((n, d), F32), jax.ShapeDtypeStruct((n, d), BF16)]
    else:
        out_specs = row
        out_shape = jax.ShapeDtypeStruct((n, d), F32)
    w_spec = pl.BlockSpec((None, d // w_steps, d), lambda s: (layer, jnp.minimum(s, w_steps - 1), 0))
    return pl.pallas_call(
        functools.partial(_out_kernel, emit_x=emit_x, sub=sub, w_steps=w_steps),
        grid=(w_steps + n // tm,),
        in_specs=[row, w_spec, row, pl.BlockSpec((1, d), lambda s: (0, 0))],
        out_specs=out_specs,
        out_shape=out_shape,
        scratch_shapes=[pltpu.VMEM((d, d), BF16)],
        compiler_params=_params("arbitrary"),
        name="out_proj_residual_norm",
    )(m, w_o, x, g.reshape(1, d))


def kernel(x, norm_g, w_in, conv_w, w_out_a, a_re, a_im, log_dt, b_re, b_im, c_re, c_im,
           d_skip, w_glu, b_glu, w_out_b, w_o, final_g):
    bsz, seq, d = x.shape
    depth = norm_g.shape[0]
    n = bsz * seq
    n_chunks = seq // SSM_CHUNK
    assert 2 * bsz == SUBLANES and n_chunks % 2 == 0
    xf = x.reshape(n, d)
    h = _rmsnorm(xf, norm_g[0])
    ssm_ops = _ssm_weights(a_re, a_im, log_dt, b_re, b_im, c_re, c_im)
    for l in range(depth):
        a_in, sga, sgb = _in_proj_a(h, w_in, conv_w, l, seq)
        u, szb = _in_proj_b(h, w_in, l)
        y = _ssm(u.reshape(bsz, seq, SSM_WIDTH), *ssm_ops, l).reshape(n, SSM_WIDTH)
        m = _tail(y, u, szb, a_in, sga, sgb, d_skip, b_glu, w_glu, w_out_a, w_out_b, l)
        if l + 1 < depth:
            xf, h = _out_proj(m, w_o, l, xf, norm_g[l + 1], True)
        else:
            out = _out_proj(m, w_o, l, xf, final_g, False)
    return out.reshape(bsz, seq, d)
```

```python
import functools

import jax
import jax.numpy as jnp
from jax import lax
from jax.experimental import pallas as pl
from jax.experimental.pallas import tpu as pltpu

F32 = jnp.float32
BF16 = jnp.bfloat16

D_MODEL = 2048
CONV_WIDTH = D_MODEL
CONV_K = 3
SSM_WIDTH = D_MODEL // 2
SSM_GROUP = 16
RMS_EPS = 1e-6

OFF_V = 0
OFF_BG = CONV_WIDTH
OFF_CG = 2 * CONV_WIDTH
OFF_ZA = 3 * CONV_WIDTH
OFF_U = 4 * CONV_WIDTH
OFF_ZB = OFF_U + SSM_WIDTH
OFF_G = OFF_ZB + SSM_WIDTH

SSM_CHUNK = 16
CHUNK_LANES = SSM_CHUNK * SSM_GROUP
LANE_TILE = 128
SUBLANES = 8
STATE_LANES = LANE_TILE
GROUPS_PER_STEP = LANE_TILE // SSM_GROUP
PREP_ROWS = 24

V7X_VMEM_LIMIT = 56 * 1024 * 1024


def _sigmoid(x):
    return 0.5 * (jnp.tanh(0.5 * x) + 1.0)


def _params(*sem):
    return pltpu.CompilerParams(dimension_semantics=sem, vmem_limit_bytes=V7X_VMEM_LIMIT)


def _rmsnorm_kernel(x_ref, g_ref, o_ref):
    x = x_ref[...]
    ms = jnp.mean(x * x, axis=-1, keepdims=True)
    o_ref[...] = (x * lax.rsqrt(ms + RMS_EPS) * g_ref[...]).astype(o_ref.dtype)


def _rmsnorm(x, g, tm=1024):
    n, d = x.shape
    return pl.pallas_call(
        _rmsnorm_kernel,
        grid=(n // tm,),
        in_specs=[pl.BlockSpec((tm, d), lambda i: (i, 0)),
                  pl.BlockSpec((1, d), lambda i: (0, 0))],
        out_specs=pl.BlockSpec((tm, d), lambda i: (i, 0)),
        out_shape=jax.ShapeDtypeStruct((n, d), BF16),
        compiler_params=_params("arbitrary"),
        name="rmsnorm",
    )(x, g.reshape(1, d))


def _in_proj_a_kernel(h_ref, w_hbm, cw_ref, a_ref, sga_ref, sgb_ref,
                      carry_ref, w_ref, stage_ref, sem, *, layer, offsets, n_blocks, tiles_per_seq, sub):
    j = pl.program_id(0)
    i = pl.program_id(1)
    tn = a_ref.shape[1]

    def weight_copies(block):
        return [pltpu.make_async_copy(
            w_hbm.at[layer, :, pl.ds(pl.multiple_of(off + block * tn, tn), tn)], stage_ref.at[k], sem.at[k])
            for k, off in enumerate(offsets)]

    @pl.when(i == 0)
    def _():
        @pl.when(j == 0)
        def _():
            for k, c in enumerate(weight_copies(0)):
                c.start(priority=k % 2)

        for k, c in enumerate(weight_copies(j)):
            c.wait()
            w_ref[k] = stage_ref[k].astype(BF16)

        @pl.when(j + 1 < n_blocks)
        def _():
            for k, c in enumerate(weight_copies(j + 1)):
                c.start(priority=k % 2)

    @pl.when(i % tiles_per_seq == 0)
    def _():
        carry_ref[...] = jnp.zeros_like(carry_ref)

    cw = cw_ref[...]
    rs = h_ref.shape[0] // sub
    row = lax.broadcasted_iota(jnp.int32, (rs, tn), 0)
    tail = carry_ref[...]
    for r in range(sub):
        rows = slice(r * rs, (r + 1) * rs)
        h = h_ref[rows, :]
        v = jnp.dot(h, w_ref[0], preferred_element_type=F32)
        cg = jnp.dot(h, w_ref[2], preferred_element_type=F32)
        cv = cg * v
        za = jnp.dot(h, w_ref[3], preferred_element_type=F32)
        sza = za * _sigmoid(za)
        bg = jnp.dot(h, w_ref[1], preferred_element_type=F32)
        prev1 = tail[SUBLANES - 1:SUBLANES, :]
        prev2 = tail[SUBLANES - 2:SUBLANES - 1, :]
        cv1 = jnp.where(row == 0, prev1, pltpu.roll(cv, 1, axis=0))
        cv2 = jnp.where(row == 0, prev2, jnp.where(row == 1, prev1, pltpu.roll(cv, 2, axis=0)))
        conv = cw[0:1, :] * cv2 + cw[1:2, :] * cv1 + cw[2:3, :] * cv
        a_ref[rows, :] = (bg * conv * sza).astype(a_ref.dtype)
        tail = cv[rs - SUBLANES:rs, :]
        ga = jnp.dot(h, w_ref[4], preferred_element_type=F32)
        sga_ref[rows, :] = _sigmoid(ga).astype(sga_ref.dtype)
        gb = jnp.dot(h, w_ref[5], preferred_element_type=F32)
        sgb_ref[rows, :] = _sigmoid(gb).astype(sgb_ref.dtype)
    carry_ref[...] = tail


def _in_proj_a(h, w_in, conv_w, layer, seq_len, tm=2048, tn=256, sub=16):
    n, d = h.shape
    nj = CONV_WIDTH // tn
    offsets = (OFF_V, OFF_BG, OFF_CG, OFF_ZA, OFF_G, OFF_G + D_MODEL)
    out = pl.BlockSpec((tm, tn), lambda j, i: (i, j))
    shape = jax.ShapeDtypeStruct((n, CONV_WIDTH), BF16)
    return pl.pallas_call(
        functools.partial(_in_proj_a_kernel, layer=layer, offsets=offsets, n_blocks=nj,
                          tiles_per_seq=seq_len // tm, sub=sub),
        grid=(nj, n // tm),
        in_specs=[pl.BlockSpec((tm, d), lambda j, i: (i, 0)),
                  pl.BlockSpec(memory_space=pl.ANY),
                  pl.BlockSpec((None, CONV_K, tn), lambda j, i: (layer, 0, j))],
        out_specs=[out, out, out],
        out_shape=[shape, shape, shape],
        scratch_shapes=[pltpu.VMEM((SUBLANES, tn), F32), pltpu.VMEM((len(offsets), d, tn), BF16),
                        pltpu.VMEM((len(offsets), d, tn), F32), pltpu.SemaphoreType.DMA((len(offsets),))],
        compiler_params=_params("arbitrary", "arbitrary"),
        name="in_proj_conv_gates",
    )(h, w_in, conv_w)


def _in_proj_b_kernel(h_ref, wu_ref, wzb_ref, u_ref, szb_ref, w_ref, *, sub):
    @pl.when(pl.program_id(1) == 0)
    def _():
        w_ref[0] = wu_ref[...].astype(BF16)
        w_ref[1] = wzb_ref[...].astype(BF16)

    rs = h_ref.shape[0] // sub
    for r in range(sub):
        rows = slice(r * rs, (r + 1) * rs)
        h = h_ref[rows, :]
        u_ref[rows, :] = jnp.dot(h, w_ref[0], preferred_element_type=F32)
        zb = jnp.dot(h, w_ref[1], preferred_element_type=F32)
        szb_ref[rows, :] = (zb * _sigmoid(zb)).astype(szb_ref.dtype)


def _in_proj_b(h, w_in, layer, tm=1024, tn=512, sub=4):
    n, d = h.shape

    def wspec(off):
        return pl.BlockSpec((None, d, tn), lambda j, i, o=off // tn: (layer, 0, o + j))

    out = pl.BlockSpec((tm, tn), lambda j, i: (i, j))
    return pl.pallas_call(
        functools.partial(_in_proj_b_kernel, sub=sub),
        grid=(SSM_WIDTH // tn, n // tm),
        in_specs=[pl.BlockSpec((tm, d), lambda j, i: (i, 0)), wspec(OFF_U), wspec(OFF_ZB)],
        out_specs=[out, out],
        out_shape=[jax.ShapeDtypeStruct((n, SSM_WIDTH), F32), jax.ShapeDtypeStruct((n, SSM_WIDTH), BF16)],
        scratch_shapes=[pltpu.VMEM((2, d, tn), BF16)],
        compiler_params=_params("arbitrary", "arbitrary"),
        name="in_proj_ssm",
    )(h, w_in, w_in)


def _ssm_kernel(u_ref, m0_ref, win_ref, wout_ref, lre_ref, lim_ref, y_ref,
                t_ref, ub_ref, t2_ref, sre_ref, sim_ref, hre_ref, him_ref, *, n_chunks, batch):
    gb = m0_ref.shape[0]
    t = SSM_CHUNK
    for b in range(batch):
        for s in range(t):
            xt = u_ref[b, pl.ds(s, n_chunks, stride=t), :].astype(BF16).T
            for g in range(gb):
                row0 = g * CHUNK_LANES + s * SSM_GROUP
                t_ref[b, row0:row0 + SSM_GROUP, :] = xt[g * SSM_GROUP:(g + 1) * SSM_GROUP, :]
    for b in range(batch):
        for g in range(gb):
            ub_ref[g, b * n_chunks:(b + 1) * n_chunks, :] = t_ref[b, g * CHUNK_LANES:(g + 1) * CHUNK_LANES, :].T

    for g in range(gb):
        s = jnp.dot(ub_ref[g], win_ref[g], preferred_element_type=F32)
        for b in range(batch):
            sb = s[b * n_chunks:(b + 1) * n_chunks, :]
            sre_ref[g, pl.ds(b, n_chunks, stride=batch), :] = sb[:, :STATE_LANES]
            sim_ref[g, pl.ds(b, n_chunks, stride=batch), :] = sb[:, STATE_LANES:]
    lre = jnp.stack([lre_ref[0, :, g * STATE_LANES:(g + 1) * STATE_LANES] for g in range(gb)], axis=0)
    lim = jnp.stack([lim_ref[0, :, g * STATE_LANES:(g + 1) * STATE_LANES] for g in range(gb)], axis=0)
    rows = 2 * batch
    first = lax.broadcasted_iota(jnp.int32, (gb, rows, STATE_LANES), 1) < batch

    def advance(h_r, h_i, s_r, s_i):
        return lre * h_r - lim * h_i + s_r, lre * h_i + lim * h_r + s_i

    def step(j, carry):
        h_r, h_i = carry
        r0 = pl.multiple_of(j * rows, rows)
        s_r = sre_ref[:, pl.ds(r0, rows), :]
        s_i = sim_ref[:, pl.ds(r0, rows), :]
        n_r, n_i = advance(h_r, h_i, s_r, s_i)
        h_r = jnp.where(first, h_r, pltpu.roll(n_r, batch, axis=1))
        h_i = jnp.where(first, h_i, pltpu.roll(n_i, batch, axis=1))
        hre_ref[:, pl.ds(r0, rows), :] = h_r
        him_ref[:, pl.ds(r0, rows), :] = h_i
        n_r, n_i = advance(h_r, h_i, s_r, s_i)
        return pltpu.roll(n_r, batch, axis=1), pltpu.roll(n_i, batch, axis=1)

    zero = jnp.zeros((gb, rows, STATE_LANES), F32)
    lax.fori_loop(0, n_chunks // 2, step, (zero, zero))

    for g in range(gb):
        h = jnp.concatenate(
            [jnp.concatenate([ref[g, pl.ds(b, n_chunks, stride=batch), :] for b in range(batch)], axis=0)
             for ref in (hre_ref, him_ref)], axis=1)
        y = jnp.dot(ub_ref[g], m0_ref[g], preferred_element_type=F32)
        y = y + jnp.dot(h.astype(BF16), wout_ref[g], preferred_element_type=F32)
        for b in range(batch):
            at = y[b * n_chunks:(b + 1) * n_chunks, :].T
            for tt in range(t):
                row0 = tt * LANE_TILE + g * SSM_GROUP
                t2_ref[b, row0:row0 + SSM_GROUP, :] = at[tt * SSM_GROUP:(tt + 1) * SSM_GROUP, :]
    for b in range(batch):
        for tt in range(t):
            y_ref[b, pl.ds(tt, n_chunks, stride=t), :] = t2_ref[b, tt * LANE_TILE:(tt + 1) * LANE_TILE, :].T


def _ssm(u_p, m0, w_in_state, w_out_state, lam_re, lam_im, layer):
    batch, seq, width = u_p.shape
    n_inst = batch * seq // SSM_CHUNK
    gb = GROUPS_PER_STEP
    assert gb * SSM_GROUP == LANE_TILE and n_inst // batch == LANE_TILE
    first = layer * (width // LANE_TILE)
    blk = pl.BlockSpec((batch, seq, LANE_TILE), lambda i: (0, 0, i))
    sq = pl.BlockSpec((gb, CHUNK_LANES, CHUNK_LANES), lambda i: (first + i, 0, 0))
    vec = pl.BlockSpec((1, 1, gb * STATE_LANES), lambda i: (first + i, 0, 0))
    return pl.pallas_call(
        functools.partial(_ssm_kernel, n_chunks=n_inst // batch, batch=batch),
        grid=(width // LANE_TILE,),
        in_specs=[blk, sq, sq, sq, vec, vec],
        out_specs=blk,
        out_shape=jax.ShapeDtypeStruct(u_p.shape, F32),
        scratch_shapes=[pltpu.VMEM((batch, gb * CHUNK_LANES, LANE_TILE), BF16),
                        pltpu.VMEM((gb, n_inst, CHUNK_LANES), BF16),
                        pltpu.VMEM((batch, gb * CHUNK_LANES, LANE_TILE), F32)]
                       + [pltpu.VMEM((gb, n_inst, STATE_LANES), F32) for _ in range(4)],
        compiler_params=_params("arbitrary"),
        name="s5_chunked_scan",
    )(u_p, m0, w_in_state, w_out_state, lam_re, lam_im)


def _split_bf16(x):
    hi = x.astype(BF16)
    return hi, (x - hi.astype(F32)).astype(BF16)


def _dot_nt_f32(a, b):
    dn = (((1,), (1,)), ((), ()))
    a_hi, a_lo = _split_bf16(a)
    b_hi, b_lo = _split_bf16(b)

    def d(x, y):
        return lax.dot_general(x, y, dn, preferred_element_type=F32)

    return d(a_hi, b_hi) + d(a_hi, b_lo) + d(a_lo, b_hi)


def _ssm_prep_kernel(vec_ref, mat_ref, m0_ref, win_ref, wout_ref, lre_ref, lim_ref, *, n_state):
    gb = vec_ref.shape[0]
    t = SSM_CHUNK
    tau = lax.broadcasted_iota(jnp.int32, (PREP_ROWS, STATE_LANES), 0).astype(F32)
    lane = lax.broadcasted_iota(jnp.int32, (SSM_GROUP, CHUNK_LANES), 1)
    valid = lax.broadcasted_iota(jnp.int32, (1, STATE_LANES), 1) < n_state
    for g in range(gb):
        are = jnp.where(valid, vec_ref[g, 0:1, :], -1.0)
        aim = vec_ref[g, 1:2, :]
        dt = jnp.exp(vec_ref[g, 2:3, :])
        mag = jnp.exp(tau * (dt * are))
        ang = tau * (dt * aim)
        pw_re = mag * jnp.cos(ang)
        pw_im = mag * jnp.sin(ang)
        nr = pw_re[1:2] - 1.0
        ni = pw_im[1:2]
        den = are * are + aim * aim
        cr = (nr * are + ni * aim) / den
        ci = (ni * are - nr * aim) / den
        btr = mat_ref[g, 0]
        bti = mat_ref[g, 1]
        bb_re = cr * btr - ci * bti
        bb_im = cr * bti + ci * btr
        cre = mat_ref[g, 2]
        cim = mat_ref[g, 3]

        win_re, win_im = [], []
        for s in range(t):
            pr = pw_re[t - 1 - s:t - s]
            pi = pw_im[t - 1 - s:t - s]
            win_re.append(pr * bb_re - pi * bb_im)
            win_im.append(pr * bb_im + pi * bb_re)
        win_ref[g] = jnp.concatenate(
            [jnp.concatenate(win_re, axis=0), jnp.concatenate(win_im, axis=0)], axis=1).astype(BF16)

        z_re, z_im = [], []
        for k in range(t + 1):
            pr = pw_re[k:k + 1]
            pi = pw_im[k:k + 1]
            z_re.append(cre * pr - cim * pi)
            z_im.append(cre * pi + cim * pr)

        zo_re = jnp.concatenate(z_re[1:], axis=0)
        zo_im = jnp.concatenate(z_im[1:], axis=0)
        wout_ref[g] = jnp.concatenate([zo_re.T, -(zo_im.T)], axis=0).astype(BF16)

        zk_re = jnp.concatenate(z_re[:t], axis=0)
        zk_im = jnp.concatenate(z_im[:t], axis=0)
        kt = _dot_nt_f32(bb_re, zk_re) - _dot_nt_f32(bb_im, zk_im)
        blocks = [kt]
        for s in range(1, t):
            shifted = pltpu.roll(kt, s * SSM_GROUP, axis=1)
            blocks.append(jnp.where(lane >= s * SSM_GROUP, shifted, 0.0))
        m0_ref[g] = jnp.concatenate(blocks, axis=0).astype(BF16)

        lre_ref[0, :, g * STATE_LANES:(g + 1) * STATE_LANES] = pw_re[t:t + 1]
        lim_ref[0, :, g * STATE_LANES:(g + 1) * STATE_LANES] = pw_im[t:t + 1]


def _ssm_weights(a_re, a_im, log_dt, b_re, b_im, c_re, c_im):
    a_re, a_im, b_re, b_im, c_re, c_im = (
        x.reshape((-1,) + x.shape[2:]) for x in (a_re, a_im, b_re, b_im, c_re, c_im))
    log_dt = log_dt.reshape(-1)
    g, p = a_re.shape
    gb = GROUPS_PER_STEP
    pad = STATE_LANES - p

    def lanes(x):
        return jnp.pad(x.astype(F32), [(0, 0)] * (x.ndim - 1) + [(0, pad)])

    vecs = lanes(jnp.stack([a_re, a_im, jnp.broadcast_to(log_dt[:, None], a_re.shape)], axis=1))
    mats = lanes(jnp.stack([jnp.swapaxes(b_re, 1, 2), jnp.swapaxes(b_im, 1, 2), c_re, c_im], axis=1))
    vec = pl.BlockSpec((gb, 3, STATE_LANES), lambda i: (i, 0, 0))
    mat = pl.BlockSpec((gb, 4, SSM_GROUP, STATE_LANES), lambda i: (i, 0, 0, 0))
    sq = pl.BlockSpec((gb, CHUNK_LANES, CHUNK_LANES), lambda i: (i, 0, 0))
    row = pl.BlockSpec((1, 1, gb * STATE_LANES), lambda i: (i, 0, 0))
    sq_shape = jax.ShapeDtypeStruct((g, CHUNK_LANES, CHUNK_LANES), BF16)
    row_shape = jax.ShapeDtypeStruct((g // gb, 1, gb * STATE_LANES), F32)
    return pl.pallas_call(
        functools.partial(_ssm_prep_kernel, n_state=p),
        grid=(g // gb,),
        in_specs=[vec, mat],
        out_specs=[sq, sq, sq, row, row],
        out_shape=[sq_shape, sq_shape, sq_shape, row_shape, row_shape],
        compiler_params=_params("arbitrary"),
        name="s5_chunk_operators",
    )(vecs, mats)


STAGE_ROWS = 512


def _tail_kernel(y_ref, u_ref, szb_ref, a_ref, sga_ref, sgb_ref, d_ref, bgl_ref,
                 wg32_ref, wa32_ref, wb32_ref, m_ref, wg_ref, wa_ref, wb_ref, *, sub):
    step = pl.program_id(0)
    stages = ((wg32_ref, wg_ref), (wa32_ref, wa_ref), (wb32_ref, wb_ref))
    first = 0
    for src, dst in stages:
        n_chunks = dst.shape[0] // STAGE_ROWS

        @pl.when((step >= first) & (step < first + n_chunks))
        def _(src=src, dst=dst, first=first):
            r0 = pl.multiple_of((step - first) * STAGE_ROWS, STAGE_ROWS)
            dst[pl.ds(r0, STAGE_ROWS), :] = src[...].astype(BF16)

        first += n_chunks

    @pl.when(step >= first)
    def _():
        rs = y_ref.shape[0] // sub
        for r in range(sub):
            rows = slice(r * rs, (r + 1) * rs)
            yb = jax.nn.gelu(y_ref[rows, :] + d_ref[...] * u_ref[rows, :])
            z = jnp.dot(yb.astype(BF16), wg_ref[...], preferred_element_type=F32) + bgl_ref[...]
            b = (yb * _sigmoid(z) * szb_ref[rows, :].astype(F32)).astype(BF16)
            ya = jnp.dot(a_ref[rows, :], wa_ref[...], preferred_element_type=F32)
            yb2 = jnp.dot(b, wb_ref[...], preferred_element_type=F32)
            m_ref[rows, :] = (sga_ref[rows, :].astype(F32) * ya
                              + sgb_ref[rows, :].astype(F32) * yb2).astype(m_ref.dtype)


def _tail(y, u, szb, a_in, sga, sgb, d_skip, b_glu, w_glu, w_a, w_b, layer, tm=256, sub=1):
    n, w = y.shape
    d = a_in.shape[1]
    counts = [rows // STAGE_ROWS for rows in (w, d, w)]
    starts = [sum(counts[:k]) for k in range(len(counts))]
    w_steps = sum(counts)

    def tile(width):
        return pl.BlockSpec((tm, width), lambda s: (jnp.maximum(s - w_steps, 0), 0))

    def staged(k, width):
        return pl.BlockSpec((None, STAGE_ROWS, width),
                            lambda s: (layer, jnp.clip(s - starts[k], 0, counts[k] - 1), 0))

    vec = pl.BlockSpec((None, 1, w), lambda s: (layer, 0, 0))
    return pl.pallas_call(
        functools.partial(_tail_kernel, sub=sub),
        grid=(w_steps + n // tm,),
        in_specs=[tile(w), tile(w), tile(w), tile(d), tile(d), tile(d), vec, vec,
                  staged(0, w), staged(1, d), staged(2, d)],
        out_specs=tile(d),
        out_shape=jax.ShapeDtypeStruct((n, d), BF16),
        scratch_shapes=[pltpu.VMEM((w, w), BF16), pltpu.VMEM((d, d), BF16), pltpu.VMEM((w, d), BF16)],
        compiler_params=_params("arbitrary"),
        name="glu_gated_merge",
    )(y, u, szb, a_in, sga, sgb, d_skip.reshape(-1, 1, w), b_glu.reshape(-1, 1, w), w_glu, w_a, w_b)


def _out_kernel(m_ref, wo32_ref, x_ref, g_ref, *refs, emit_x, sub, w_steps):
    o_refs, wo_ref = refs[:-1], refs[-1]
    step = pl.program_id(0)
    wr = wo32_ref.shape[0]

    @pl.when(step < w_steps)
    def _():
        r0 = pl.multiple_of(step * wr, wr)
        wo_ref[pl.ds(r0, wr), :] = wo32_ref[...].astype(BF16)

    @pl.when(step >= w_steps)
    def _():
        rs = m_ref.shape[0] // sub
        for r in range(sub):
            rows = slice(r * rs, (r + 1) * rs)
            xn = x_ref[rows, :] + jnp.dot(m_ref[rows, :], wo_ref[...], preferred_element_type=F32)
            ms = jnp.mean(xn * xn, axis=-1, keepdims=True)
            normed = xn * lax.rsqrt(ms + RMS_EPS) * g_ref[...]
            if emit_x:
                o_refs[0][rows, :] = xn
                o_refs[1][rows, :] = normed.astype(o_refs[1].dtype)
            else:
                o_refs[0][rows, :] = normed.astype(o_refs[0].dtype)


def _out_proj(m, w_o, layer, x, g, emit_x, tm=512, sub=2, w_steps=2):
    n, d = x.shape
    row = pl.BlockSpec((tm, d), lambda s: (jnp.maximum(s - w_steps, 0), 0))
    if emit_x:
        out_specs = [row, row]
        out_shape = [jax.ShapeDtypeStruct((n, d), F32), jax.ShapeDtypeStruct((n, d), BF16)]
    else:
        out_specs = row
        out_shape = jax.ShapeDtypeStruct((n, d), F32)
    w_spec = pl.BlockSpec((None, d // w_steps, d), lambda s: (layer, jnp.minimum(s, w_steps - 1), 0))
    return pl.pallas_call(
        functools.partial(_out_kernel, emit_x=emit_x, sub=sub, w_steps=w_steps),
        grid=(w_steps + n // tm,),
        in_specs=[row, w_spec, row, pl.BlockSpec((1, d), lambda s: (0, 0))],
        out_specs=out_specs,
        out_shape=out_shape,
        scratch_shapes=[pltpu.VMEM((d, d), BF16)],
        compiler_params=_params("arbitrary"),
        name="out_proj_residual_norm",
    )(m, w_o, x, g.reshape(1, d))


def kernel(x, norm_g, w_in, conv_w, w_out_a, a_re, a_im, log_dt, b_re, b_im, c_re, c_im,
           d_skip, w_glu, b_glu, w_out_b, w_o, final_g):
    bsz, seq, d = x.shape
    depth = norm_g.shape[0]
    n = bsz * seq
    n_chunks = seq // SSM_CHUNK
    assert 2 * bsz == SUBLANES and n_chunks % 2 == 0
    xf = x.reshape(n, d)
    h = _rmsnorm(xf, norm_g[0])
    ssm_ops = _ssm_weights(a_re, a_im, log_dt, b_re, b_im, c_re, c_im)
    for l in range(depth):
        a_in, sga, sgb = _in_proj_a(h, w_in, conv_w, l, seq)
        u, szb = _in_proj_b(h, w_in, l)
        y = _ssm(u.reshape(bsz, seq, SSM_WIDTH), *ssm_ops, l).reshape(n, SSM_WIDTH)
        m = _tail(y, u, szb, a_in, sga, sgb, d_skip, b_glu, w_glu, w_out_a, w_out_b, l)
        if l + 1 < depth:
            xf, h = _out_proj(m, w_o, l, xf, norm_g[l + 1], True)
        else:
            out = _out_proj(m, w_o, l, xf, final_g, False)
    return out.reshape(bsz, seq, d)
```
